```python
import math
import jax, jax.numpy as jnp
from jax import lax
import numpy as np

D_MODEL = 2048
BATCH = 1
SEQ = 8192
DEPTH = 4
DEC_BATCH = 16
DEC_SEQ = 2048
PAST_LEN = 128

HEAD_DIM = 128
A_HEADS = 8
A_CONFIGS = ((128, 1), (512, 4), (2048, 16))
A_BRANCHES = len(A_CONFIGS)
A_IN = 3 * A_BRANCHES * A_HEADS * HEAD_DIM
B_HEADS = 8
B_QK_DIM = 64
B_V_DIM = 2 * B_QK_DIM
B_Q = B_HEADS * 2 * B_QK_DIM
B_IN = 2 * B_Q + B_HEADS * B_V_DIM
C_HEADS = 12
C_Q_LORA = 512
C_KV_LORA = 512
C_NOPE = 128
C_ROPE = 64
C_V = 128
ROPE_THETA = 10000.0
D_HEADS = 4
GRID_W = 64
NA_ROWS_MAX = 8
NA_COLS = 16
REL_BUCKETS = 32
REL_MAX_DIST = 1024
REL_HEADS = A_BRANCHES * A_HEADS + B_HEADS
D_FF = 5632
EVEN_IN = A_IN + B_IN
EVEN_OUT = A_HEADS * HEAD_DIM + B_HEADS * B_V_DIM
ODD_IN = C_Q_LORA + C_KV_LORA + C_ROPE + 3 * D_HEADS * HEAD_DIM
ODD_OUT = C_HEADS * C_V + D_HEADS * HEAD_DIM
N_EVEN = (DEPTH + 1) // 2
N_ODD = DEPTH // 2
Q_BLOCK = 128
EPS = 1e-6
NEG = -1e30
F32 = jnp.float32

kernel_name = 'hybrid_bidir_encoder_dilated_diff_mla_na'


def rmsnorm(x, g):
    xf = x.astype(F32)
    y = xf * lax.rsqrt(jnp.mean(xf * xf, axis=-1, keepdims=True) + EPS)
    return (y * g.astype(F32)).astype(x.dtype)


def swiglu(x, w_in, w_out):
    gate, up = jnp.split(x @ w_in, 2, axis=-1)
    return (jax.nn.silu(gate) * up) @ w_out


def t5_bucket(rel):
    nb = REL_BUCKETS // 2
    max_exact = nb // 2
    base = jnp.where(rel > 0, nb, 0)
    n = jnp.abs(rel)
    nf = jnp.maximum(n, 1).astype(F32)
    large = max_exact + (jnp.log(nf / max_exact) / math.log(REL_MAX_DIST / max_exact)
                         * (nb - max_exact)).astype(jnp.int32)
    large = jnp.minimum(large, nb - 1)
    return base + jnp.where(n < max_exact, n, large)


def apply_rope(x):
    T, d = x.shape[1], x.shape[-1]
    inv = ROPE_THETA ** (-jnp.arange(0, d, 2, dtype=F32) / d)
    ang = jnp.arange(T, dtype=F32)[:, None] * inv[None, :]
    shape = (1, T) + (1,) * (x.ndim - 3) + (d // 2,)
    cos, sin = jnp.cos(ang).reshape(shape), jnp.sin(ang).reshape(shape)
    x1, x2 = jnp.split(x.astype(F32), 2, axis=-1)
    return jnp.concatenate([x1 * cos - x2 * sin, x2 * cos + x1 * sin], axis=-1).astype(x.dtype)


def dilated_branch(q, k, v, bias_tab, window, dilation):
    B, T, H, dh = q.shape
    half = window // (2 * dilation)
    blk = half
    L = T // dilation
    nb = -(-L // blk)
    Lp = nb * blk

    def to_sub(a):
        a = a.reshape(B, L, dilation, H, dh).transpose(0, 2, 1, 3, 4)
        return jnp.pad(a, ((0, 0), (0, 0), (0, Lp - L), (0, 0), (0, 0)))

    def key_blocks(a):
        ap = jnp.pad(to_sub(a), ((0, 0), (0, 0), (blk, blk), (0, 0), (0, 0)))
        ap = ap.reshape(B, dilation, nb + 2, blk, H, dh)
        return jnp.concatenate([ap[:, :, :-2], ap[:, :, 1:-1], ap[:, :, 2:]], axis=3)

    qs = to_sub(q).reshape(B, dilation, nb, blk, H, dh)
    kb, vb = key_blocks(k), key_blocks(v)
    qi = jnp.arange(blk)[:, None]
    ki = jnp.arange(3 * blk)[None, :]
    delta = ki - blk - qi
    key_idx = jnp.arange(nb)[:, None, None] * blk + (ki - blk)[None]
    valid = (jnp.abs(delta) <= half)[None] & (key_idx >= 0) & (key_idx < L)
    bias = jnp.transpose(bias_tab[t5_bucket(dilation * delta)], (2, 0, 1))
    s = jnp.einsum('brnqhd,brnkhd->brnhqk', qs, kb, preferred_element_type=F32) * (dh ** -0.5) + bias
    s = jnp.where(valid[None, None, :, None], s, NEG)
    m = jnp.max(s, axis=-1, keepdims=True)
    e = jnp.exp(s - m)
    den = jnp.sum(e, axis=-1, keepdims=True)
    o = jnp.einsum('brnhqk,brnkhd->brnqhd', (e / den).astype(v.dtype), vb)
    lse = (m + jnp.log(den))[..., 0]
    o = o.reshape(B, dilation, Lp, H, dh)[:, :, :L].transpose(0, 2, 1, 3, 4).reshape(B, T, H, dh)
    lse = lse.transpose(0, 1, 2, 4, 3).reshape(B, dilation, Lp, H)[:, :, :L]
    lse = lse.transpose(0, 2, 1, 3).reshape(B, T, H)
    return o, lse


def diff_attention(q, k, v, bias_tab, lam, subln_g, lambda_init):
    B, T, H, _, dk = q.shape
    dv = v.shape[-1]
    nb = T // Q_BLOCK
    qb = jnp.moveaxis(q.reshape(B, nb, Q_BLOCK, H, 2, dk), 1, 0)
    starts = jnp.arange(nb) * Q_BLOCK
    kpos = jnp.arange(T)

    def one(args):
        qblk, s0 = args
        qpos = s0 + jnp.arange(Q_BLOCK)
        bias = jnp.transpose(bias_tab[t5_bucket(kpos[None, :] - qpos[:, None])], (2, 0, 1))
        s = jnp.einsum('bqhcd,bkhcd->bhcqk', qblk, k, preferred_element_type=F32) * (dk ** -0.5)
        p = jax.nn.softmax(s + bias[None, :, None], axis=-1)
        a = p[:, :, 0] - lam * p[:, :, 1]
        return jnp.einsum('bhqk,bkhd->bqhd', a.astype(v.dtype), v)

    o = lax.map(one, (qb, starts))
    o = jnp.moveaxis(o, 0, 1).reshape(B, T, H, dv)
    return rmsnorm(o, subln_g) * (1.0 - lambda_init)


def mla_attention(q_nope, q_rope, k_nope, k_rope, v):
    B, T, H, _ = q_nope.shape
    nb = T // Q_BLOCK
    scale = (C_NOPE + C_ROPE) ** -0.5

    def blocks(a):
        return jnp.moveaxis(a.reshape((B, nb, Q_BLOCK) + a.shape[2:]), 1, 0)

    def one(args):
        qn, qr = args
        s = (jnp.einsum('bqhd,bkhd->bhqk', qn, k_nope, preferred_element_type=F32)
             + jnp.einsum('bqhd,bkd->bhqk', qr, k_rope, preferred_element_type=F32)) * scale
        p = jax.nn.softmax(s, axis=-1)
        return jnp.einsum('bhqk,bkhd->bqhd', p.astype(v.dtype), v)

    o = lax.map(one, (blocks(q_nope), blocks(q_rope)))
    return jnp.moveaxis(o, 0, 1).reshape(B, T, H, v.shape[-1])


def neighbourhood_attention(q, k, v, rpb):
    B, T, H, dh = q.shape
    rows = T // GRID_W
    kr = min(NA_ROWS_MAX, rows)
    r = jnp.arange(rows)
    ridx = jnp.clip(r - kr // 2, 0, rows - kr)[:, None] + jnp.arange(kr)[None, :]
    qg = q.reshape(B, rows, GRID_W, H, dh)
    kg = jnp.take(k.reshape(B, rows, GRID_W, H, dh), ridx, axis=1)
    vg = jnp.take(v.reshape(B, rows, GRID_W, H, dh), ridx, axis=1)
    s = jnp.einsum('brqhd,brkchd->bhrqkc', qg, kg, preferred_element_type=F32) * (dh ** -0.5)
    col = jnp.arange(GRID_W)
    cstart = jnp.clip(col - NA_COLS // 2, 0, GRID_W - NA_COLS)
    cmask = (col[None, :] >= cstart[:, None]) & (col[None, :] < cstart[:, None] + NA_COLS)
    dc = jnp.clip(col[None, :] - col[:, None], -(NA_COLS - 1), NA_COLS - 1)
    dr = ridx - r[:, None]
    bias = rpb[:, dr[:, None, :, None] + (NA_ROWS_MAX - 1),
               dc[None, :, None, :] + (NA_COLS - 1)]
    s = jnp.where(cmask[:, None, :], s + bias[None], NEG)
    p = jax.nn.softmax(s, axis=(-2, -1))
    o = jnp.einsum('bhrqkc,brkchd->brqhd', p.astype(v.dtype), vg)
    return o.reshape(B, T, H, dh)


def even_mixer(h, w_in, w_out, lam, subln_g, rel_bias, lambda_init):
    B, T, _ = h.shape
    proj = h @ w_in
    qkv_a = proj[..., :A_IN].reshape(B, T, 3, A_BRANCHES, A_HEADS, HEAD_DIM)
    outs, lses = [], []
    for g, (win, dil) in enumerate(A_CONFIGS):
        o, l = dilated_branch(qkv_a[:, :, 0, g], qkv_a[:, :, 1, g], qkv_a[:, :, 2, g],
                              rel_bias[:, g * A_HEADS:(g + 1) * A_HEADS], win, dil)
        outs.append(o)
        lses.append(l)
    wts = jax.nn.softmax(jnp.stack(lses, axis=2), axis=2)
    o_a = jnp.einsum('btgh,btghd->bthd', wts.astype(h.dtype), jnp.stack(outs, axis=2))
    qb = proj[..., A_IN:A_IN + B_Q].reshape(B, T, B_HEADS, 2, B_QK_DIM)
    kb = proj[..., A_IN + B_Q:A_IN + 2 * B_Q].reshape(B, T, B_HEADS, 2, B_QK_DIM)
    vb = proj[..., A_IN + 2 * B_Q:].reshape(B, T, B_HEADS, B_V_DIM)
    lf = lam.astype(F32)
    lam_full = jnp.exp(jnp.sum(lf[0] * lf[1])) - jnp.exp(jnp.sum(lf[2] * lf[3])) + lambda_init
    o_b = diff_attention(qb, kb, vb, rel_bias[:, A_BRANCHES * A_HEADS:], lam_full, subln_g, lambda_init)
    return jnp.concatenate([o_a.reshape(B, T, -1), o_b.reshape(B, T, -1)], axis=-1) @ w_out


def odd_mixer(h, w_in, q_norm_g, kv_norm_g, w_q_up, w_kv_up, rpb, w_out):
    B, T, _ = h.shape
    proj = h @ w_in
    o1 = C_Q_LORA
    o2 = o1 + C_KV_LORA
    o3 = o2 + C_ROPE
    cq = rmsnorm(proj[..., :o1], q_norm_g)
    ckv = rmsnorm(proj[..., o1:o2], kv_norm_g)
    k_rope = apply_rope(proj[..., o2:o3])
    qc = (cq @ w_q_up).reshape(B, T, C_HEADS, C_NOPE + C_ROPE)
    kv = (ckv @ w_kv_up).reshape(B, T, C_HEADS, C_NOPE + C_V)
    o_c = mla_attention(qc[..., :C_NOPE], apply_rope(qc[..., C_NOPE:]), kv[..., :C_NOPE], k_rope, kv[..., C_NOPE:])
    qkv_d = proj[..., o3:].reshape(B, T, 3, D_HEADS, HEAD_DIM)
    o_d = neighbourhood_attention(qkv_d[:, :, 0], qkv_d[:, :, 1], qkv_d[:, :, 2], rpb)
    return jnp.concatenate([o_c.reshape(B, T, -1), o_d.reshape(B, T, -1)], axis=-1) @ w_out


def trunk(x, c, ada_w, ada_b, norm_g, ffn_w_in, ffn_w_out, rel_bias, ev_w_in, ev_w_out,
          diff_lambda, diff_subln_g, od_w_in, mla_q_norm_g, mla_kv_norm_g, mla_w_q_up,
          mla_w_kv_up, na_rpb, od_w_out, final_norm_g):
    B = x.shape[0]
    c_act = jax.nn.silu(c)
    for i in range(DEPTH):
        mod = (c_act @ ada_w[i] + ada_b[i]).reshape(B, 3, 3, D_MODEL)

        def pre(xx, s):
            return rmsnorm(xx, norm_g[i, s]) * (1.0 + mod[:, s, 1][:, None]) + mod[:, s, 0][:, None]

        x = x + 0.5 * mod[:, 0, 2][:, None] * swiglu(pre(x, 0), ffn_w_in[i, 0], ffn_w_out[i, 0])
        j = i // 2
        if i % 2 == 0:
            mix = even_mixer(pre(x, 1), ev_w_in[j], ev_w_out[j], diff_lambda[j], diff_subln_g[j],
                             rel_bias, 0.8 - 0.6 * math.exp(-0.3 * i))
        else:
            mix = odd_mixer(pre(x, 1), od_w_in[j], mla_q_norm_g[j], mla_kv_norm_g[j], mla_w_q_up[j],
                            mla_w_kv_up[j], na_rpb[j], od_w_out[j])
        x = x + mod[:, 1, 2][:, None] * mix
        x = x + 0.5 * mod[:, 2, 2][:, None] * swiglu(pre(x, 2), ffn_w_in[i, 1], ffn_w_out[i, 1])
    return rmsnorm(x, final_norm_g)


def setup_inputs(seed: int = 0) -> dict:
    key = jax.random.key(seed)
    ks = jax.random.split(key, 24)

    def nrm(k, shape, s):
        return jax.random.normal(k, shape, F32) * s

    return {
        'x_prompt': nrm(ks[0], (BATCH, SEQ, D_MODEL), 1.0),
        'x_sample': nrm(ks[1], (DEC_BATCH, DEC_SEQ, D_MODEL), 1.0),
        'c_prompt': nrm(ks[2], (BATCH, D_MODEL), 1.0),
        'c_sample': nrm(ks[3], (DEC_BATCH, D_MODEL), 1.0),
        'ada_w': nrm(ks[4], (DEPTH, D_MODEL, 9 * D_MODEL), D_MODEL ** -0.5),
        'ada_b': nrm(ks[5], (DEPTH, 9 * D_MODEL), 0.02),
        'norm_g': 1.0 + nrm(ks[6], (DEPTH, 3, D_MODEL), 0.02),
        'ffn_w_in': nrm(ks[7], (DEPTH, 2, D_MODEL, 2 * D_FF), D_MODEL ** -0.5),
        'ffn_w_out': nrm(ks[8], (DEPTH, 2, D_FF, D_MODEL), D_FF ** -0.5),
        'rel_bias': nrm(ks[9], (REL_BUCKETS, REL_HEADS), 0.1),
        'ev_w_in': nrm(ks[10], (N_EVEN, D_MODEL, EVEN_IN), D_MODEL ** -0.5),
        'ev_w_out': nrm(ks[11], (N_EVEN, EVEN_OUT, D_MODEL), EVEN_OUT ** -0.5),
        'diff_lambda': nrm(ks[12], (N_EVEN, 4, B_QK_DIM), 0.1),
        'diff_subln_g': 1.0 + nrm(ks[13], (N_EVEN, B_V_DIM), 0.02),
        'od_w_in': nrm(ks[14], (N_ODD, D_MODEL, ODD_IN), D_MODEL ** -0.5),
        'mla_q_norm_g': 1.0 + nrm(ks[15], (N_ODD, C_Q_LORA), 0.02),
        'mla_kv_norm_g': 1.0 + nrm(ks[16], (N_ODD, C_KV_LORA), 0.02),
        'mla_w_q_up': nrm(ks[17], (N_ODD, C_Q_LORA, C_HEADS * (C_NOPE + C_ROPE)), C_Q_LORA ** -0.5),
        'mla_w_kv_up': nrm(ks[18], (N_ODD, C_KV_LORA, C_HEADS * (C_NOPE + C_V)), C_KV_LORA ** -0.5),
        'na_rpb': nrm(ks[19], (N_ODD, D_HEADS, 2 * NA_ROWS_MAX - 1, 2 * NA_COLS - 1), 0.1),
        'od_w_out': nrm(ks[20], (N_ODD, ODD_OUT, D_MODEL), ODD_OUT ** -0.5),
        'final_norm_g': 1.0 + nrm(ks[21], (D_MODEL,), 0.02),
    }


def reference(x_prompt, x_sample, c_prompt, c_sample, ada_w, ada_b, norm_g, ffn_w_in, ffn_w_out,
              rel_bias, ev_w_in, ev_w_out, diff_lambda, diff_subln_g, od_w_in, mla_q_norm_g,
              mla_kv_norm_g, mla_w_q_up, mla_w_kv_up, na_rpb, od_w_out, final_norm_g):
    y_prompt = trunk(x_prompt, c_prompt, ada_w, ada_b, norm_g, ffn_w_in, ffn_w_out, rel_bias,
                     ev_w_in, ev_w_out, diff_lambda, diff_subln_g, od_w_in, mla_q_norm_g,
                     mla_kv_norm_g, mla_w_q_up, mla_w_kv_up, na_rpb, od_w_out, final_norm_g)
    y_sample = trunk(x_sample, c_sample, ada_w, ada_b, norm_g, ffn_w_in, ffn_w_out, rel_bias,
                     ev_w_in, ev_w_out, diff_lambda, diff_subln_g, od_w_in, mla_q_norm_g,
                     mla_kv_norm_g, mla_w_q_up, mla_w_kv_up, na_rpb, od_w_out, final_norm_g)
    return (y_prompt, y_sample)
```

```python
import functools
import math
from typing import NamedTuple

import numpy as np
import jax
import jax.numpy as jnp
from jax import lax
from jax.experimental import pallas as pl
from jax.experimental.pallas import tpu as pltpu

F32 = jnp.float32
BF16 = jnp.bfloat16

D_MODEL = 2048
D_FF = 5632
HEAD_DIM = 128
A_HEADS = 8
A_CONFIGS = ((128, 1), (512, 4), (2048, 16))
A_HALF = 64
A_IN = 3 * 3 * A_HEADS * HEAD_DIM
B_HEADS = 8
B_QK_DIM = 64
B_W = B_HEADS * 2 * B_QK_DIM
EVEN_IN = A_IN + 3 * B_W
C_HEADS = 12
C_LORA = 512
C_NOPE = 128
C_ROPE = 64
C_QK_PAD = 256
ROPE_THETA = 10000.0
D_HEADS = 4
GRID_W = 64
NA_ROWS = 8
NA_COLS = 16
NA_BLOCK = NA_ROWS * GRID_W
REL_BUCKETS = 32
REL_MAX_DIST = 1024
EPS = 1e-6
NEG = -1e30

V7X_VMEM_BYTES = 64 * 1024 * 1024
VMEM_LIMIT = V7X_VMEM_BYTES - 8 * 1024 * 1024


class Cfg(NamedTuple):
    ch: int
    pc: int
    sb: int

    @property
    def n(self):
        return self.ch * (self.pc + self.sb)

    def seqs(self):
        out = [(0, self.pc * self.ch)]
        out += [((self.pc + b) * self.ch, self.ch) for b in range(self.sb)]
        return out


def _params(sem):
    return pltpu.CompilerParams(dimension_semantics=sem, vmem_limit_bytes=VMEM_LIMIT)


def _mod_body(c_ref, w_ref, b_ref, o_ref):
    c = c_ref[...]
    act = (c * jax.nn.sigmoid(c)).astype(BF16)
    o_ref[0] = jnp.dot(act, w_ref[0].astype(BF16), preferred_element_type=F32) + b_ref[0]


def _modulation(c_pad, ada_w, ada_b):
    depth, _, nout = ada_w.shape
    r = c_pad.shape[0]
    tn = 1024
    return pl.pallas_call(
        _mod_body,
        grid=(depth, nout // tn),
        in_specs=[pl.BlockSpec((r, D_MODEL), lambda l, j: (0, 0)),
                  pl.BlockSpec((1, D_MODEL, tn), lambda l, j: (l, 0, j)),
                  pl.BlockSpec((1, 1, tn), lambda l, j: (l, 0, j))],
        out_specs=pl.BlockSpec((1, r, tn), lambda l, j: (l, 0, j)),
        out_shape=jax.ShapeDtypeStruct((depth, r, nout), F32),
        compiler_params=_params(("arbitrary", "arbitrary")),
        name="modulation",
    )(c_pad, ada_w, ada_b.reshape(depth, 1, nout))


NORM_ROWS = 256


def _norm_rows(x_ref, g_ref, sh_ref, sc_ref, h_scr, tm):
    def body(r, carry):
        rows = pl.ds(pl.multiple_of(r * NORM_ROWS, NORM_ROWS), NORM_ROWS)
        x = x_ref[rows, :]
        inv = lax.rsqrt(jnp.mean(x * x, axis=-1, keepdims=True) + EPS)
        y = x * inv * g_ref[...]
        h_scr[rows, :] = (y * (1.0 + sc_ref[0]) + sh_ref[0]).astype(BF16)
        return carry
    lax.fori_loop(0, tm // NORM_ROWS, body, 0)


def _norm_mm_body(x_ref, g_ref, sh_ref, sc_ref, w_ref, cs_ref, o_ref, h_scr, *, tm):
    @pl.when(pl.program_id(1) == 0)
    def _():
        _norm_rows(x_ref, g_ref, sh_ref, sc_ref, h_scr, tm)
    acc = jnp.dot(h_scr[...], w_ref[...], preferred_element_type=F32)
    o_ref[...] = (acc * cs_ref[...]).astype(o_ref.dtype)


def _norm_swiglu_body(x_ref, g_ref, sh_ref, sc_ref, wg_ref, wu_ref, o_ref, h_scr, *, tm):
    @pl.when(pl.program_id(1) == 0)
    def _():
        _norm_rows(x_ref, g_ref, sh_ref, sc_ref, h_scr, tm)
    h = h_scr[...]
    gate = jnp.dot(h, wg_ref[...], preferred_element_type=F32)
    up = jnp.dot(h, wu_ref[...], preferred_element_type=F32)
    o_ref[...] = (gate * jax.nn.sigmoid(gate) * up).astype(o_ref.dtype)


def _mod_specs(cfg, tm, sub):
    per = cfg.ch // tm
    shift = pl.BlockSpec((1, 1, D_MODEL), lambda i, j: ((i // per) * 9 + sub * 3, 0, 0))
    scale = pl.BlockSpec((1, 1, D_MODEL), lambda i, j: ((i // per) * 9 + sub * 3 + 1, 0, 0))
    return shift, scale


def _norm_matmul(cfg, x, g, mods, sub, w, col_scale, out_dtype, tm, tn, name):
    n = x.shape[0]
    nout = w.shape[1]
    shift, scale = _mod_specs(cfg, tm, sub)
    return pl.pallas_call(
        functools.partial(_norm_mm_body, tm=tm),
        grid=(n // tm, nout // tn),
        in_specs=[pl.BlockSpec((tm, D_MODEL), lambda i, j: (i, 0)),
                  pl.BlockSpec((1, D_MODEL), lambda i, j: (0, 0)),
                  shift, scale,
                  pl.BlockSpec((D_MODEL, tn), lambda i, j: (0, j)),
                  pl.BlockSpec((1, tn), lambda i, j: (0, j))],
        out_specs=pl.BlockSpec((tm, tn), lambda i, j: (i, j)),
        out_shape=jax.ShapeDtypeStruct((n, nout), out_dtype),
        scratch_shapes=[pltpu.VMEM((tm, D_MODEL), BF16)],
        compiler_params=_params(("arbitrary", "arbitrary")),
        name=name,
    )(x, g.reshape(1, D_MODEL), mods, mods, w, col_scale)


def _norm_swiglu(cfg, x, g, mods, sub, w_in, tm, tn):
    n = x.shape[0]
    nj = D_FF // tn
    shift, scale = _mod_specs(cfg, tm, sub)
    return pl.pallas_call(
        functools.partial(_norm_swiglu_body, tm=tm),
        grid=(n // tm, nj),
        in_specs=[pl.BlockSpec((tm, D_MODEL), lambda i, j: (i, 0)),
                  pl.BlockSpec((1, D_MODEL), lambda i, j: (0, 0)),
                  shift, scale,
                  pl.BlockSpec((D_MODEL, tn), lambda i, j: (0, j)),
                  pl.BlockSpec((D_MODEL, tn), lambda i, j: (0, j + nj))],
        out_specs=pl.BlockSpec((tm, tn), lambda i, j: (i, j)),
        out_shape=jax.ShapeDtypeStruct((n, D_FF), BF16),
        scratch_shapes=[pltpu.VMEM((tm, D_MODEL), BF16)],
        compiler_params=_params(("arbitrary", "arbitrary")),
        name="ffn_in",
    )(x, g.reshape(1, D_MODEL), mods, mods, w_in, w_in)


def _resid_mm_body(*refs, pieces, coef):
    lhs = refs[:pieces]
    ws = refs[pieces:2 * pieces]
    x_ref, gate_ref, o_ref = refs[2 * pieces:]
    acc = jnp.dot(lhs[0][...], ws[0][...], preferred_element_type=F32)
    for p in range(1, pieces):
        acc = acc + jnp.dot(lhs[p][...], ws[p][...], preferred_element_type=F32)
    o_ref[...] = x_ref[...] + (coef * gate_ref[0]) * acc


def _resid_matmul(cfg, lhs_list, w, x, mods, sub, coef, tm, tn, name):
    n = x.shape[0]
    per = cfg.ch // tm
    widths = [a.shape[1] for a in lhs_list]
    offs = np.cumsum([0] + widths[:-1])
    in_specs = [pl.BlockSpec((tm, k), lambda i, j: (i, 0)) for k in widths]
    for k, off in zip(widths, offs):
        assert off % k == 0
        in_specs.append(pl.BlockSpec((k, tn), lambda i, j, b=int(off // k): (b, j)))
    in_specs += [pl.BlockSpec((tm, tn), lambda i, j: (i, j)),
                 pl.BlockSpec((1, 1, tn), lambda i, j: ((i // per) * 9 + sub * 3 + 2, 0, j))]
    return pl.pallas_call(
        functools.partial(_resid_mm_body, pieces=len(lhs_list), coef=coef),
        grid=(n // tm, D_MODEL // tn),
        in_specs=in_specs,
        out_specs=pl.BlockSpec((tm, tn), lambda i, j: (i, j)),
        out_shape=jax.ShapeDtypeStruct((n, D_MODEL), F32),
        compiler_params=_params(("arbitrary", "arbitrary")),
        name=name,
    )(*lhs_list, *([w] * len(lhs_list)), x, mods)


def _final_norm_body(x_ref, g_ref, o_ref):
    x = x_ref[...]
    inv = lax.rsqrt(jnp.mean(x * x, axis=-1, keepdims=True) + EPS)
    o_ref[...] = x * inv * g_ref[...]


def _final_norm(x, g, tm=512):
    n = x.shape[0]
    return pl.pallas_call(
        _final_norm_body,
        grid=(n // tm,),
        in_specs=[pl.BlockSpec((tm, D_MODEL), lambda i: (i, 0)),
                  pl.BlockSpec((1, D_MODEL), lambda i: (0, 0))],
        out_specs=pl.BlockSpec((tm, D_MODEL), lambda i: (i, 0)),
        out_shape=jax.ShapeDtypeStruct((n, D_MODEL), F32),
        compiler_params=_params(("arbitrary",)),
        name="final_norm",
    )(x, g.reshape(1, D_MODEL))


def _t5_bucket_np(rel):
    nb = REL_BUCKETS // 2
    max_exact = nb // 2
    rel = np.asarray(rel, np.int64)
    base = np.where(rel > 0, nb, 0)
    n = np.abs(rel)
    nf = np.maximum(n, 1).astype(np.float64)
    large = max_exact + (np.log(nf / max_exact) / math.log(REL_MAX_DIST / max_exact)
                         * (nb - max_exact)).astype(np.int64)
    large = np.minimum(large, nb - 1)
    return (base + np.where(n < max_exact, n, large)).astype(np.int32)


DIL_BQ = 128


def _dil_tables(cfg, dil):
    lc = cfg.ch // dil
    nbp = lc // DIL_BQ
    assert lc % DIL_BQ == 0
    per_chunk = cfg.ch // DIL_BQ
    nblk = cfg.n // DIL_BQ
    prev = np.zeros(nblk, np.int32)
    nxt = np.zeros(nblk, np.int32)
    lo = np.zeros(nblk, np.int32)
    hi = np.zeros(nblk, np.int32)
    for i in range(nblk):
        c, blk = divmod(i, per_chunk)
        r, nn = divmod(blk, nbp)
        first_chunk, last_chunk = (0, cfg.pc - 1) if c < cfg.pc else (c, c)
        if nn > 0:
            prev[i], lo[i] = 2 * i - 1, 0
        elif c > first_chunk:
            prev[i], lo[i] = ((c - 1) * cfg.ch + r * lc + lc - A_HALF) // A_HALF, 0
        else:
            prev[i], lo[i] = 2 * i, A_HALF
        if nn < nbp - 1:
            nxt[i], hi[i] = 2 * i + 2, DIL_BQ + 2 * A_HALF
        elif c < last_chunk:
            nxt[i], hi[i] = ((c + 1) * cfg.ch + r * lc) // A_HALF, DIL_BQ + 2 * A_HALF
        else:
            nxt[i], hi[i] = 2 * i + 1, DIL_BQ + A_HALF
    return prev, nxt, lo, hi


def _dil_body(prev_ref, nxt_ref, lo_ref, hi_ref, q_ref, kp_ref, km_ref, kn_ref,
              vp_ref, vm_ref, vn_ref, bias_ref, o_ref, lse_ref):
    i = pl.program_id(0)
    nk = DIL_BQ + 2 * A_HALF
    row = lax.broadcasted_iota(jnp.int32, (DIL_BQ, nk), 0)
    col = lax.broadcasted_iota(jnp.int32, (DIL_BQ, nk), 1)
    off = col - row
    valid = (off >= 0) & (off <= 2 * A_HALF) & (col >= lo_ref[i]) & (col < hi_ref[i])
    k = jnp.concatenate([kp_ref[...], km_ref[...], kn_ref[...]], axis=0)
    v = jnp.concatenate([vp_ref[...], vm_ref[...], vn_ref[...]], axis=0)
    for h in range(A_HEADS):
        cols = slice(h * HEAD_DIM, (h + 1) * HEAD_DIM)
        s = lax.dot_general(q_ref[:, cols], k[:, cols], (((1,), (1,)), ((), ())),
                            preferred_element_type=F32)
        s = jnp.where(valid, s + bias_ref[h], NEG)
        m = jnp.max(s, axis=-1, keepdims=True)
        e = jnp.exp(s - m)
        den = jnp.sum(e, axis=-1, keepdims=True)
        o = jnp.dot(e.astype(BF16), v[:, cols], preferred_element_type=F32)
        o_ref[:, cols] = o / den
        lse_ref[:, cols] = jnp.broadcast_to(m + jnp.log(den), (DIL_BQ, HEAD_DIM))


def _dilated_branch(cfg, dil, q_arr, k_arr, v_arr, qcol, kcol, vcol, bias):
    n = cfg.n
    w = A_HEADS * HEAD_DIM
    prev, nxt, lo, hi = _dil_tables(cfg, dil)
    main = lambda cb: pl.BlockSpec((DIL_BQ, w), lambda i, p, x, l, h: (i, cb))
    before = lambda cb: pl.BlockSpec((A_HALF, w), lambda i, p, x, l, h: (p[i], cb))
    after = lambda cb: pl.BlockSpec((A_HALF, w), lambda i, p, x, l, h: (x[i], cb))
    out_spec = pl.BlockSpec((DIL_BQ, w), lambda i, p, x, l, h: (i, 0))
    grid_spec = pltpu.PrefetchScalarGridSpec(
        num_scalar_prefetch=4,
        grid=(n // DIL_BQ,),
        in_specs=[main(qcol), before(kcol), main(kcol), after(kcol),
                  before(vcol), main(vcol), after(vcol),
                  pl.BlockSpec((A_HEADS, DIL_BQ, DIL_BQ + 2 * A_HALF),
                               lambda i, p, x, l, h: (0, 0, 0))],
        out_specs=[out_spec, out_spec],
    )
    return pl.pallas_call(
        _dil_body,
        grid_spec=grid_spec,
        out_shape=[jax.ShapeDtypeStruct((n, w), F32), jax.ShapeDtypeStruct((n, w), F32)],
        compiler_params=_params(("arbitrary",)),
        name=f"dilated_d{dil}",
    )(jnp.asarray(prev), jnp.asarray(nxt), jnp.asarray(lo), jnp.asarray(hi),
      q_arr, k_arr, k_arr, k_arr, v_arr, v_arr, v_arr, bias)


def _dil_bias(rel_bias, g, dil):
    off = np.arange(DIL_BQ + 2 * A_HALF)[None, :] - np.arange(DIL_BQ)[:, None] - A_HALF
    bucket = _t5_bucket_np(dil * np.clip(off, -A_HALF, A_HALF))
    tab = rel_bias[:, g * A_HEADS:(g + 1) * A_HEADS][bucket]
    return jnp.transpose(tab, (2, 0, 1))


def _merge_body(o1, l1, o2, l2, o3, l3, out_ref):
    a, b, c = l1[...], l2[...], l3[...]
    m = jnp.maximum(jnp.maximum(a, b), c)
    ea, eb, ec = jnp.exp(a - m), jnp.exp(b - m), jnp.exp(c - m)
    tot = ea + eb + ec
    out = (ea / tot) * o1[...] + (eb / tot) * o2[...] + (ec / tot) * o3[...]
    out_ref[...] = out.astype(out_ref.dtype)


def _merge_branches(outs, lses, tm=512):
    n, w = outs[0].shape
    spec = pl.BlockSpec((tm, w), lambda i: (i, 0))
    args = [a for pair in zip(outs, lses) for a in pair]
    return pl.pallas_call(
        _merge_body,
        grid=(n // tm,),
        in_specs=[spec] * 6,
        out_specs=spec,
        out_shape=jax.ShapeDtypeStruct((n, w), BF16),
        compiler_params=_params(("arbitrary",)),
        name="dilated_merge",
    )(*args)


def _to_residue_major(cfg, a, dil):
    n, w = a.shape
    return a.reshape(n // cfg.ch, cfg.ch // dil, dil, w).transpose(0, 2, 1, 3).reshape(n, w)


def _from_residue_major(cfg, a, dil):
    n, w = a.shape
    return a.reshape(n // cfg.ch, dil, cfg.ch // dil, w).transpose(0, 2, 1, 3).reshape(n, w)


def _flash_worklist(cfg, tq, tk, rel_lo=None, rel_hi=None):
    qb, kb, tile, flags = [], [], [], []
    for start, length in cfg.seqs():
        assert length % tq == 0 and length % tk == 0 and start % tq == 0 and start % tk == 0
        nk = length // tk
        for qi in range(length // tq):
            for kj in range(nk):
                qb.append(start // tq + qi)
                kb.append(start // tk + kj)
                if rel_lo is not None:
                    assert (kj * tk) % tq == 0
                    d = (kj * tk - qi * tq) // tq
                    tile.append(min(max(d, rel_lo), rel_hi) - rel_lo)
                else:
                    tile.append(0)
                flags.append((1 if kj == 0 else 0) | (2 if kj == nk - 1 else 0))
    as_i32 = lambda a: jnp.asarray(np.asarray(a, np.int32))
    return as_i32(qb), as_i32(kb), as_i32(tile), as_i32(flags)


DIFF_TQ = 256
DIFF_TK = 512


def _diff_tile_range(max_len):
    rel = np.arange(-max_len + 1, max_len)
    b = _t5_bucket_np(rel)
    sat_pos = int(rel[b != b[-1]].max()) + 1
    sat_neg = int(rel[b != b[0]].min()) - 1
    hi = -(-(sat_pos + DIFF_TQ - 1) // DIFF_TQ)
    lo = (sat_neg - (DIFF_TK - 1)) // DIFF_TQ
    return lo, hi


def _diff_bias_tiles(rel_bias, lo, hi):
    d = np.arange(lo, hi + 1)[:, None, None] * DIFF_TQ
    rel = d + np.arange(DIFF_TK)[None, None, :] - np.arange(DIFF_TQ)[None, :, None]
    tab = rel_bias[:, 3 * A_HEADS:][_t5_bucket_np(rel)]
    return jnp.transpose(tab, (0, 3, 1, 2))


def _diff_body(qb_ref, kb_ref, tile_ref, fl_ref, q_ref, k_ref, v_ref, bias_ref, lam_ref, g_ref,
               o_ref, m_scr, l_scr, acc_scr, *, lam_init):
    step = pl.program_id(0)
    flags = fl_ref[step]
    tq = DIFF_TQ

    @pl.when((flags & 1) != 0)
    def _():
        m_scr[...] = jnp.full(m_scr.shape, -jnp.inf, F32)
        l_scr[...] = jnp.zeros(l_scr.shape, F32)
        acc_scr[...] = jnp.zeros(acc_scr.shape, F32)

    lane = lax.broadcasted_iota(jnp.int32, (tq, HEAD_DIM), 1)
    for h in range(B_HEADS):
        cols = slice(h * HEAD_DIM, (h + 1) * HEAD_DIM)
        qh = q_ref[:, cols]
        zero = jnp.zeros_like(qh)
        q2 = jnp.concatenate([jnp.where(lane < B_QK_DIM, qh, zero),
                              jnp.where(lane >= B_QK_DIM, qh, zero)], axis=0)
        s = lax.dot_general(q2, k_ref[:, cols], (((1,), (1,)), ((), ())),
                            preferred_element_type=F32)
        b = bias_ref[0, h]
        s = s + jnp.concatenate([b, b], axis=0)
        m_prev = m_scr[h]
        m_new = jnp.maximum(m_prev, jnp.max(s, axis=-1, keepdims=True))
        alpha = jnp.exp(m_prev - m_new)
        p = jnp.exp(s - m_new)
        l_scr[h] = alpha * l_scr[h] + jnp.sum(p, axis=-1, keepdims=True)
        acc_scr[h] = alpha * acc_scr[h] + jnp.dot(p.astype(BF16), v_ref[:, cols],
                                                  preferred_element_type=F32)
        m_scr[h] = m_new

    @pl.when((flags & 2) != 0)
    def _():
        lf = lam_ref[...]
        lam = (jnp.exp(jnp.sum(lf[0:1] * lf[1:2], axis=-1, keepdims=True))
               - jnp.exp(jnp.sum(lf[2:3] * lf[3:4], axis=-1, keepdims=True)) + lam_init)
        for h in range(B_HEADS):
            cols = slice(h * HEAD_DIM, (h + 1) * HEAD_DIM)
            acc = acc_scr[h]
            l = l_scr[h]
            o = acc[:tq] / l[:tq] - lam * (acc[tq:] / l[tq:])
            inv = lax.rsqrt(jnp.mean(o * o, axis=-1, keepdims=True) + EPS)
            o_ref[:, cols] = ((o * inv * g_ref[...]) * (1.0 - lam_init)).astype(o_ref.dtype)


def _diff_attention(cfg, proj, qcol, kcol, vcol, bias_tiles, tile_lo, tile_hi, lam, subln_g,
                    lam_init):
    n = cfg.n
    tq, tk = DIFF_TQ, DIFF_TK
    qb, kb, tile, flags = _flash_worklist(cfg, tq, tk, tile_lo, tile_hi)
    grid_spec = pltpu.PrefetchScalarGridSpec(
        num_scalar_prefetch=4,
        grid=(qb.shape[0],),
        in_specs=[pl.BlockSpec((tq, B_W), lambda s, q, k, t, f: (q[s], qcol)),
                  pl.BlockSpec((tk, B_W), lambda s, q, k, t, f: (k[s], kcol)),
                  pl.BlockSpec((tk, B_W), lambda s, q, k, t, f: (k[s], vcol)),
                  pl.BlockSpec((1, B_HEADS, tq, tk), lambda s, q, k, t, f: (t[s], 0, 0, 0)),
                  pl.BlockSpec((4, B_QK_DIM), lambda s, q, k, t, f: (0, 0)),
                  pl.BlockSpec((1, HEAD_DIM), lambda s, q, k, t, f: (0, 0))],
        out_specs=pl.BlockSpec((tq, B_W), lambda s, q, k, t, f: (q[s], 0)),
        scratch_shapes=[pltpu.VMEM((B_HEADS, 2 * tq, 1), F32),
                        pltpu.VMEM((B_HEADS, 2 * tq, 1), F32),
                        pltpu.VMEM((B_HEADS, 2 * tq, HEAD_DIM), F32)],
    )
    return pl.pallas_call(
        functools.partial(_diff_body, lam_init=lam_init),
        grid_spec=grid_spec,
        out_shape=jax.ShapeDtypeStruct((n, B_W), BF16),
        compiler_params=_params(("arbitrary",)),
        name="diff_attention",
    )(qb, kb, tile, flags, proj, proj, proj, bias_tiles, lam, subln_g.reshape(1, HEAD_DIM))


MLA_TM = 512
MLA_TQ = 512
MLA_TK = 512


def _rope_tables(max_len):
    inv = ROPE_THETA ** (-jnp.arange(0, C_ROPE, 2, dtype=F32) / C_ROPE)
    ang = jnp.arange(max_len, dtype=F32)[:, None] * inv[None, :]
    cos, sin = jnp.cos(ang), jnp.sin(ang)
    zero = jnp.zeros((max_len, HEAD_DIM - C_ROPE), F32)
    return (jnp.concatenate([cos, cos, zero], axis=1),
            jnp.concatenate([-sin, sin, zero], axis=1))


def _mla_prep_body(p_ref, gq_ref, gkv_ref, wqa_ref, wqb_ref, wk_ref, wv_ref, cos_ref, sin_ref,
                   q_ref, k_ref, v_ref, *, scale):
    def normed(x, g):
        inv = lax.rsqrt(jnp.mean(x * x, axis=-1, keepdims=True) + EPS)
        return (x * inv * g).astype(BF16)

    cq = normed(p_ref[:, 0:C_LORA], gq_ref[...])
    ckv = normed(p_ref[:, C_LORA:2 * C_LORA], gkv_ref[...])
    cos = cos_ref[...]
    sin = sin_ref[...]
    qa = jnp.dot(cq, wqa_ref[...], preferred_element_type=F32)
    qb = jnp.dot(cq, wqb_ref[...], preferred_element_type=F32)
    kn = jnp.dot(ckv, wk_ref[...], preferred_element_type=F32)
    v_ref[...] = jnp.dot(ckv, wv_ref[...], preferred_element_type=F32).astype(BF16)
    kr = p_ref[:, 2 * C_LORA:2 * C_LORA + HEAD_DIM]
    kr_sw = p_ref[:, 2 * C_LORA + HEAD_DIM:2 * C_LORA + 2 * HEAD_DIM]
    k_rope = (kr * cos + kr_sw * sin).astype(BF16)
    for h in range(C_HEADS):
        a0 = h * C_QK_PAD
        a1 = a0 + HEAD_DIM
        a2 = a0 + C_QK_PAD
        b = slice(h * HEAD_DIM, (h + 1) * HEAD_DIM)
        q_ref[:, a0:a1] = (qa[:, a0:a1] * scale).astype(BF16)
        q_ref[:, a1:a2] = ((qa[:, a1:a2] * cos + qb[:, b] * sin) * scale).astype(BF16)
        k_ref[:, a0:a1] = kn[:, b].astype(BF16)
        k_ref[:, a1:a2] = k_rope


def _mla_prep(cfg, proj_c, gq, gkv, wqa, wqb, wk, wv, cos_tab, sin_tab):
    n = cfg.n
    tm = MLA_TM
    pblocks = cfg.pc * cfg.ch // tm
    per = cfg.ch // tm
    pos = lambda i: (jnp.where(i < pblocks, i, (i - pblocks) % per), 0)
    full = lambda shape: pl.BlockSpec(shape, lambda i: (0, 0))
    qk_w = C_HEADS * C_QK_PAD
    v_w = C_HEADS * HEAD_DIM
    return pl.pallas_call(
        functools.partial(_mla_prep_body, scale=(C_NOPE + C_ROPE) ** -0.5),
        grid=(n // tm,),
        in_specs=[pl.BlockSpec((tm, proj_c.shape[1]), lambda i: (i, 0)),
                  full((1, C_LORA)), full((1, C_LORA)),
                  full(wqa.shape), full(wqb.shape), full(wk.shape), full(wv.shape),
                  pl.BlockSpec((tm, HEAD_DIM), pos), pl.BlockSpec((tm, HEAD_DIM), pos)],
        out_specs=[pl.BlockSpec((tm, qk_w), lambda i: (i, 0)),
                   pl.BlockSpec((tm, qk_w), lambda i: (i, 0)),
                   pl.BlockSpec((tm, v_w), lambda i: (i, 0))],
        out_shape=[jax.ShapeDtypeStruct((n, qk_w), BF16),
                   jax.ShapeDtypeStruct((n, qk_w), BF16),
                   jax.ShapeDtypeStruct((n, v_w), BF16)],
        compiler_params=_params(("arbitrary",)),
        name="mla_prep",
    )(proj_c, gq.reshape(1, C_LORA), gkv.reshape(1, C_LORA), wqa, wqb, wk, wv, cos_tab, sin_tab)


def _mla_body(qb_ref, kb_ref, tile_ref, fl_ref, q_ref, k_ref, v_ref, o_ref, m_scr, l_scr, acc_scr):
    flags = fl_ref[pl.program_id(0)]

    @pl.when((flags & 1) != 0)
    def _():
        m_scr[...] = jnp.full(m_scr.shape, -jnp.inf, F32)
        l_scr[...] = jnp.zeros(l_scr.shape, F32)
        acc_scr[...] = jnp.zeros(acc_scr.shape, F32)

    for h in range(C_HEADS):
        qk = slice(h * C_QK_PAD, (h + 1) * C_QK_PAD)
        vc = slice(h * HEAD_DIM, (h + 1) * HEAD_DIM)
        s = lax.dot_general(q_ref[:, qk], k_ref[:, qk], (((1,), (1,)), ((), ())),
                            preferred_element_type=F32)
        m_prev = m_scr[h]
        m_new = jnp.maximum(m_prev, jnp.max(s, axis=-1, keepdims=True))
        alpha = jnp.exp(m_prev - m_new)
        p = jnp.exp(s - m_new)
        l_scr[h] = alpha * l_scr[h] + jnp.sum(p, axis=-1, keepdims=True)
        acc_scr[h] = alpha * acc_scr[h] + jnp.dot(p.astype(BF16), v_ref[:, vc],
                                                  preferred_element_type=F32)
        m_scr[h] = m_new

    @pl.when((flags & 2) != 0)
    def _():
        for h in range(C_HEADS):
            vc = slice(h * HEAD_DIM, (h + 1) * HEAD_DIM)
            o_ref[:, vc] = (acc_scr[h] / l_scr[h]).astype(o_ref.dtype)


def _mla_attention(cfg, q, k, v):
    n = cfg.n
    tq, tk = MLA_TQ, MLA_TK
    qb, kb, tile, flags = _flash_worklist(cfg, tq, tk)
    qk_w = C_HEADS * C_QK_PAD
    v_w = C_HEADS * HEAD_DIM
    grid_spec = pltpu.PrefetchScalarGridSpec(
        num_scalar_prefetch=4,
        grid=(qb.shape[0],),
        in_specs=[pl.BlockSpec((tq, qk_w), lambda s, q_, k_, t, f: (q_[s], 0)),
                  pl.BlockSpec((tk, qk_w), lambda s, q_, k_, t, f: (k_[s], 0)),
                  pl.BlockSpec((tk, v_w), lambda s, q_, k_, t, f: (k_[s], 0))],
        out_specs=pl.BlockSpec((tq, v_w), lambda s, q_, k_, t, f: (q_[s], 0)),
        scratch_shapes=[pltpu.VMEM((C_HEADS, tq, 1), F32),
                        pltpu.VMEM((C_HEADS, tq, 1), F32),
                        pltpu.VMEM((C_HEADS, tq, HEAD_DIM), F32)],
    )
    return pl.pallas_call(
        _mla_body,
        grid_spec=grid_spec,
        out_shape=jax.ShapeDtypeStruct((n, v_w), BF16),
        compiler_params=_params(("arbitrary",)),
        name="mla_attention",
    )(qb, kb, tile, flags, q, k, v)


def _na_tables(cfg):
    prev, nxt, var = [], [], []
    for start, length in cfg.seqs():
        nb = length // NA_BLOCK
        assert length % NA_BLOCK == 0 and start % NA_BLOCK == 0 and nb >= 3
        b0 = start // NA_BLOCK
        for r in range(nb):
            prev.append(b0 + max(r - 1, 0))
            nxt.append(b0 + min(r + 1, nb - 1))
            var.append(0 if r == 0 else (2 if r == nb - 1 else 1))
    as_i32 = lambda a: jnp.asarray(np.asarray(a, np.int32))
    return as_i32(prev), as_i32(nxt), as_i32(var)


def _na_bias_tables(rpb):
    a = np.arange(NA_ROWS)[:, None, None, None]
    c = np.arange(GRID_W)[None, :, None, None]
    bb = np.arange(3 * NA_ROWS)[None, None, :, None]
    kc = np.arange(GRID_W)[None, None, None, :]
    cstart = np.clip(c - NA_COLS // 2, 0, GRID_W - NA_COLS)
    col_ok = (kc >= cstart) & (kc < cstart + NA_COLS)
    dc = np.clip(kc - c, -(NA_COLS - 1), NA_COLS - 1) + (NA_COLS - 1)
    dr = bb - NA_ROWS - a
    starts = [np.maximum(a - NA_ROWS // 2, 0),
              a - NA_ROWS // 2,
              np.minimum(a - NA_ROWS // 2, 0)]
    tabs = []
    shape = (NA_ROWS, GRID_W, 3 * NA_ROWS, GRID_W)
    for s in starts:
        krow = bb - NA_ROWS
        row_ok = (krow >= s) & (krow < s + NA_ROWS)
        ok = np.broadcast_to(row_ok & col_ok, shape).reshape(NA_BLOCK, 3 * NA_BLOCK)
        ri = np.broadcast_to(np.clip(dr + NA_ROWS - 1, 0, 2 * NA_ROWS - 2), shape)
        ci = np.broadcast_to(dc, shape)
        vals = rpb[:, ri.reshape(NA_BLOCK, -1), ci.reshape(NA_BLOCK, -1)]
        tabs.append(jnp.where(ok[None], vals, NEG))
    return jnp.stack(tabs, axis=0)


def _na_body(prev_ref, nxt_ref, var_ref, q_ref, kp_ref, km_ref, kn_ref, vp_ref, vm_ref, vn_ref,
             tab_ref, o_ref):
    k = jnp.concatenate([kp_ref[...], km_ref[...], kn_ref[...]], axis=0)
    v = jnp.concatenate([vp_ref[...], vm_ref[...], vn_ref[...]], axis=0)
    for h in range(D_HEADS):
        cols = slice(h * HEAD_DIM, (h + 1) * HEAD_DIM)
        s = lax.dot_general(q_ref[:, cols], k[:, cols], (((1,), (1,)), ((), ())),
                            preferred_element_type=F32)
        s = s + tab_ref[0, h]
        m = jnp.max(s, axis=-1, keepdims=True)
        e = jnp.exp(s - m)
        den = jnp.sum(e, axis=-1, keepdims=True)
        o = jnp.dot(e.astype(BF16), v[:, cols], preferred_element_type=F32)
        o_ref[:, cols] = (o / den).astype(o_ref.dtype)


def _na_attention(cfg, qkv, tabs):
    n = cfg.n
    w = D_HEADS * HEAD_DIM
    prev, nxt, var = _na_tables(cfg)
    own = lambda cb: pl.BlockSpec((NA_BLOCK, w), lambda i, p, x, t: (i, cb))
    before = lambda cb: pl.BlockSpec((NA_BLOCK, w), lambda i, p, x, t: (p[i], cb))
    after = lambda cb: pl.BlockSpec((NA_BLOCK, w), lambda i, p, x, t: (x[i], cb))
    grid_spec = pltpu.PrefetchScalarGridSpec(
        num_scalar_prefetch=3,
        grid=(n // NA_BLOCK,),
        in_specs=[own(0), before(1), own(1), after(1), before(2), own(2), after(2),
                  pl.BlockSpec((1, D_HEADS, NA_BLOCK, 3 * NA_BLOCK),
                               lambda i, p, x, t: (t[i], 0, 0, 0))],
        out_specs=pl.BlockSpec((NA_BLOCK, w), lambda i, p, x, t: (i, 0)),
    )
    return pl.pallas_call(
        _na_body,
        grid_spec=grid_spec,
        out_shape=jax.ShapeDtypeStruct((n, w), BF16),
        compiler_params=_params(("arbitrary",)),
        name="na_attention",
    )(prev, nxt, var, qkv, qkv, qkv, qkv, qkv, qkv, qkv, tabs)


def _even_col_scale():
    cs = np.ones((1, EVEN_IN), np.float32)
    cs[0, :3 * A_HEADS * HEAD_DIM] = HEAD_DIM ** -0.5
    cs[0, A_IN:A_IN + B_W] = B_QK_DIM ** -0.5
    return jnp.asarray(cs)


def _even_mixer(cfg, x, g, mods, w_in, w_out, lam, subln_g, rel_bias, lam_init, tm):
    w = A_HEADS * HEAD_DIM
    proj = _norm_matmul(cfg, x, g, mods, 1, w_in, _even_col_scale(), BF16, tm, 1024, "even_in")
    outs, lses = [], []
    for gi, (_, dil) in enumerate(A_CONFIGS):
        bias = _dil_bias(rel_bias, gi, dil)
        if dil == 1:
            o, l = _dilated_branch(cfg, dil, proj, proj, proj, gi, 3 + gi, 6 + gi, bias)
        else:
            q, k, v = (_to_residue_major(cfg, proj[:, (s * 3 + gi) * w:(s * 3 + gi + 1) * w], dil)
                       for s in range(3))
            o, l = _dilated_branch(cfg, dil, q, k, v, 0, 0, 0, bias)
            o, l = _from_residue_major(cfg, o, dil), _from_residue_major(cfg, l, dil)
        outs.append(o)
        lses.append(l)
    o_a = _merge_branches(outs, lses)
    tile_lo, tile_hi = _diff_tile_range(cfg.pc * cfg.ch)
    tiles = _diff_bias_tiles(rel_bias, tile_lo, tile_hi)
    cb = A_IN // B_W
    o_b = _diff_attention(cfg, proj, cb, cb + 1, cb + 2, tiles, tile_lo, tile_hi, lam, subln_g,
                          lam_init)
    return _resid_matmul(cfg, [o_a, o_b], w_out, x, mods, 1, 1.0, tm, 512, "even_out")


def _odd_weights(w_in, w_q_up, w_kv_up):
    half = C_ROPE // 2
    swap = np.concatenate([np.arange(half, C_ROPE), np.arange(half)])
    o2 = 2 * C_LORA
    zpad = jnp.zeros((D_MODEL, HEAD_DIM - C_ROPE), w_in.dtype)
    kr = w_in[:, o2:o2 + C_ROPE]
    w_c = jnp.concatenate([w_in[:, :o2], kr, zpad, kr[:, swap], zpad], axis=1)
    w_d = w_in[:, o2 + C_ROPE:]
    q3 = w_q_up.reshape(C_LORA, C_HEADS, C_NOPE + C_ROPE)
    zq = jnp.zeros((C_LORA, C_HEADS, C_QK_PAD - C_NOPE - C_ROPE), w_q_up.dtype)
    wqa = jnp.concatenate([q3, zq], axis=2).reshape(C_LORA, C_HEADS * C_QK_PAD)
    wqb = jnp.concatenate([q3[:, :, C_NOPE:][:, :, swap], zq], axis=2).reshape(
        C_LORA, C_HEADS * HEAD_DIM)
    kv3 = w_kv_up.reshape(C_LORA, C_HEADS, 2 * HEAD_DIM)
    wk = kv3[:, :, :C_NOPE].reshape(C_LORA, C_HEADS * HEAD_DIM)
    wv = kv3[:, :, C_NOPE:].reshape(C_LORA, C_HEADS * HEAD_DIM)
    return w_c, w_d, wqa, wqb, wk, wv


def _odd_mixer(cfg, x, g, mods, weights, gq, gkv, rpb, w_out, rope, tm):
    w_c, w_d, wqa, wqb, wk, wv = weights
    ones_c = jnp.ones((1, w_c.shape[1]), F32)
    proj_c = _norm_matmul(cfg, x, g, mods, 1, w_c, ones_c, F32, tm, w_c.shape[1], "odd_in_latent")
    cs = np.ones((1, w_d.shape[1]), np.float32)
    cs[0, :D_HEADS * HEAD_DIM] = HEAD_DIM ** -0.5
    qkv_d = _norm_matmul(cfg, x, g, mods, 1, w_d, jnp.asarray(cs), BF16, tm, w_d.shape[1],
                         "odd_in_na")
    q, k, v = _mla_prep(cfg, proj_c, gq, gkv, wqa, wqb, wk, wv, *rope)
    o_c = _mla_attention(cfg, q, k, v)
    o_d = _na_attention(cfg, qkv_d, _na_bias_tables(rpb))
    return _resid_matmul(cfg, [o_c, o_d], w_out, x, mods, 1, 1.0, tm, 512, "odd_out")


def _trunk(cfg, x, c_pad, ada_w, ada_b, norm_g, ffn_w_in, ffn_w_out, rel_bias, ev_w_in, ev_w_out,
           diff_lambda, diff_subln_g, od_w_in, mla_q_norm_g, mla_kv_norm_g, mla_w_q_up,
           mla_w_kv_up, na_rpb, od_w_out, final_norm_g):
    depth = ada_w.shape[0]
    tm = 1024
    nseq = 1 + cfg.sb
    mod_all = _modulation(c_pad, ada_w, ada_b)[:, :nseq]
    chunk_seq = np.array([0] * cfg.pc + list(range(1, nseq)))
    mod_all = mod_all[:, chunk_seq].reshape(depth, (cfg.pc + cfg.sb) * 9, 1, D_MODEL)
    rope = _rope_tables(cfg.pc * cfg.ch)
    for i in range(depth):
        mods = mod_all[i]
        j = i // 2
        hid = _norm_swiglu(cfg, x, norm_g[i, 0], mods, 0, ffn_w_in[i, 0].astype(BF16), tm, 512)
        x = _resid_matmul(cfg, [hid], ffn_w_out[i, 0].astype(BF16), x, mods, 0, 0.5, tm, 512,
                          "ffn_out")
        if i % 2 == 0:
            x = _even_mixer(cfg, x, norm_g[i, 1], mods, ev_w_in[j].astype(BF16),
                            ev_w_out[j].astype(BF16), diff_lambda[j], diff_subln_g[j], rel_bias,
                            0.8 - 0.6 * math.exp(-0.3 * i), tm)
        else:
            weights = tuple(a.astype(BF16) for a in
                            _odd_weights(od_w_in[j], mla_w_q_up[j], mla_w_kv_up[j]))
            x = _odd_mixer(cfg, x, norm_g[i, 1], mods, weights, mla_q_norm_g[j], mla_kv_norm_g[j],
                           na_rpb[j], od_w_out[j].astype(BF16), rope, tm)
        hid = _norm_swiglu(cfg, x, norm_g[i, 2], mods, 2, ffn_w_in[i, 1].astype(BF16), tm, 512)
        x = _resid_matmul(cfg, [hid], ffn_w_out[i, 1].astype(BF16), x, mods, 2, 0.5, tm, 512,
                          "ffn_out")
    return _final_norm(x, final_norm_g)


def kernel(x_prompt, x_sample, c_prompt, c_sample, ada_w, ada_b, norm_g, ffn_w_in, ffn_w_out, rel_bias, ev_w_in, ev_w_out, diff_lambda, diff_subln_g, od_w_in, mla_q_norm_g, mla_kv_norm_g, mla_w_q_up, mla_w_kv_up, na_rpb, od_w_out, final_norm_g):
    pb, pt, _ = x_prompt.shape
    sb, st, _ = x_sample.shape
    assert pb == 1 and pt % st == 0
    cfg = Cfg(ch=st, pc=pt // st, sb=sb)
    x = jnp.concatenate([x_prompt.reshape(-1, D_MODEL), x_sample.reshape(-1, D_MODEL)], axis=0)
    c = jnp.concatenate([c_prompt, c_sample], axis=0)
    c_pad = jnp.pad(c, ((0, -c.shape[0] % 8), (0, 0)))
    y = _trunk(cfg, x, c_pad, ada_w, ada_b, norm_g, ffn_w_in, ffn_w_out, rel_bias, ev_w_in,
               ev_w_out, diff_lambda, diff_subln_g, od_w_in, mla_q_norm_g, mla_kv_norm_g,
               mla_w_q_up, mla_w_kv_up, na_rpb, od_w_out, final_norm_g)
    return (y[:pt].reshape(x_prompt.shape), y[pt:].reshape(x_sample.shape))
```

```python
import functools
import math
from typing import NamedTuple

import numpy as np
import jax
import jax.numpy as jnp
from jax import lax
from jax.experimental import pallas as pl
from jax.experimental.pallas import tpu as pltpu

F32 = jnp.float32
BF16 = jnp.bfloat16

D_MODEL = 2048
D_FF = 5632
HEAD_DIM = 128
A_HEADS = 8
A_CONFIGS = ((128, 1), (512, 4), (2048, 16))
A_HALF = 64
A_IN = 3 * 3 * A_HEADS * HEAD_DIM
B_HEADS = 8
B_QK_DIM = 64
B_W = B_HEADS * 2 * B_QK_DIM
EVEN_IN = A_IN + 3 * B_W
C_HEADS = 12
C_LORA = 512
C_NOPE = 128
C_ROPE = 64
C_QK_PAD = 256
ROPE_THETA = 10000.0
D_HEADS = 4
GRID_W = 64
NA_ROWS = 8
NA_COLS = 16
NA_BLOCK = NA_ROWS * GRID_W
REL_BUCKETS = 32
REL_MAX_DIST = 1024
EPS = 1e-6
NEG = -1e30
LOG2E = math.log2(math.e)
ONES_ROWS = 16

V7X_VMEM_BYTES = 64 * 1024 * 1024
VMEM_LIMIT = V7X_VMEM_BYTES - 8 * 1024 * 1024


class Cfg(NamedTuple):
    ch: int
    pc: int
    sb: int

    @property
    def n(self):
        return self.ch * (self.pc + self.sb)

    def seqs(self):
        out = [(0, self.pc * self.ch)]
        out += [((self.pc + b) * self.ch, self.ch) for b in range(self.sb)]
        return out


def _params(sem):
    return pltpu.CompilerParams(dimension_semantics=sem, vmem_limit_bytes=VMEM_LIMIT)


def _mod_body(c_ref, w_ref, b_ref, o_ref):
    c = c_ref[...]
    act = (c * jax.nn.sigmoid(c)).astype(BF16)
    o_ref[0] = jnp.dot(act, w_ref[0].astype(BF16), preferred_element_type=F32) + b_ref[0]


def _modulation(c_pad, ada_w, ada_b):
    depth, _, nout = ada_w.shape
    r = c_pad.shape[0]
    tn = 1024
    return pl.pallas_call(
        _mod_body,
        grid=(depth, nout // tn),
        in_specs=[pl.BlockSpec((r, D_MODEL), lambda l, j: (0, 0)),
                  pl.BlockSpec((1, D_MODEL, tn), lambda l, j: (l, 0, j)),
                  pl.BlockSpec((1, 1, tn), lambda l, j: (l, 0, j))],
        out_specs=pl.BlockSpec((1, r, tn), lambda l, j: (l, 0, j)),
        out_shape=jax.ShapeDtypeStruct((depth, r, nout), F32),
        compiler_params=_params(("arbitrary", "arbitrary")),
        name="modulation",
    )(c_pad, ada_w, ada_b.reshape(depth, 1, nout))


NORM_ROWS = 256


def _norm_rows(x_ref, g_ref, sh_ref, sc_ref, h_scr, tm):
    def body(r, carry):
        rows = pl.ds(pl.multiple_of(r * NORM_ROWS, NORM_ROWS), NORM_ROWS)
        x = x_ref[rows, :]
        inv = lax.rsqrt(jnp.mean(x * x, axis=-1, keepdims=True) + EPS)
        y = x * inv * g_ref[...]
        h_scr[rows, :] = (y * (1.0 + sc_ref[0]) + sh_ref[0]).astype(BF16)
        return carry
    lax.fori_loop(0, tm // NORM_ROWS, body, 0)


def _norm_mm_body(x_ref, g_ref, sh_ref, sc_ref, w_ref, cs_ref, o_ref, h_scr, *, tm):
    @pl.when(pl.program_id(1) == 0)
    def _():
        _norm_rows(x_ref, g_ref, sh_ref, sc_ref, h_scr, tm)
    acc = jnp.dot(h_scr[...], w_ref[...], preferred_element_type=F32)
    o_ref[...] = (acc * cs_ref[...]).astype(o_ref.dtype)


def _norm_mm_t_body(x_ref, g_ref, sh_ref, sc_ref, wt_ref, o_ref, h_scr, *, tm):
    _norm_rows(x_ref, g_ref, sh_ref, sc_ref, h_scr, tm)
    acc = lax.dot_general(wt_ref[...], h_scr[...], (((1,), (1,)), ((), ())),
                          preferred_element_type=F32)
    o_ref[...] = acc.astype(o_ref.dtype)


def _norm_swiglu_body(x_ref, g_ref, sh_ref, sc_ref, wg_ref, wu_ref, o_ref, h_scr, *, tm):
    @pl.when(pl.program_id(1) == 0)
    def _():
        _norm_rows(x_ref, g_ref, sh_ref, sc_ref, h_scr, tm)
    h = h_scr[...]
    gate = jnp.dot(h, wg_ref[...], preferred_element_type=F32)
    up = jnp.dot(h, wu_ref[...], preferred_element_type=F32)
    o_ref[...] = (gate * jax.nn.sigmoid(gate) * up).astype(o_ref.dtype)


def _mod_specs(cfg, tm, sub):
    per = cfg.ch // tm
    shift = pl.BlockSpec((1, 1, D_MODEL), lambda i, *_: ((i // per) * 9 + sub * 3, 0, 0))
    scale = pl.BlockSpec((1, 1, D_MODEL), lambda i, *_: ((i // per) * 9 + sub * 3 + 1, 0, 0))
    return shift, scale


def _norm_matmul(cfg, x, g, mods, sub, w, col_scale, out_dtype, tm, tn, name):
    n = x.shape[0]
    nout = w.shape[1]
    shift, scale = _mod_specs(cfg, tm, sub)
    return pl.pallas_call(
        functools.partial(_norm_mm_body, tm=tm),
        grid=(n // tm, nout // tn),
        in_specs=[pl.BlockSpec((tm, D_MODEL), lambda i, j: (i, 0)),
                  pl.BlockSpec((1, D_MODEL), lambda i, j: (0, 0)),
                  shift, scale,
                  pl.BlockSpec((D_MODEL, tn), lambda i, j: (0, j)),
                  pl.BlockSpec((1, tn), lambda i, j: (0, j))],
        out_specs=pl.BlockSpec((tm, tn), lambda i, j: (i, j)),
        out_shape=jax.ShapeDtypeStruct((n, nout), out_dtype),
        scratch_shapes=[pltpu.VMEM((tm, D_MODEL), BF16)],
        compiler_params=_params(("arbitrary", "arbitrary")),
        name=name,
    )(x, g.reshape(1, D_MODEL), mods, mods, w, col_scale)


def _norm_matmul_t(cfg, x, g, mods, sub, wt, tm, name):
    n = x.shape[0]
    nout = wt.shape[0]
    shift, scale = _mod_specs(cfg, tm, sub)
    return pl.pallas_call(
        functools.partial(_norm_mm_t_body, tm=tm),
        grid=(n // tm,),
        in_specs=[pl.BlockSpec((tm, D_MODEL), lambda i: (i, 0)),
                  pl.BlockSpec((1, D_MODEL), lambda i: (0, 0)),
                  shift, scale,
                  pl.BlockSpec((nout, D_MODEL), lambda i: (0, 0))],
        out_specs=pl.BlockSpec((nout, tm), lambda i: (0, i)),
        out_shape=jax.ShapeDtypeStruct((nout, n), BF16),
        scratch_shapes=[pltpu.VMEM((tm, D_MODEL), BF16)],
        compiler_params=_params(("arbitrary",)),
        name=name,
    )(x, g.reshape(1, D_MODEL), mods, mods, wt)


def _norm_swiglu(cfg, x, g, mods, sub, w_in, tm, tn):
    n = x.shape[0]
    nj = D_FF // tn
    shift, scale = _mod_specs(cfg, tm, sub)
    return pl.pallas_call(
        functools.partial(_norm_swiglu_body, tm=tm),
        grid=(n // tm, nj),
        in_specs=[pl.BlockSpec((tm, D_MODEL), lambda i, j: (i, 0)),
                  pl.BlockSpec((1, D_MODEL), lambda i, j: (0, 0)),
                  shift, scale,
                  pl.BlockSpec((D_MODEL, tn), lambda i, j: (0, j)),
                  pl.BlockSpec((D_MODEL, tn), lambda i, j: (0, j + nj))],
        out_specs=pl.BlockSpec((tm, tn), lambda i, j: (i, j)),
        out_shape=jax.ShapeDtypeStruct((n, D_FF), BF16),
        scratch_shapes=[pltpu.VMEM((tm, D_MODEL), BF16)],
        compiler_params=_params(("arbitrary", "arbitrary")),
        name="ffn_in",
    )(x, g.reshape(1, D_MODEL), mods, mods, w_in, w_in)


def _resid_mm_body(*refs, pieces, coef):
    lhs = refs[:pieces]
    ws = refs[pieces:2 * pieces]
    x_ref, gate_ref, o_ref = refs[2 * pieces:]
    acc = jnp.dot(lhs[0][...], ws[0][...], preferred_element_type=F32)
    for p in range(1, pieces):
        acc = acc + jnp.dot(lhs[p][...], ws[p][...], preferred_element_type=F32)
    o_ref[...] = x_ref[...] + (coef * gate_ref[0]) * acc


def _resid_matmul(cfg, lhs_list, w, x, mods, sub, coef, tm, tn, name):
    n = x.shape[0]
    per = cfg.ch // tm
    widths = [a.shape[1] for a in lhs_list]
    offs = np.cumsum([0] + widths[:-1])
    in_specs = [pl.BlockSpec((tm, k), lambda i, j: (i, 0)) for k in widths]
    for k, off in zip(widths, offs):
        assert off % k == 0
        in_specs.append(pl.BlockSpec((k, tn), lambda i, j, b=int(off // k): (b, j)))
    in_specs += [pl.BlockSpec((tm, tn), lambda i, j: (i, j)),
                 pl.BlockSpec((1, 1, tn), lambda i, j: ((i // per) * 9 + sub * 3 + 2, 0, j))]
    return pl.pallas_call(
        functools.partial(_resid_mm_body, pieces=len(lhs_list), coef=coef),
        grid=(n // tm, D_MODEL // tn),
        in_specs=in_specs,
        out_specs=pl.BlockSpec((tm, tn), lambda i, j: (i, j)),
        out_shape=jax.ShapeDtypeStruct((n, D_MODEL), F32),
        compiler_params=_params(("arbitrary", "arbitrary")),
        name=name,
    )(*lhs_list, *([w] * len(lhs_list)), x, mods)


def _final_norm_body(x_ref, g_ref, o_ref):
    x = x_ref[...]
    inv = lax.rsqrt(jnp.mean(x * x, axis=-1, keepdims=True) + EPS)
    o_ref[...] = x * inv * g_ref[...]


def _final_norm(x, g, tm=512):
    n = x.shape[0]
    return pl.pallas_call(
        _final_norm_body,
        grid=(n // tm,),
        in_specs=[pl.BlockSpec((tm, D_MODEL), lambda i: (i, 0)),
                  pl.BlockSpec((1, D_MODEL), lambda i: (0, 0))],
        out_specs=pl.BlockSpec((tm, D_MODEL), lambda i: (i, 0)),
        out_shape=jax.ShapeDtypeStruct((n, D_MODEL), F32),
        compiler_params=_params(("arbitrary",)),
        name="final_norm",
    )(x, g.reshape(1, D_MODEL))


def _t5_bucket_np(rel):
    nb = REL_BUCKETS // 2
    max_exact = nb // 2
    rel = np.asarray(rel, np.int64)
    base = np.where(rel > 0, nb, 0)
    n = np.abs(rel)
    nf = np.maximum(n, 1).astype(np.float64)
    large = max_exact + (np.log(nf / max_exact) / math.log(REL_MAX_DIST / max_exact)
                         * (nb - max_exact)).astype(np.int64)
    large = np.minimum(large, nb - 1)
    return (base + np.where(n < max_exact, n, large)).astype(np.int32)


EXPAND_ROWS = 32


def _expand_body(tab_ref, idx_ref, o_ref, *, nb):
    t = pl.program_id(1)

    def chunk(r, carry):
        rows = pl.ds(pl.multiple_of(r * EXPAND_ROWS, EXPAND_ROWS), EXPAND_ROWS)
        idx = idx_ref[0, rows, :]
        pick = lambda b, acc: jnp.where(idx == b, tab_ref[t, b], acc)
        o_ref[0, 0, rows, :] = lax.fori_loop(0, nb, pick, jnp.zeros(idx.shape, F32))
        return carry
    lax.fori_loop(0, idx_ref.shape[1] // EXPAND_ROWS, chunk, 0)


def _expand(tab, idx):
    ntab, nb = tab.shape
    ni, r, c = idx.shape
    assert r % EXPAND_ROWS == 0 and idx.min() >= 0 and idx.max() < nb
    return pl.pallas_call(
        functools.partial(_expand_body, nb=nb),
        grid=(ni, ntab),
        in_specs=[pl.BlockSpec(memory_space=pltpu.SMEM),
                  pl.BlockSpec((1, r, c), lambda i, t: (i, 0, 0))],
        out_specs=pl.BlockSpec((1, 1, r, c), lambda i, t: (i, t, 0, 0)),
        out_shape=jax.ShapeDtypeStruct((ni, ntab, r, c), F32),
        compiler_params=_params(("arbitrary", "arbitrary")),
        name="expand_table",
    )(tab, jnp.asarray(idx.astype(np.int32)))


DIL_BQ = 128


def _dil_tables(cfg, dil):
    lc = cfg.ch // dil
    nbp = lc // DIL_BQ
    assert lc % DIL_BQ == 0
    per_chunk = cfg.ch // DIL_BQ
    nblk = cfg.n // DIL_BQ
    prev = np.zeros(nblk, np.int32)
    nxt = np.zeros(nblk, np.int32)
    lo = np.zeros(nblk, np.int32)
    hi = np.zeros(nblk, np.int32)
    for i in range(nblk):
        c, blk = divmod(i, per_chunk)
        r, nn = divmod(blk, nbp)
        first_chunk, last_chunk = (0, cfg.pc - 1) if c < cfg.pc else (c, c)
        if nn > 0:
            prev[i], lo[i] = 2 * i - 1, 0
        elif c > first_chunk:
            prev[i], lo[i] = ((c - 1) * cfg.ch + r * lc + lc - A_HALF) // A_HALF, 0
        else:
            prev[i], lo[i] = 2 * i, A_HALF
        if nn < nbp - 1:
            nxt[i], hi[i] = 2 * i + 2, DIL_BQ + 2 * A_HALF
        elif c < last_chunk:
            nxt[i], hi[i] = ((c + 1) * cfg.ch + r * lc) // A_HALF, DIL_BQ + 2 * A_HALF
        else:
            nxt[i], hi[i] = 2 * i + 1, DIL_BQ + A_HALF
    return prev, nxt, lo, hi


def _dil_body(prev_ref, nxt_ref, lo_ref, hi_ref, q_ref, kp_ref, km_ref, kn_ref,
              vp_ref, vm_ref, vn_ref, bias_ref, o_ref, lse_ref):
    i = pl.program_id(0)
    nk = DIL_BQ + 2 * A_HALF
    row = lax.broadcasted_iota(jnp.int32, (DIL_BQ, nk), 0)
    col = lax.broadcasted_iota(jnp.int32, (DIL_BQ, nk), 1)
    off = col - row
    valid = (off >= 0) & (off <= 2 * A_HALF) & (col >= lo_ref[i]) & (col < hi_ref[i])
    k = jnp.concatenate([kp_ref[...], km_ref[...], kn_ref[...]], axis=0)
    v = jnp.concatenate([vp_ref[...], vm_ref[...], vn_ref[...]], axis=0)
    for h in range(A_HEADS):
        cols = slice(h * HEAD_DIM, (h + 1) * HEAD_DIM)
        s = lax.dot_general(q_ref[:, cols], k[:, cols], (((1,), (1,)), ((), ())),
                            preferred_element_type=F32)
        s = jnp.where(valid, s + bias_ref[h], NEG)
        m = jnp.max(s, axis=-1, keepdims=True)
        e = jnp.exp(s - m)
        den = jnp.sum(e, axis=-1, keepdims=True)
        o = jnp.dot(e.astype(BF16), v[:, cols], preferred_element_type=F32)
        o_ref[:, cols] = o / den
        lse_ref[:, cols] = jnp.broadcast_to(m + jnp.log(den), (DIL_BQ, HEAD_DIM))


def _dilated_branch(cfg, dil, q_arr, k_arr, v_arr, qcol, kcol, vcol, bias):
    n = cfg.n
    w = A_HEADS * HEAD_DIM
    prev, nxt, lo, hi = _dil_tables(cfg, dil)
    main = lambda cb: pl.BlockSpec((DIL_BQ, w), lambda i, p, x, l, h: (i, cb))
    before = lambda cb: pl.BlockSpec((A_HALF, w), lambda i, p, x, l, h: (p[i], cb))
    after = lambda cb: pl.BlockSpec((A_HALF, w), lambda i, p, x, l, h: (x[i], cb))
    out_spec = pl.BlockSpec((DIL_BQ, w), lambda i, p, x, l, h: (i, 0))
    grid_spec = pltpu.PrefetchScalarGridSpec(
        num_scalar_prefetch=4,
        grid=(n // DIL_BQ,),
        in_specs=[main(qcol), before(kcol), main(kcol), after(kcol),
                  before(vcol), main(vcol), after(vcol),
                  pl.BlockSpec((A_HEADS, DIL_BQ, DIL_BQ + 2 * A_HALF),
                               lambda i, p, x, l, h: (0, 0, 0))],
        out_specs=[out_spec, out_spec],
    )
    return pl.pallas_call(
        _dil_body,
        grid_spec=grid_spec,
        out_shape=[jax.ShapeDtypeStruct((n, w), F32), jax.ShapeDtypeStruct((n, w), F32)],
        compiler_params=_params(("arbitrary",)),
        name=f"dilated_d{dil}",
    )(jnp.asarray(prev), jnp.asarray(nxt), jnp.asarray(lo), jnp.asarray(hi),
      q_arr, k_arr, k_arr, k_arr, v_arr, v_arr, v_arr, bias)


def _dil_bias(rel_bias, g, dil):
    off = np.arange(DIL_BQ + 2 * A_HALF)[None, :] - np.arange(DIL_BQ)[:, None] - A_HALF
    bucket = _t5_bucket_np(dil * np.clip(off, -A_HALF, A_HALF))
    return _expand(rel_bias[:, g * A_HEADS:(g + 1) * A_HEADS].T, bucket[None])[0]


def _merge_body(o1, l1, o2, l2, o3, l3, out_ref):
    a, b, c = l1[...], l2[...], l3[...]
    m = jnp.maximum(jnp.maximum(a, b), c)
    ea, eb, ec = jnp.exp(a - m), jnp.exp(b - m), jnp.exp(c - m)
    tot = ea + eb + ec
    out = (ea / tot) * o1[...] + (eb / tot) * o2[...] + (ec / tot) * o3[...]
    out_ref[...] = out.astype(out_ref.dtype)


def _merge_branches(outs, lses, tm=512):
    n, w = outs[0].shape
    spec = pl.BlockSpec((tm, w), lambda i: (i, 0))
    args = [a for pair in zip(outs, lses) for a in pair]
    return pl.pallas_call(
        _merge_body,
        grid=(n // tm,),
        in_specs=[spec] * 6,
        out_specs=spec,
        out_shape=jax.ShapeDtypeStruct((n, w), BF16),
        compiler_params=_params(("arbitrary",)),
        name="dilated_merge",
    )(*args)


def _to_residue_major(cfg, a, dil):
    n, w = a.shape
    return a.reshape(n // cfg.ch, cfg.ch // dil, dil, w).transpose(0, 2, 1, 3).reshape(n, w)


def _from_residue_major(cfg, a, dil):
    n, w = a.shape
    return a.reshape(n // cfg.ch, dil, cfg.ch // dil, w).transpose(0, 2, 1, 3).reshape(n, w)


def _flash_worklist(cfg, tq, tk, rel_lo=None, rel_hi=None):
    qb, kb, tile, flags = [], [], [], []
    for start, length in cfg.seqs():
        assert length % tq == 0 and length % tk == 0 and start % tq == 0 and start % tk == 0
        nk = length // tk
        for qi in range(length // tq):
            for kj in range(nk):
                qb.append(start // tq + qi)
                kb.append(start // tk + kj)
                if rel_lo is not None:
                    assert (kj * tk) % tq == 0
                    d = (kj * tk - qi * tq) // tq
                    tile.append(min(max(d, rel_lo), rel_hi) - rel_lo)
                else:
                    tile.append(0)
                flags.append((1 if kj == 0 else 0) | (2 if kj == nk - 1 else 0))
    as_i32 = lambda a: jnp.asarray(np.asarray(a, np.int32))
    return as_i32(qb), as_i32(kb), as_i32(tile), as_i32(flags)


def _flash_init(flags, m_scr, acc_scr):
    @pl.when((flags & 1) != 0)
    def _():
        m_scr[...] = jnp.full(m_scr.shape, -jnp.inf, F32)
        acc_scr[...] = jnp.zeros(acc_scr.shape, F32)


def _flash_update(h, s, vt, m_scr, acc_scr):
    m_prev = m_scr[h]
    m_new = jnp.maximum(m_prev, jnp.max(s, axis=0, keepdims=True))
    alpha = jnp.exp2(m_prev - m_new)
    p = jnp.exp2(s - m_new).astype(BF16)
    lhs = jnp.concatenate([vt, jnp.ones((ONES_ROWS, vt.shape[1]), BF16)], axis=0)
    acc_scr[h] = alpha * acc_scr[h] + jnp.dot(lhs, p, preferred_element_type=F32)
    m_scr[h] = m_new


DIFF_TQ = 256
DIFF_TK = 512


def _diff_tile_range(max_len):
    rel = np.arange(-max_len + 1, max_len)
    b = _t5_bucket_np(rel)
    sat_pos = int(rel[b != b[-1]].max()) + 1
    sat_neg = int(rel[b != b[0]].min()) - 1
    hi = -(-(sat_pos + DIFF_TQ - 1) // DIFF_TQ)
    lo = (sat_neg - (DIFF_TK - 1)) // DIFF_TQ
    return lo, hi


def _diff_bias_tiles(rel_bias, lo, hi):
    d = np.arange(lo, hi + 1)[:, None, None] * DIFF_TQ
    rel = d + np.arange(DIFF_TK)[None, :, None] - np.arange(DIFF_TQ)[None, None, :]
    return _expand(rel_bias[:, 3 * A_HEADS:].T * LOG2E, _t5_bucket_np(rel))


def _diff_body(qb_ref, kb_ref, tile_ref, fl_ref, q_ref, k_ref, vt_ref, bias_ref, lam_ref, g_ref,
               o_ref, m_scr, acc_scr, *, lam_init):
    flags = fl_ref[pl.program_id(0)]
    tq = DIFF_TQ
    _flash_init(flags, m_scr, acc_scr)

    lane = lax.broadcasted_iota(jnp.int32, (tq, HEAD_DIM), 1)
    for h in range(B_HEADS):
        cols = slice(h * HEAD_DIM, (h + 1) * HEAD_DIM)
        qh = q_ref[:, cols]
        zero = jnp.zeros_like(qh)
        q2 = jnp.concatenate([jnp.where(lane < B_QK_DIM, qh, zero),
                              jnp.where(lane >= B_QK_DIM, qh, zero)], axis=0)
        s = lax.dot_general(k_ref[:, cols], q2, (((1,), (1,)), ((), ())),
                            preferred_element_type=F32)
        b = bias_ref[0, h]
        s = s + jnp.concatenate([b, b], axis=1)
        _flash_update(h, s, vt_ref[cols, :], m_scr, acc_scr)

    @pl.when((flags & 2) != 0)
    def _():
        lf = lam_ref[...]
        lam = (jnp.exp(jnp.sum(lf[0:1] * lf[1:2], axis=-1, keepdims=True))
               - jnp.exp(jnp.sum(lf[2:3] * lf[3:4], axis=-1, keepdims=True)) + lam_init)
        for h in range(B_HEADS):
            cols = slice(h * HEAD_DIM, (h + 1) * HEAD_DIM)
            acc = acc_scr[h]
            att = acc[:HEAD_DIM] / acc[HEAD_DIM:HEAD_DIM + 1]
            o = att[:, :tq] - lam * att[:, tq:]
            inv = lax.rsqrt(jnp.mean(o * o, axis=0, keepdims=True) + EPS)
            o_ref[:, cols] = (((o * inv).T * g_ref[...]) * (1.0 - lam_init)).astype(o_ref.dtype)


def _diff_attention(cfg, proj, qcol, kcol, vt, bias_tiles, tile_lo, tile_hi, lam, subln_g,
                    lam_init):
    n = cfg.n
    tq, tk = DIFF_TQ, DIFF_TK
    qb, kb, tile, flags = _flash_worklist(cfg, tq, tk, tile_lo, tile_hi)
    grid_spec = pltpu.PrefetchScalarGridSpec(
        num_scalar_prefetch=4,
        grid=(qb.shape[0],),
        in_specs=[pl.BlockSpec((tq, B_W), lambda s, q, k, t, f: (q[s], qcol)),
                  pl.BlockSpec((tk, B_W), lambda s, q, k, t, f: (k[s], kcol)),
                  pl.BlockSpec((B_W, tk), lambda s, q, k, t, f: (0, k[s])),
                  pl.BlockSpec((1, B_HEADS, tk, tq), lambda s, q, k, t, f: (t[s], 0, 0, 0)),
                  pl.BlockSpec((4, B_QK_DIM), lambda s, q, k, t, f: (0, 0)),
                  pl.BlockSpec((1, HEAD_DIM), lambda s, q, k, t, f: (0, 0))],
        out_specs=pl.BlockSpec((tq, B_W), lambda s, q, k, t, f: (q[s], 0)),
        scratch_shapes=[pltpu.VMEM((B_HEADS, 1, 2 * tq), F32),
                        pltpu.VMEM((B_HEADS, HEAD_DIM + ONES_ROWS, 2 * tq), F32)],
    )
    return pl.pallas_call(
        functools.partial(_diff_body, lam_init=lam_init),
        grid_spec=grid_spec,
        out_shape=jax.ShapeDtypeStruct((n, B_W), BF16),
        compiler_params=_params(("arbitrary",)),
        name="diff_attention",
    )(qb, kb, tile, flags, proj, proj, vt, bias_tiles, lam, subln_g.reshape(1, HEAD_DIM))


MLA_TM = 512
MLA_TQ = 512
MLA_TK = 512


def _rope_tables(max_len):
    inv = ROPE_THETA ** (-jnp.arange(0, C_ROPE, 2, dtype=F32) / C_ROPE)
    ang = jnp.arange(max_len, dtype=F32)[:, None] * inv[None, :]
    cos, sin = jnp.cos(ang), jnp.sin(ang)
    zero = jnp.zeros((max_len, HEAD_DIM - C_ROPE), F32)
    return (jnp.concatenate([cos, cos, zero], axis=1),
            jnp.concatenate([-sin, sin, zero], axis=1))


def _mla_prep_body(p_ref, gq_ref, gkv_ref, wqa_ref, wqb_ref, wk_ref, wvt_ref, cos_ref, sin_ref,
                   q_ref, k_ref, vt_ref, *, scale):
    def normed(x, g):
        inv = lax.rsqrt(jnp.mean(x * x, axis=-1, keepdims=True) + EPS)
        return (x * inv * g).astype(BF16)

    cq = normed(p_ref[:, 0:C_LORA], gq_ref[...])
    ckv = normed(p_ref[:, C_LORA:2 * C_LORA], gkv_ref[...])
    cos = cos_ref[...]
    sin = sin_ref[...]
    qa = jnp.dot(cq, wqa_ref[...], preferred_element_type=F32)
    qb = jnp.dot(cq, wqb_ref[...], preferred_element_type=F32)
    kn = jnp.dot(ckv, wk_ref[...], preferred_element_type=F32)
    vt_ref[...] = lax.dot_general(wvt_ref[...], ckv, (((1,), (1,)), ((), ())),
                                  preferred_element_type=F32).astype(BF16)
    kr = p_ref[:, 2 * C_LORA:2 * C_LORA + HEAD_DIM]
    kr_sw = p_ref[:, 2 * C_LORA + HEAD_DIM:2 * C_LORA + 2 * HEAD_DIM]
    k_rope = (kr * cos + kr_sw * sin).astype(BF16)
    for h in range(C_HEADS):
        a0 = h * C_QK_PAD
        a1 = a0 + HEAD_DIM
        a2 = a0 + C_QK_PAD
        b = slice(h * HEAD_DIM, (h + 1) * HEAD_DIM)
        q_ref[:, a0:a1] = (qa[:, a0:a1] * scale).astype(BF16)
        q_ref[:, a1:a2] = ((qa[:, a1:a2] * cos + qb[:, b] * sin) * scale).astype(BF16)
        k_ref[:, a0:a1] = kn[:, b].astype(BF16)
        k_ref[:, a1:a2] = k_rope


def _mla_prep(cfg, proj_c, gq, gkv, wqa, wqb, wk, wvt, cos_tab, sin_tab):
    n = cfg.n
    tm = MLA_TM
    pblocks = cfg.pc * cfg.ch // tm
    per = cfg.ch // tm
    pos = lambda i: (jnp.where(i < pblocks, i, (i - pblocks) % per), 0)
    full = lambda shape: pl.BlockSpec(shape, lambda i: (0, 0))
    qk_w = C_HEADS * C_QK_PAD
    v_w = C_HEADS * HEAD_DIM
    return pl.pallas_call(
        functools.partial(_mla_prep_body, scale=(C_NOPE + C_ROPE) ** -0.5 * LOG2E),
        grid=(n // tm,),
        in_specs=[pl.BlockSpec((tm, proj_c.shape[1]), lambda i: (i, 0)),
                  full((1, C_LORA)), full((1, C_LORA)),
                  full(wqa.shape), full(wqb.shape), full(wk.shape), full(wvt.shape),
                  pl.BlockSpec((tm, HEAD_DIM), pos), pl.BlockSpec((tm, HEAD_DIM), pos)],
        out_specs=[pl.BlockSpec((tm, qk_w), lambda i: (i, 0)),
                   pl.BlockSpec((tm, qk_w), lambda i: (i, 0)),
                   pl.BlockSpec((v_w, tm), lambda i: (0, i))],
        out_shape=[jax.ShapeDtypeStruct((n, qk_w), BF16),
                   jax.ShapeDtypeStruct((n, qk_w), BF16),
                   jax.ShapeDtypeStruct((v_w, n), BF16)],
        compiler_params=_params(("arbitrary",)),
        name="mla_prep",
    )(proj_c, gq.reshape(1, C_LORA), gkv.reshape(1, C_LORA), wqa, wqb, wk, wvt, cos_tab, sin_tab)


def _mla_body(qb_ref, kb_ref, tile_ref, fl_ref, q_ref, k_ref, vt_ref, o_ref, m_scr, acc_scr):
    flags = fl_ref[pl.program_id(0)]
    _flash_init(flags, m_scr, acc_scr)

    for h in range(C_HEADS):
        qk = slice(h * C_QK_PAD, (h + 1) * C_QK_PAD)
        vc = slice(h * HEAD_DIM, (h + 1) * HEAD_DIM)
        s = lax.dot_general(k_ref[:, qk], q_ref[:, qk], (((1,), (1,)), ((), ())),
                            preferred_element_type=F32)
        _flash_update(h, s, vt_ref[vc, :], m_scr, acc_scr)

    @pl.when((flags & 2) != 0)
    def _():
        for h in range(C_HEADS):
            vc = slice(h * HEAD_DIM, (h + 1) * HEAD_DIM)
            acc = acc_scr[h]
            o_ref[:, vc] = (acc[:HEAD_DIM] / acc[HEAD_DIM:HEAD_DIM + 1]).T.astype(o_ref.dtype)


def _mla_attention(cfg, q, k, vt):
    n = cfg.n
    tq, tk = MLA_TQ, MLA_TK
    qb, kb, tile, flags = _flash_worklist(cfg, tq, tk)
    qk_w = C_HEADS * C_QK_PAD
    v_w = C_HEADS * HEAD_DIM
    grid_spec = pltpu.PrefetchScalarGridSpec(
        num_scalar_prefetch=4,
        grid=(qb.shape[0],),
        in_specs=[pl.BlockSpec((tq, qk_w), lambda s, q_, k_, t, f: (q_[s], 0)),
                  pl.BlockSpec((tk, qk_w), lambda s, q_, k_, t, f: (k_[s], 0)),
                  pl.BlockSpec((v_w, tk), lambda s, q_, k_, t, f: (0, k_[s]))],
        out_specs=pl.BlockSpec((tq, v_w), lambda s, q_, k_, t, f: (q_[s], 0)),
        scratch_shapes=[pltpu.VMEM((C_HEADS, 1, tq), F32),
                        pltpu.VMEM((C_HEADS, HEAD_DIM + ONES_ROWS, tq), F32)],
    )
    return pl.pallas_call(
        _mla_body,
        grid_spec=grid_spec,
        out_shape=jax.ShapeDtypeStruct((n, v_w), BF16),
        compiler_params=_params(("arbitrary",)),
        name="mla_attention",
    )(qb, kb, tile, flags, q, k, vt)


def _na_tables(cfg):
    prev, nxt, var = [], [], []
    for start, length in cfg.seqs():
        nb = length // NA_BLOCK
        assert length % NA_BLOCK == 0 and start % NA_BLOCK == 0 and nb >= 3
        b0 = start // NA_BLOCK
        for r in range(nb):
            prev.append(b0 + max(r - 1, 0))
            nxt.append(b0 + min(r + 1, nb - 1))
            var.append(0 if r == 0 else (2 if r == nb - 1 else 1))
    as_i32 = lambda a: jnp.asarray(np.asarray(a, np.int32))
    return as_i32(prev), as_i32(nxt), as_i32(var)


def _na_window_start(a, variant):
    centred = a - NA_ROWS // 2
    return (max(centred, 0), centred, min(centred, 0))[variant]


def _na_bias_tables(rpb):
    ncol = 2 * NA_COLS - 1
    c = np.arange(GRID_W)[:, None, None]
    b = np.arange(NA_ROWS)[None, :, None]
    kc = np.arange(GRID_W)[None, None, :]
    cstart = np.clip(c - NA_COLS // 2, 0, GRID_W - NA_COLS)
    col_ok = (kc >= cstart) & (kc < cstart + NA_COLS)
    dc = np.clip(kc - c, -(NA_COLS - 1), NA_COLS - 1) + (NA_COLS - 1)
    masked = NA_ROWS * ncol
    idx = np.where(col_ok, b * ncol + dc, masked).reshape(1, GRID_W, NA_BLOCK)
    rows = np.arange(NA_ROWS)[:, None] + np.arange(NA_ROWS)[None, :]
    tab = rpb[:, rows, :].reshape(D_HEADS * NA_ROWS, NA_ROWS * ncol)
    tab = jnp.concatenate([tab, jnp.full((tab.shape[0], 1), NEG, F32)], axis=1)
    slabs = _expand(tab, idx)[0].reshape(D_HEADS, NA_ROWS, GRID_W, NA_BLOCK)
    variants = []
    for variant in range(3):
        row_blocks = []
        for a in range(NA_ROWS):
            start = _na_window_start(a, variant)
            left = (NA_ROWS + start) * GRID_W
            row_blocks.append(jnp.pad(slabs[:, start - a + NA_ROWS - 1],
                                      ((0, 0), (0, 0), (left, 2 * NA_BLOCK - left)),
                                      constant_values=NEG))
        variants.append(jnp.concatenate(row_blocks, axis=1))
    return jnp.stack(variants, axis=0)


def _na_body(prev_ref, nxt_ref, var_ref, q_ref, kp_ref, km_ref, kn_ref, vp_ref, vm_ref, vn_ref,
             tab_ref, o_ref):
    k = jnp.concatenate([kp_ref[...], km_ref[...], kn_ref[...]], axis=0)
    v = jnp.concatenate([vp_ref[...], vm_ref[...], vn_ref[...]], axis=0)
    for h in range(D_HEADS):
        cols = slice(h * HEAD_DIM, (h + 1) * HEAD_DIM)
        s = lax.dot_general(q_ref[:, cols], k[:, cols], (((1,), (1,)), ((), ())),
                            preferred_element_type=F32)
        s = s + tab_ref[0, h]
        m = jnp.max(s, axis=-1, keepdims=True)
        e = jnp.exp(s - m)
        den = jnp.sum(e, axis=-1, keepdims=True)
        o = jnp.dot(e.astype(BF16), v[:, cols], preferred_element_type=F32)
        o_ref[:, cols] = (o / den).astype(o_ref.dtype)


def _na_attention(cfg, qkv, tabs):
    n = cfg.n
    w = D_HEADS * HEAD_DIM
    prev, nxt, var = _na_tables(cfg)
    own = lambda cb: pl.BlockSpec((NA_BLOCK, w), lambda i, p, x, t: (i, cb))
    before = lambda cb: pl.BlockSpec((NA_BLOCK, w), lambda i, p, x, t: (p[i], cb))
    after = lambda cb: pl.BlockSpec((NA_BLOCK, w), lambda i, p, x, t: (x[i], cb))
    grid_spec = pltpu.PrefetchScalarGridSpec(
        num_scalar_prefetch=3,
        grid=(n // NA_BLOCK,),
        in_specs=[own(0), before(1), own(1), after(1), before(2), own(2), after(2),
                  pl.BlockSpec((1, D_HEADS, NA_BLOCK, 3 * NA_BLOCK),
                               lambda i, p, x, t: (t[i], 0, 0, 0))],
        out_specs=pl.BlockSpec((NA_BLOCK, w), lambda i, p, x, t: (i, 0)),
    )
    return pl.pallas_call(
        _na_body,
        grid_spec=grid_spec,
        out_shape=jax.ShapeDtypeStruct((n, w), BF16),
        compiler_params=_params(("arbitrary",)),
        name="na_attention",
    )(prev, nxt, var, qkv, qkv, qkv, qkv, qkv, qkv, qkv, tabs)


EVEN_PROJ = A_IN + 2 * B_W


def _even_col_scale():
    cs = np.ones((1, EVEN_PROJ), np.float32)
    cs[0, :3 * A_HEADS * HEAD_DIM] = HEAD_DIM ** -0.5
    cs[0, A_IN:A_IN + B_W] = B_QK_DIM ** -0.5 * LOG2E
    return jnp.asarray(cs)


def _even_tables(cfg, rel_bias):
    dil = [_dil_bias(rel_bias, gi, d) for gi, (_, d) in enumerate(A_CONFIGS)]
    tile_lo, tile_hi = _diff_tile_range(cfg.pc * cfg.ch)
    return dil, _diff_bias_tiles(rel_bias, tile_lo, tile_hi), tile_lo, tile_hi


def _even_mixer(cfg, x, g, mods, w_in, wvt, w_out, lam, subln_g, tables, lam_init, tm):
    w = A_HEADS * HEAD_DIM
    dil_bias, diff_tiles, tile_lo, tile_hi = tables
    proj = _norm_matmul(cfg, x, g, mods, 1, w_in, _even_col_scale(), BF16, tm, 1024, "even_in")
    vt = _norm_matmul_t(cfg, x, g, mods, 1, wvt, tm, "even_in_vt")
    outs, lses = [], []
    for gi, (_, dil) in enumerate(A_CONFIGS):
        if dil == 1:
            o, l = _dilated_branch(cfg, dil, proj, proj, proj, gi, 3 + gi, 6 + gi, dil_bias[gi])
        else:
            q, k, v = (_to_residue_major(cfg, proj[:, (s * 3 + gi) * w:(s * 3 + gi + 1) * w], dil)
                       for s in range(3))
            o, l = _dilated_branch(cfg, dil, q, k, v, 0, 0, 0, dil_bias[gi])
            o, l = _from_residue_major(cfg, o, dil), _from_residue_major(cfg, l, dil)
        outs.append(o)
        lses.append(l)
    o_a = _merge_branches(outs, lses)
    cb = A_IN // B_W
    o_b = _diff_attention(cfg, proj, cb, cb + 1, vt, diff_tiles, tile_lo, tile_hi, lam, subln_g,
                          lam_init)
    return _resid_matmul(cfg, [o_a, o_b], w_out, x, mods, 1, 1.0, tm, 512, "even_out")


def _odd_weights(w_in, w_q_up, w_kv_up):
    half = C_ROPE // 2
    swap = np.concatenate([np.arange(half, C_ROPE), np.arange(half)])
    o2 = 2 * C_LORA
    zpad = jnp.zeros((D_MODEL, HEAD_DIM - C_ROPE), w_in.dtype)
    kr = w_in[:, o2:o2 + C_ROPE]
    w_c = jnp.concatenate([w_in[:, :o2], kr, zpad, kr[:, swap], zpad], axis=1)
    w_d = w_in[:, o2 + C_ROPE:]
    q3 = w_q_up.reshape(C_LORA, C_HEADS, C_NOPE + C_ROPE)
    zq = jnp.zeros((C_LORA, C_HEADS, C_QK_PAD - C_NOPE - C_ROPE), w_q_up.dtype)
    wqa = jnp.concatenate([q3, zq], axis=2).reshape(C_LORA, C_HEADS * C_QK_PAD)
    wqb = jnp.concatenate([q3[:, :, C_NOPE:][:, :, swap], zq], axis=2).reshape(
        C_LORA, C_HEADS * HEAD_DIM)
    kv3 = w_kv_up.reshape(C_LORA, C_HEADS, 2 * HEAD_DIM)
    wk = kv3[:, :, :C_NOPE].reshape(C_LORA, C_HEADS * HEAD_DIM)
    wvt = kv3[:, :, C_NOPE:].reshape(C_LORA, C_HEADS * HEAD_DIM).T
    return w_c, w_d, wqa, wqb, wk, wvt


def _odd_mixer(cfg, x, g, mods, weights, gq, gkv, rpb, w_out, rope, tm):
    w_c, w_d, wqa, wqb, wk, wvt = weights
    ones_c = jnp.ones((1, w_c.shape[1]), F32)
    proj_c = _norm_matmul(cfg, x, g, mods, 1, w_c, ones_c, F32, tm, w_c.shape[1], "odd_in_latent")
    cs = np.ones((1, w_d.shape[1]), np.float32)
    cs[0, :D_HEADS * HEAD_DIM] = HEAD_DIM ** -0.5
    qkv_d = _norm_matmul(cfg, x, g, mods, 1, w_d, jnp.asarray(cs), BF16, tm, w_d.shape[1],
                         "odd_in_na")
    q, k, vt = _mla_prep(cfg, proj_c, gq, gkv, wqa, wqb, wk, wvt, *rope)
    o_c = _mla_attention(cfg, q, k, vt)
    o_d = _na_attention(cfg, qkv_d, _na_bias_tables(rpb))
    return _resid_matmul(cfg, [o_c, o_d], w_out, x, mods, 1, 1.0, tm, 512, "odd_out")


def _trunk(cfg, x, c_pad, ada_w, ada_b, norm_g, ffn_w_in, ffn_w_out, rel_bias, ev_w_in, ev_w_out,
           diff_lambda, diff_subln_g, od_w_in, mla_q_norm_g, mla_kv_norm_g, mla_w_q_up,
           mla_w_kv_up, na_rpb, od_w_out, final_norm_g):
    depth = ada_w.shape[0]
    tm = 1024
    nseq = 1 + cfg.sb
    mod_all = _modulation(c_pad, ada_w, ada_b)[:, :nseq]
    chunk_seq = np.array([0] * cfg.pc + list(range(1, nseq)))
    mod_all = mod_all[:, chunk_seq].reshape(depth, (cfg.pc + cfg.sb) * 9, 1, D_MODEL)
    rope = _rope_tables(cfg.pc * cfg.ch)
    even_tables = _even_tables(cfg, rel_bias)
    for i in range(depth):
        mods = mod_all[i]
        j = i // 2
        hid = _norm_swiglu(cfg, x, norm_g[i, 0], mods, 0, ffn_w_in[i, 0].astype(BF16), tm, 512)
        x = _resid_matmul(cfg, [hid], ffn_w_out[i, 0].astype(BF16), x, mods, 0, 0.5, tm, 512,
                          "ffn_out")
        if i % 2 == 0:
            w_in = ev_w_in[j].astype(BF16)
            x = _even_mixer(cfg, x, norm_g[i, 1], mods, w_in[:, :EVEN_PROJ], w_in[:, EVEN_PROJ:].T,
                            ev_w_out[j].astype(BF16), diff_lambda[j], diff_subln_g[j], even_tables,
                            0.8 - 0.6 * math.exp(-0.3 * i), tm)
        else:
            weights = tuple(a.astype(BF16) for a in
                            _odd_weights(od_w_in[j], mla_w_q_up[j], mla_w_kv_up[j]))
            x = _odd_mixer(cfg, x, norm_g[i, 1], mods, weights, mla_q_norm_g[j], mla_kv_norm_g[j],
                           na_rpb[j], od_w_out[j].astype(BF16), rope, tm)
        hid = _norm_swiglu(cfg, x, norm_g[i, 2], mods, 2, ffn_w_in[i, 1].astype(BF16), tm, 512)
        x = _resid_matmul(cfg, [hid], ffn_w_out[i, 1].astype(BF16), x, mods, 2, 0.5, tm, 512,
                          "ffn_out")
    return _final_norm(x, final_norm_g)


def kernel(x_prompt, x_sample, c_prompt, c_sample, ada_w, ada_b, norm_g, ffn_w_in, ffn_w_out, rel_bias, ev_w_in, ev_w_out, diff_lambda, diff_subln_g, od_w_in, mla_q_norm_g, mla_kv_norm_g, mla_w_q_up, mla_w_kv_up, na_rpb, od_w_out, final_norm_g):
    pb, pt, _ = x_prompt.shape
    sb, st, _ = x_sample.shape
    assert pb == 1 and pt % st == 0
    cfg = Cfg(ch=st, pc=pt // st, sb=sb)
    x = jnp.concatenate([x_prompt.reshape(-1, D_MODEL), x_sample.reshape(-1, D_MODEL)], axis=0)
    c = jnp.concatenate([c_prompt, c_sample], axis=0)
    c_pad = jnp.pad(c, ((0, -c.shape[0] % 8), (0, 0)))
    y = _trunk(cfg, x, c_pad, ada_w, ada_b, norm_g, ffn_w_in, ffn_w_out, rel_bias, ev_w_in,
               ev_w_out, diff_lambda, diff_subln_g, od_w_in, mla_q_norm_g, mla_kv_norm_g,
               mla_w_q_up, mla_w_kv_up, na_rpb, od_w_out, final_norm_g)
    return (y[:pt].reshape(x_prompt.shape), y[pt:].reshape(x_sample.shape))
```

```python
import functools
import math
from typing import NamedTuple

import numpy as np
import jax
import jax.numpy as jnp
from jax import lax
from jax.experimental import pallas as pl
from jax.experimental.pallas import tpu as pltpu

F32 = jnp.float32
BF16 = jnp.bfloat16

D_MODEL = 2048
D_FF = 5632
HEAD_DIM = 128
A_HEADS = 8
A_CONFIGS = ((128, 1), (512, 4), (2048, 16))
A_HALF = 64
A_IN = 3 * 3 * A_HEADS * HEAD_DIM
B_HEADS = 8
B_QK_DIM = 64
B_W = B_HEADS * 2 * B_QK_DIM
EVEN_IN = A_IN + 3 * B_W
C_HEADS = 12
C_LORA = 512
C_NOPE = 128
C_ROPE = 64
C_QK_PAD = 256
ROPE_THETA = 10000.0
D_HEADS = 4
GRID_W = 64
NA_ROWS = 8
NA_COLS = 16
NA_BLOCK = NA_ROWS * GRID_W
REL_BUCKETS = 32
REL_MAX_DIST = 1024
EPS = 1e-6
NEG = -1e30
LOG2E = math.log2(math.e)
ONES_ROWS = 16

V7X_VMEM_BYTES = 64 * 1024 * 1024
VMEM_LIMIT = V7X_VMEM_BYTES - 8 * 1024 * 1024


class Cfg(NamedTuple):
    ch: int
    pc: int
    sb: int

    @property
    def n(self):
        return self.ch * (self.pc + self.sb)

    def seqs(self):
        out = [(0, self.pc * self.ch)]
        out += [((self.pc + b) * self.ch, self.ch) for b in range(self.sb)]
        return out


def _params(sem):
    return pltpu.CompilerParams(dimension_semantics=sem, vmem_limit_bytes=VMEM_LIMIT)


def _mod_body(c_ref, w_ref, b_ref, o_ref):
    c = c_ref[...]
    act = (c * jax.nn.sigmoid(c)).astype(BF16)
    o_ref[0] = jnp.dot(act, w_ref[0].astype(BF16), preferred_element_type=F32) + b_ref[0]


def _modulation(c_pad, ada_w, ada_b):
    depth, _, nout = ada_w.shape
    r = c_pad.shape[0]
    tn = 1024
    return pl.pallas_call(
        _mod_body,
        grid=(depth, nout // tn),
        in_specs=[pl.BlockSpec((r, D_MODEL), lambda l, j: (0, 0)),
                  pl.BlockSpec((1, D_MODEL, tn), lambda l, j: (l, 0, j)),
                  pl.BlockSpec((1, 1, tn), lambda l, j: (l, 0, j))],
        out_specs=pl.BlockSpec((1, r, tn), lambda l, j: (l, 0, j)),
        out_shape=jax.ShapeDtypeStruct((depth, r, nout), F32),
        compiler_params=_params(("arbitrary", "arbitrary")),
        name="modulation",
    )(c_pad, ada_w, ada_b.reshape(depth, 1, nout))


NORM_ROWS = 256


def _norm_rows(x_ref, g_ref, sh_ref, sc_ref, h_scr, tm):
    def body(r, carry):
        rows = pl.ds(pl.multiple_of(r * NORM_ROWS, NORM_ROWS), NORM_ROWS)
        x = x_ref[rows, :]
        inv = lax.rsqrt(jnp.mean(x * x, axis=-1, keepdims=True) + EPS)
        y = x * inv * g_ref[...]
        h_scr[rows, :] = (y * (1.0 + sc_ref[0]) + sh_ref[0]).astype(BF16)
        return carry
    lax.fori_loop(0, tm // NORM_ROWS, body, 0)


def _norm_mm_body(x_ref, g_ref, sh_ref, sc_ref, w_ref, cs_ref, o_ref, h_scr, *, tm):
    @pl.when(pl.program_id(1) == 0)
    def _():
        _norm_rows(x_ref, g_ref, sh_ref, sc_ref, h_scr, tm)
    acc = jnp.dot(h_scr[...], w_ref[...], preferred_element_type=F32)
    o_ref[...] = (acc * cs_ref[...]).astype(o_ref.dtype)


def _norm_mm_t_body(x_ref, g_ref, sh_ref, sc_ref, wt_ref, o_ref, h_scr, *, tm):
    _norm_rows(x_ref, g_ref, sh_ref, sc_ref, h_scr, tm)
    acc = lax.dot_general(wt_ref[...], h_scr[...], (((1,), (1,)), ((), ())),
                          preferred_element_type=F32)
    o_ref[...] = acc.astype(o_ref.dtype)


def _norm_swiglu_body(x_ref, g_ref, sh_ref, sc_ref, wg_ref, wu_ref, o_ref, h_scr, *, tm):
    @pl.when(pl.program_id(1) == 0)
    def _():
        _norm_rows(x_ref, g_ref, sh_ref, sc_ref, h_scr, tm)
    h = h_scr[...]
    gate = jnp.dot(h, wg_ref[...], preferred_element_type=F32)
    up = jnp.dot(h, wu_ref[...], preferred_element_type=F32)
    o_ref[...] = (gate * jax.nn.sigmoid(gate) * up).astype(o_ref.dtype)


def _mod_specs(cfg, tm, sub):
    per = cfg.ch // tm
    shift = pl.BlockSpec((1, 1, D_MODEL), lambda i, *_: ((i // per) * 9 + sub * 3, 0, 0))
    scale = pl.BlockSpec((1, 1, D_MODEL), lambda i, *_: ((i // per) * 9 + sub * 3 + 1, 0, 0))
    return shift, scale


def _norm_matmul(cfg, x, g, mods, sub, w, col_scale, out_dtype, tm, tn, name):
    n = x.shape[0]
    nout = w.shape[1]
    shift, scale = _mod_specs(cfg, tm, sub)
    return pl.pallas_call(
        functools.partial(_norm_mm_body, tm=tm),
        grid=(n // tm, nout // tn),
        in_specs=[pl.BlockSpec((tm, D_MODEL), lambda i, j: (i, 0)),
                  pl.BlockSpec((1, D_MODEL), lambda i, j: (0, 0)),
                  shift, scale,
                  pl.BlockSpec((D_MODEL, tn), lambda i, j: (0, j)),
                  pl.BlockSpec((1, tn), lambda i, j: (0, j))],
        out_specs=pl.BlockSpec((tm, tn), lambda i, j: (i, j)),
        out_shape=jax.ShapeDtypeStruct((n, nout), out_dtype),
        scratch_shapes=[pltpu.VMEM((tm, D_MODEL), BF16)],
        compiler_params=_params(("arbitrary", "arbitrary")),
        name=name,
    )(x, g.reshape(1, D_MODEL), mods, mods, w, col_scale)


def _norm_matmul_t(cfg, x, g, mods, sub, wt, tm, name):
    n = x.shape[0]
    nout = wt.shape[0]
    shift, scale = _mod_specs(cfg, tm, sub)
    return pl.pallas_call(
        functools.partial(_norm_mm_t_body, tm=tm),
        grid=(n // tm,),
        in_specs=[pl.BlockSpec((tm, D_MODEL), lambda i: (i, 0)),
                  pl.BlockSpec((1, D_MODEL), lambda i: (0, 0)),
                  shift, scale,
                  pl.BlockSpec((nout, D_MODEL), lambda i: (0, 0))],
        out_specs=pl.BlockSpec((nout, tm), lambda i: (0, i)),
        out_shape=jax.ShapeDtypeStruct((nout, n), BF16),
        scratch_shapes=[pltpu.VMEM((tm, D_MODEL), BF16)],
        compiler_params=_params(("arbitrary",)),
        name=name,
    )(x, g.reshape(1, D_MODEL), mods, mods, wt)


def _norm_swiglu(cfg, x, g, mods, sub, w_in, tm, tn):
    n = x.shape[0]
    nj = D_FF // tn
    shift, scale = _mod_specs(cfg, tm, sub)
    return pl.pallas_call(
        functools.partial(_norm_swiglu_body, tm=tm),
        grid=(n // tm, nj),
        in_specs=[pl.BlockSpec((tm, D_MODEL), lambda i, j: (i, 0)),
                  pl.BlockSpec((1, D_MODEL), lambda i, j: (0, 0)),
                  shift, scale,
                  pl.BlockSpec((D_MODEL, tn), lambda i, j: (0, j)),
                  pl.BlockSpec((D_MODEL, tn), lambda i, j: (0, j + nj))],
        out_specs=pl.BlockSpec((tm, tn), lambda i, j: (i, j)),
        out_shape=jax.ShapeDtypeStruct((n, D_FF), BF16),
        scratch_shapes=[pltpu.VMEM((tm, D_MODEL), BF16)],
        compiler_params=_params(("arbitrary", "arbitrary")),
        name="ffn_in",
    )(x, g.reshape(1, D_MODEL), mods, mods, w_in, w_in)


def _resid_mm_body(*refs, pieces, coef):
    lhs = refs[:pieces]
    ws = refs[pieces:2 * pieces]
    x_ref, gate_ref, o_ref = refs[2 * pieces:]
    acc = jnp.dot(lhs[0][...], ws[0][...], preferred_element_type=F32)
    for p in range(1, pieces):
        acc = acc + jnp.dot(lhs[p][...], ws[p][...], preferred_element_type=F32)
    o_ref[...] = x_ref[...] + (coef * gate_ref[0]) * acc


def _resid_matmul(cfg, lhs_list, w, x, mods, sub, coef, tm, tn, name):
    n = x.shape[0]
    per = cfg.ch // tm
    widths = [a.shape[1] for a in lhs_list]
    offs = np.cumsum([0] + widths[:-1])
    in_specs = [pl.BlockSpec((tm, k), lambda i, j: (i, 0)) for k in widths]
    for k, off in zip(widths, offs):
        assert off % k == 0
        in_specs.append(pl.BlockSpec((k, tn), lambda i, j, b=int(off // k): (b, j)))
    in_specs += [pl.BlockSpec((tm, tn), lambda i, j: (i, j)),
                 pl.BlockSpec((1, 1, tn), lambda i, j: ((i // per) * 9 + sub * 3 + 2, 0, j))]
    return pl.pallas_call(
        functools.partial(_resid_mm_body, pieces=len(lhs_list), coef=coef),
        grid=(n // tm, D_MODEL // tn),
        in_specs=in_specs,
        out_specs=pl.BlockSpec((tm, tn), lambda i, j: (i, j)),
        out_shape=jax.ShapeDtypeStruct((n, D_MODEL), F32),
        compiler_params=_params(("arbitrary", "arbitrary")),
        name=name,
    )(*lhs_list, *([w] * len(lhs_list)), x, mods)


def _final_norm_body(x_ref, g_ref, o_ref):
    x = x_ref[...]
    inv = lax.rsqrt(jnp.mean(x * x, axis=-1, keepdims=True) + EPS)
    o_ref[...] = x * inv * g_ref[...]


def _final_norm(x, g, tm=512):
    n = x.shape[0]
    return pl.pallas_call(
        _final_norm_body,
        grid=(n // tm,),
        in_specs=[pl.BlockSpec((tm, D_MODEL), lambda i: (i, 0)),
                  pl.BlockSpec((1, D_MODEL), lambda i: (0, 0))],
        out_specs=pl.BlockSpec((tm, D_MODEL), lambda i: (i, 0)),
        out_shape=jax.ShapeDtypeStruct((n, D_MODEL), F32),
        compiler_params=_params(("arbitrary",)),
        name="final_norm",
    )(x, g.reshape(1, D_MODEL))


def _t5_bucket_np(rel):
    nb = REL_BUCKETS // 2
    max_exact = nb // 2
    rel = np.asarray(rel, np.int64)
    base = np.where(rel > 0, nb, 0)
    n = np.abs(rel)
    nf = np.maximum(n, 1).astype(np.float64)
    large = max_exact + (np.log(nf / max_exact) / math.log(REL_MAX_DIST / max_exact)
                         * (nb - max_exact)).astype(np.int64)
    large = np.minimum(large, nb - 1)
    return (base + np.where(n < max_exact, n, large)).astype(np.int32)


EXPAND_ROWS = 32


def _expand_body(tab_ref, idx_ref, o_ref, *, nb):
    t = pl.program_id(1)

    def chunk(r, carry):
        rows = pl.ds(pl.multiple_of(r * EXPAND_ROWS, EXPAND_ROWS), EXPAND_ROWS)
        idx = idx_ref[0, rows, :]
        pick = lambda b, acc: jnp.where(idx == b, tab_ref[t, b], acc)
        o_ref[0, 0, rows, :] = lax.fori_loop(0, nb, pick, jnp.zeros(idx.shape, F32))
        return carry
    lax.fori_loop(0, idx_ref.shape[1] // EXPAND_ROWS, chunk, 0)


def _expand(tab, idx):
    ntab, nb = tab.shape
    ni, r, c = idx.shape
    assert r % EXPAND_ROWS == 0 and idx.min() >= 0 and idx.max() < nb
    return pl.pallas_call(
        functools.partial(_expand_body, nb=nb),
        grid=(ni, ntab),
        in_specs=[pl.BlockSpec(memory_space=pltpu.SMEM),
                  pl.BlockSpec((1, r, c), lambda i, t: (i, 0, 0))],
        out_specs=pl.BlockSpec((1, 1, r, c), lambda i, t: (i, t, 0, 0)),
        out_shape=jax.ShapeDtypeStruct((ni, ntab, r, c), F32),
        compiler_params=_params(("arbitrary", "arbitrary")),
        name="expand_table",
    )(tab, jnp.asarray(idx.astype(np.int32)))


DIL_BQ_MAX = 256


def _banded_heads(heads, scores, finish):
    s_next = scores(0)
    for h in range(heads):
        s_cur = s_next
        if h + 1 < heads:
            s_next = scores(h + 1)
        finish(h, s_cur)


def _dil_block(cfg, dil):
    return min(DIL_BQ_MAX, cfg.ch // dil)


def _dil_tables(cfg, dil):
    bq = _dil_block(cfg, dil)
    lc = cfg.ch // dil
    nbp = lc // bq
    assert lc % bq == 0 and bq % A_HALF == 0
    per_chunk = cfg.ch // bq
    nblk = cfg.n // bq
    halos = bq // A_HALF
    prev = np.zeros(nblk, np.int32)
    nxt = np.zeros(nblk, np.int32)
    lo = np.zeros(nblk, np.int32)
    hi = np.zeros(nblk, np.int32)
    for i in range(nblk):
        c, blk = divmod(i, per_chunk)
        r, nn = divmod(blk, nbp)
        first_chunk, last_chunk = (0, cfg.pc - 1) if c < cfg.pc else (c, c)
        if nn > 0:
            prev[i], lo[i] = halos * i - 1, 0
        elif c > first_chunk:
            prev[i], lo[i] = ((c - 1) * cfg.ch + r * lc + lc - A_HALF) // A_HALF, 0
        else:
            prev[i], lo[i] = halos * i, A_HALF
        if nn < nbp - 1:
            nxt[i], hi[i] = halos * (i + 1), bq + 2 * A_HALF
        elif c < last_chunk:
            nxt[i], hi[i] = ((c + 1) * cfg.ch + r * lc) // A_HALF, bq + 2 * A_HALF
        else:
            nxt[i], hi[i] = halos * (i + 1) - 1, bq + A_HALF
    return prev, nxt, lo, hi


def _dil_body(prev_ref, nxt_ref, lo_ref, hi_ref, q_ref, kp_ref, km_ref, kn_ref,
              vp_ref, vm_ref, vn_ref, bias_ref, o_ref, lse_ref):
    i = pl.program_id(0)
    bq = q_ref.shape[0]
    nk = bq + 2 * A_HALF
    row = lax.broadcasted_iota(jnp.int32, (bq, nk), 0)
    col = lax.broadcasted_iota(jnp.int32, (bq, nk), 1)
    off = col - row
    valid = (off >= 0) & (off <= 2 * A_HALF) & (col >= lo_ref[i]) & (col < hi_ref[i])
    k = jnp.concatenate([kp_ref[...], km_ref[...], kn_ref[...]], axis=0)
    v = jnp.concatenate([vp_ref[...], vm_ref[...], vn_ref[...]], axis=0)

    def scores(h):
        cols = slice(h * HEAD_DIM, (h + 1) * HEAD_DIM)
        s = lax.dot_general(q_ref[:, cols], k[:, cols], (((1,), (1,)), ((), ())),
                            preferred_element_type=F32)
        return jnp.where(valid, s + bias_ref[h], NEG)

    def finish(h, s):
        cols = slice(h * HEAD_DIM, (h + 1) * HEAD_DIM)
        m = jnp.max(s, axis=-1, keepdims=True)
        e = jnp.exp(s - m)
        den = jnp.sum(e, axis=-1, keepdims=True)
        o = jnp.dot(e.astype(BF16), v[:, cols], preferred_element_type=F32)
        o_ref[:, cols] = (o / den).astype(o_ref.dtype)
        lse_ref[:, h:h + 1] = m + jnp.log(den)

    _banded_heads(A_HEADS, scores, finish)


def _dilated_branch(cfg, dil, q_arr, k_arr, v_arr, qcol, kcol, vcol, bias):
    n = cfg.n
    w = A_HEADS * HEAD_DIM
    bq = _dil_block(cfg, dil)
    prev, nxt, lo, hi = _dil_tables(cfg, dil)
    main = lambda cb: pl.BlockSpec((bq, w), lambda i, p, x, l, h: (i, cb))
    before = lambda cb: pl.BlockSpec((A_HALF, w), lambda i, p, x, l, h: (p[i], cb))
    after = lambda cb: pl.BlockSpec((A_HALF, w), lambda i, p, x, l, h: (x[i], cb))
    out_spec = pl.BlockSpec((bq, w), lambda i, p, x, l, h: (i, 0))
    lse_spec = pl.BlockSpec((bq, A_HEADS), lambda i, p, x, l, h: (i, 0))
    grid_spec = pltpu.PrefetchScalarGridSpec(
        num_scalar_prefetch=4,
        grid=(n // bq,),
        in_specs=[main(qcol), before(kcol), main(kcol), after(kcol),
                  before(vcol), main(vcol), after(vcol),
                  pl.BlockSpec((A_HEADS, bq, bq + 2 * A_HALF),
                               lambda i, p, x, l, h: (0, 0, 0))],
        out_specs=[out_spec, lse_spec],
    )
    return pl.pallas_call(
        _dil_body,
        grid_spec=grid_spec,
        out_shape=[jax.ShapeDtypeStruct((n, w), BF16), jax.ShapeDtypeStruct((n, A_HEADS), F32)],
        compiler_params=_params(("arbitrary",)),
        name=f"dilated_d{dil}",
    )(jnp.asarray(prev), jnp.asarray(nxt), jnp.asarray(lo), jnp.asarray(hi),
      q_arr, k_arr, k_arr, k_arr, v_arr, v_arr, v_arr, bias)


def _dil_bias(cfg, rel_bias, g, dil):
    bq = _dil_block(cfg, dil)
    off = np.arange(bq + 2 * A_HALF)[None, :] - np.arange(bq)[:, None] - A_HALF
    bucket = _t5_bucket_np(dil * np.clip(off, -A_HALF, A_HALF))
    return _expand(rel_bias[:, g * A_HEADS:(g + 1) * A_HEADS].T, bucket[None])[0]


def _merge_body(o1, l1, o2, l2, o3, l3, out_ref):
    a, b, c = l1[...], l2[...], l3[...]
    m = jnp.maximum(jnp.maximum(a, b), c)
    ea, eb, ec = jnp.exp(a - m), jnp.exp(b - m), jnp.exp(c - m)
    tot = ea + eb + ec
    wa, wb, wc = ea / tot, eb / tot, ec / tot
    for h in range(A_HEADS):
        cols = slice(h * HEAD_DIM, (h + 1) * HEAD_DIM)
        hh = slice(h, h + 1)
        out = (wa[:, hh] * o1[:, cols].astype(F32) + wb[:, hh] * o2[:, cols].astype(F32)
               + wc[:, hh] * o3[:, cols].astype(F32))
        out_ref[:, cols] = out.astype(out_ref.dtype)


def _merge_branches(outs, lses, tm=512):
    n, w = outs[0].shape
    spec = pl.BlockSpec((tm, w), lambda i: (i, 0))
    lspec = pl.BlockSpec((tm, A_HEADS), lambda i: (i, 0))
    args = [a for pair in zip(outs, lses) for a in pair]
    return pl.pallas_call(
        _merge_body,
        grid=(n // tm,),
        in_specs=[spec, lspec] * 3,
        out_specs=spec,
        out_shape=jax.ShapeDtypeStruct((n, w), BF16),
        compiler_params=_params(("arbitrary",)),
        name="dilated_merge",
    )(*args)


def _to_residue_major(cfg, a, dil):
    n, w = a.shape
    return a.reshape(n // cfg.ch, cfg.ch // dil, dil, w).transpose(0, 2, 1, 3).reshape(n, w)


def _from_residue_major(cfg, a, dil):
    n, w = a.shape
    return a.reshape(n // cfg.ch, dil, cfg.ch // dil, w).transpose(0, 2, 1, 3).reshape(n, w)


def _flash_worklist(cfg, tq, tk, rel_lo=None, rel_hi=None):
    qb, kb, tile, flags = [], [], [], []
    for start, length in cfg.seqs():
        assert length % tq == 0 and length % tk == 0 and start % tq == 0 and start % tk == 0
        nk = length // tk
        for qi in range(length // tq):
            for kj in range(nk):
                qb.append(start // tq + qi)
                kb.append(start // tk + kj)
                if rel_lo is not None:
                    assert (kj * tk) % tq == 0
                    d = (kj * tk - qi * tq) // tq
                    tile.append(min(max(d, rel_lo), rel_hi) - rel_lo)
                else:
                    tile.append(0)
                flags.append((1 if kj == 0 else 0) | (2 if kj == nk - 1 else 0))
    as_i32 = lambda a: jnp.asarray(np.asarray(a, np.int32))
    return as_i32(qb), as_i32(kb), as_i32(tile), as_i32(flags)


def _flash_init(flags, m_scr, acc_scr):
    @pl.when((flags & 1) != 0)
    def _():
        m_scr[...] = jnp.full(m_scr.shape, -jnp.inf, F32)
        acc_scr[...] = jnp.zeros(acc_scr.shape, F32)


def _flash_softmax(h, s, m_scr):
    m_prev = m_scr[h]
    m_new = jnp.maximum(m_prev, jnp.max(s, axis=0, keepdims=True))
    m_scr[h] = m_new
    return jnp.exp2(m_prev - m_new), jnp.exp2(s - m_new).astype(BF16)


def _flash_accumulate(h, alpha, p, vt, acc_scr):
    lhs = jnp.concatenate([vt, jnp.ones((ONES_ROWS, vt.shape[1]), BF16)], axis=0)
    acc_scr[h] = alpha * acc_scr[h] + jnp.dot(lhs, p, preferred_element_type=F32)


def _flash_heads(heads, scores, vt_rows, m_scr, acc_scr):
    s_next = scores(0)
    pending = None
    for h in range(heads):
        s_cur = s_next
        if h + 1 < heads:
            s_next = scores(h + 1)
        alpha, p = _flash_softmax(h, s_cur, m_scr)
        if pending is not None:
            _flash_accumulate(*pending, vt_rows(pending[0]), acc_scr)
        pending = (h, alpha, p)
    _flash_accumulate(*pending, vt_rows(pending[0]), acc_scr)


DIFF_TQ = 256
DIFF_TK = 512


def _diff_tile_range(max_len):
    rel = np.arange(-max_len + 1, max_len)
    b = _t5_bucket_np(rel)
    sat_pos = int(rel[b != b[-1]].max()) + 1
    sat_neg = int(rel[b != b[0]].min()) - 1
    hi = -(-(sat_pos + DIFF_TQ - 1) // DIFF_TQ)
    lo = (sat_neg - (DIFF_TK - 1)) // DIFF_TQ
    return lo, hi


def _diff_bias_tiles(rel_bias, lo, hi):
    d = np.arange(lo, hi + 1)[:, None, None] * DIFF_TQ
    rel = d + np.arange(DIFF_TK)[None, :, None] - np.arange(DIFF_TQ)[None, None, :]
    return _expand(rel_bias[:, 3 * A_HEADS:].T * LOG2E, _t5_bucket_np(rel))


def _diff_body(qb_ref, kb_ref, tile_ref, fl_ref, q_ref, k_ref, vt_ref, bias_ref, lam_ref, g_ref,
               o_ref, m_scr, acc_scr, *, lam_init):
    flags = fl_ref[pl.program_id(0)]
    tq = DIFF_TQ
    _flash_init(flags, m_scr, acc_scr)

    lane = lax.broadcasted_iota(jnp.int32, (tq, HEAD_DIM), 1)

    def scores(h):
        cols = slice(h * HEAD_DIM, (h + 1) * HEAD_DIM)
        qh = q_ref[:, cols]
        zero = jnp.zeros_like(qh)
        q2 = jnp.concatenate([jnp.where(lane < B_QK_DIM, qh, zero),
                              jnp.where(lane >= B_QK_DIM, qh, zero)], axis=0)
        s = lax.dot_general(k_ref[:, cols], q2, (((1,), (1,)), ((), ())),
                            preferred_element_type=F32)
        b = bias_ref[0, h]
        return s + jnp.concatenate([b, b], axis=1)

    vt_rows = lambda h: vt_ref[h * HEAD_DIM:(h + 1) * HEAD_DIM, :]
    _flash_heads(B_HEADS, scores, vt_rows, m_scr, acc_scr)

    @pl.when((flags & 2) != 0)
    def _():
        lf = lam_ref[...]
        lam = (jnp.exp(jnp.sum(lf[0:1] * lf[1:2], axis=-1, keepdims=True))
               - jnp.exp(jnp.sum(lf[2:3] * lf[3:4], axis=-1, keepdims=True)) + lam_init)
        for h in range(B_HEADS):
            cols = slice(h * HEAD_DIM, (h + 1) * HEAD_DIM)
            acc = acc_scr[h]
            att = acc[:HEAD_DIM] / acc[HEAD_DIM:HEAD_DIM + 1]
            o = att[:, :tq] - lam * att[:, tq:]
            inv = lax.rsqrt(jnp.mean(o * o, axis=0, keepdims=True) + EPS)
            o_ref[:, cols] = (((o * inv).T * g_ref[...]) * (1.0 - lam_init)).astype(o_ref.dtype)


def _diff_attention(cfg, proj, qcol, kcol, vt, bias_tiles, tile_lo, tile_hi, lam, subln_g,
                    lam_init):
    n = cfg.n
    tq, tk = DIFF_TQ, DIFF_TK
    qb, kb, tile, flags = _flash_worklist(cfg, tq, tk, tile_lo, tile_hi)
    grid_spec = pltpu.PrefetchScalarGridSpec(
        num_scalar_prefetch=4,
        grid=(qb.shape[0],),
        in_specs=[pl.BlockSpec((tq, B_W), lambda s, q, k, t, f: (q[s], qcol)),
                  pl.BlockSpec((tk, B_W), lambda s, q, k, t, f: (k[s], kcol)),
                  pl.BlockSpec((B_W, tk), lambda s, q, k, t, f: (0, k[s])),
                  pl.BlockSpec((1, B_HEADS, tk, tq), lambda s, q, k, t, f: (t[s], 0, 0, 0)),
                  pl.BlockSpec((4, B_QK_DIM), lambda s, q, k, t, f: (0, 0)),
                  pl.BlockSpec((1, HEAD_DIM), lambda s, q, k, t, f: (0, 0))],
        out_specs=pl.BlockSpec((tq, B_W), lambda s, q, k, t, f: (q[s], 0)),
        scratch_shapes=[pltpu.VMEM((B_HEADS, 1, 2 * tq), F32),
                        pltpu.VMEM((B_HEADS, HEAD_DIM + ONES_ROWS, 2 * tq), F32)],
    )
    return pl.pallas_call(
        functools.partial(_diff_body, lam_init=lam_init),
        grid_spec=grid_spec,
        out_shape=jax.ShapeDtypeStruct((n, B_W), BF16),
        compiler_params=_params(("arbitrary",)),
        name="diff_attention",
    )(qb, kb, tile, flags, proj, proj, vt, bias_tiles, lam, subln_g.reshape(1, HEAD_DIM))


MLA_TM = 512
MLA_TQ = 512
MLA_TK = 512


def _rope_tables(max_len):
    inv = ROPE_THETA ** (-jnp.arange(0, C_ROPE, 2, dtype=F32) / C_ROPE)
    ang = jnp.arange(max_len, dtype=F32)[:, None] * inv[None, :]
    cos, sin = jnp.cos(ang), jnp.sin(ang)
    zero = jnp.zeros((max_len, HEAD_DIM - C_ROPE), F32)
    return (jnp.concatenate([cos, cos, zero], axis=1),
            jnp.concatenate([-sin, sin, zero], axis=1))


def _mla_prep_body(p_ref, gq_ref, gkv_ref, wqa_ref, wqb_ref, wk_ref, wvt_ref, cos_ref, sin_ref,
                   q_ref, k_ref, vt_ref, *, scale):
    def normed(x, g):
        inv = lax.rsqrt(jnp.mean(x * x, axis=-1, keepdims=True) + EPS)
        return (x * inv * g).astype(BF16)

    cq = normed(p_ref[:, 0:C_LORA], gq_ref[...])
    ckv = normed(p_ref[:, C_LORA:2 * C_LORA], gkv_ref[...])
    cos = cos_ref[...]
    sin = sin_ref[...]
    qa = jnp.dot(cq, wqa_ref[...], preferred_element_type=F32)
    qb = jnp.dot(cq, wqb_ref[...], preferred_element_type=F32)
    kn = jnp.dot(ckv, wk_ref[...], preferred_element_type=F32)
    vt_ref[...] = lax.dot_general(wvt_ref[...], ckv, (((1,), (1,)), ((), ())),
                                  preferred_element_type=F32).astype(BF16)
    kr = p_ref[:, 2 * C_LORA:2 * C_LORA + HEAD_DIM]
    kr_sw = p_ref[:, 2 * C_LORA + HEAD_DIM:2 * C_LORA + 2 * HEAD_DIM]
    k_rope = (kr * cos + kr_sw * sin).astype(BF16)
    for h in range(C_HEADS):
        a0 = h * C_QK_PAD
        a1 = a0 + HEAD_DIM
        a2 = a0 + C_QK_PAD
        b = slice(h * HEAD_DIM, (h + 1) * HEAD_DIM)
        q_ref[:, a0:a1] = (qa[:, a0:a1] * scale).astype(BF16)
        q_ref[:, a1:a2] = ((qa[:, a1:a2] * cos + qb[:, b] * sin) * scale).astype(BF16)
        k_ref[:, a0:a1] = kn[:, b].astype(BF16)
        k_ref[:, a1:a2] = k_rope


def _mla_prep(cfg, proj_c, gq, gkv, wqa, wqb, wk, wvt, cos_tab, sin_tab):
    n = cfg.n
    tm = MLA_TM
    pblocks = cfg.pc * cfg.ch // tm
    per = cfg.ch // tm
    pos = lambda i: (jnp.where(i < pblocks, i, (i - pblocks) % per), 0)
    full = lambda shape: pl.BlockSpec(shape, lambda i: (0, 0))
    qk_w = C_HEADS * C_QK_PAD
    v_w = C_HEADS * HEAD_DIM
    return pl.pallas_call(
        functools.partial(_mla_prep_body, scale=(C_NOPE + C_ROPE) ** -0.5 * LOG2E),
        grid=(n // tm,),
        in_specs=[pl.BlockSpec((tm, proj_c.shape[1]), lambda i: (i, 0)),
                  full((1, C_LORA)), full((1, C_LORA)),
                  full(wqa.shape), full(wqb.shape), full(wk.shape), full(wvt.shape),
                  pl.BlockSpec((tm, HEAD_DIM), pos), pl.BlockSpec((tm, HEAD_DIM), pos)],
        out_specs=[pl.BlockSpec((tm, qk_w), lambda i: (i, 0)),
                   pl.BlockSpec((tm, qk_w), lambda i: (i, 0)),
                   pl.BlockSpec((v_w, tm), lambda i: (0, i))],
        out_shape=[jax.ShapeDtypeStruct((n, qk_w), BF16),
                   jax.ShapeDtypeStruct((n, qk_w), BF16),
                   jax.ShapeDtypeStruct((v_w, n), BF16)],
        compiler_params=_params(("arbitrary",)),
        name="mla_prep",
    )(proj_c, gq.reshape(1, C_LORA), gkv.reshape(1, C_LORA), wqa, wqb, wk, wvt, cos_tab, sin_tab)


def _mla_body(qb_ref, kb_ref, tile_ref, fl_ref, q_ref, k_ref, vt_ref, o_ref, m_scr, acc_scr):
    flags = fl_ref[pl.program_id(0)]
    _flash_init(flags, m_scr, acc_scr)

    def scores(h):
        qk = slice(h * C_QK_PAD, (h + 1) * C_QK_PAD)
        return lax.dot_general(k_ref[:, qk], q_ref[:, qk], (((1,), (1,)), ((), ())),
                               preferred_element_type=F32)

    vt_rows = lambda h: vt_ref[h * HEAD_DIM:(h + 1) * HEAD_DIM, :]
    _flash_heads(C_HEADS, scores, vt_rows, m_scr, acc_scr)

    @pl.when((flags & 2) != 0)
    def _():
        for h in range(C_HEADS):
            vc = slice(h * HEAD_DIM, (h + 1) * HEAD_DIM)
            acc = acc_scr[h]
            o_ref[:, vc] = (acc[:HEAD_DIM] / acc[HEAD_DIM:HEAD_DIM + 1]).T.astype(o_ref.dtype)


def _mla_attention(cfg, q, k, vt):
    n = cfg.n
    tq, tk = MLA_TQ, MLA_TK
    qb, kb, tile, flags = _flash_worklist(cfg, tq, tk)
    qk_w = C_HEADS * C_QK_PAD
    v_w = C_HEADS * HEAD_DIM
    grid_spec = pltpu.PrefetchScalarGridSpec(
        num_scalar_prefetch=4,
        grid=(qb.shape[0],),
        in_specs=[pl.BlockSpec((tq, qk_w), lambda s, q_, k_, t, f: (q_[s], 0)),
                  pl.BlockSpec((tk, qk_w), lambda s, q_, k_, t, f: (k_[s], 0)),
                  pl.BlockSpec((v_w, tk), lambda s, q_, k_, t, f: (0, k_[s]))],
        out_specs=pl.BlockSpec((tq, v_w), lambda s, q_, k_, t, f: (q_[s], 0)),
        scratch_shapes=[pltpu.VMEM((C_HEADS, 1, tq), F32),
                        pltpu.VMEM((C_HEADS, HEAD_DIM + ONES_ROWS, tq), F32)],
    )
    return pl.pallas_call(
        _mla_body,
        grid_spec=grid_spec,
        out_shape=jax.ShapeDtypeStruct((n, v_w), BF16),
        compiler_params=_params(("arbitrary",)),
        name="mla_attention",
    )(qb, kb, tile, flags, q, k, vt)


def _na_tables(cfg):
    prev, nxt, var = [], [], []
    for start, length in cfg.seqs():
        nb = length // NA_BLOCK
        assert length % NA_BLOCK == 0 and start % NA_BLOCK == 0 and nb >= 3
        b0 = start // NA_BLOCK
        for r in range(nb):
            prev.append(b0 + max(r - 1, 0))
            nxt.append(b0 + min(r + 1, nb - 1))
            var.append(0 if r == 0 else (2 if r == nb - 1 else 1))
    as_i32 = lambda a: jnp.asarray(np.asarray(a, np.int32))
    return as_i32(prev), as_i32(nxt), as_i32(var)


def _na_window_start(a, variant):
    centred = a - NA_ROWS // 2
    return (max(centred, 0), centred, min(centred, 0))[variant]


def _na_bias_tables(rpb):
    ncol = 2 * NA_COLS - 1
    c = np.arange(GRID_W)[:, None, None]
    b = np.arange(NA_ROWS)[None, :, None]
    kc = np.arange(GRID_W)[None, None, :]
    cstart = np.clip(c - NA_COLS // 2, 0, GRID_W - NA_COLS)
    col_ok = (kc >= cstart) & (kc < cstart + NA_COLS)
    dc = np.clip(kc - c, -(NA_COLS - 1), NA_COLS - 1) + (NA_COLS - 1)
    masked = NA_ROWS * ncol
    idx = np.where(col_ok, b * ncol + dc, masked).reshape(1, GRID_W, NA_BLOCK)
    rows = np.arange(NA_ROWS)[:, None] + np.arange(NA_ROWS)[None, :]
    tab = rpb[:, rows, :].reshape(D_HEADS * NA_ROWS, NA_ROWS * ncol)
    tab = jnp.concatenate([tab, jnp.full((tab.shape[0], 1), NEG, F32)], axis=1)
    slabs = _expand(tab, idx)[0].reshape(D_HEADS, NA_ROWS, GRID_W, NA_BLOCK)
    variants = []
    for variant in range(3):
        row_blocks = []
        for a in range(NA_ROWS):
            start = _na_window_start(a, variant)
            left = (NA_ROWS + start) * GRID_W
            row_blocks.append(jnp.pad(slabs[:, start - a + NA_ROWS - 1],
                                      ((0, 0), (0, 0), (left, 2 * NA_BLOCK - left)),
                                      constant_values=NEG))
        variants.append(jnp.concatenate(row_blocks, axis=1))
    return jnp.stack(variants, axis=0)


def _na_body(prev_ref, nxt_ref, var_ref, q_ref, kp_ref, km_ref, kn_ref, vp_ref, vm_ref, vn_ref,
             tab_ref, o_ref):
    k = jnp.concatenate([kp_ref[...], km_ref[...], kn_ref[...]], axis=0)
    v = jnp.concatenate([vp_ref[...], vm_ref[...], vn_ref[...]], axis=0)
    def scores(h):
        cols = slice(h * HEAD_DIM, (h + 1) * HEAD_DIM)
        s = lax.dot_general(q_ref[:, cols], k[:, cols], (((1,), (1,)), ((), ())),
                            preferred_element_type=F32)
        return s + tab_ref[0, h]

    def finish(h, s):
        cols = slice(h * HEAD_DIM, (h + 1) * HEAD_DIM)
        m = jnp.max(s, axis=-1, keepdims=True)
        e = jnp.exp(s - m)
        den = jnp.sum(e, axis=-1, keepdims=True)
        o = jnp.dot(e.astype(BF16), v[:, cols], preferred_element_type=F32)
        o_ref[:, cols] = (o / den).astype(o_ref.dtype)

    _banded_heads(D_HEADS, scores, finish)


def _na_attention(cfg, qkv, tabs):
    n = cfg.n
    w = D_HEADS * HEAD_DIM
    prev, nxt, var = _na_tables(cfg)
    own = lambda cb: pl.BlockSpec((NA_BLOCK, w), lambda i, p, x, t: (i, cb))
    before = lambda cb: pl.BlockSpec((NA_BLOCK, w), lambda i, p, x, t: (p[i], cb))
    after = lambda cb: pl.BlockSpec((NA_BLOCK, w), lambda i, p, x, t: (x[i], cb))
    grid_spec = pltpu.PrefetchScalarGridSpec(
        num_scalar_prefetch=3,
        grid=(n // NA_BLOCK,),
        in_specs=[own(0), before(1), own(1), after(1), before(2), own(2), after(2),
                  pl.BlockSpec((1, D_HEADS, NA_BLOCK, 3 * NA_BLOCK),
                               lambda i, p, x, t: (t[i], 0, 0, 0))],
        out_specs=pl.BlockSpec((NA_BLOCK, w), lambda i, p, x, t: (i, 0)),
    )
    return pl.pallas_call(
        _na_body,
        grid_spec=grid_spec,
        out_shape=jax.ShapeDtypeStruct((n, w), BF16),
        compiler_params=_params(("arbitrary",)),
        name="na_attention",
    )(prev, nxt, var, qkv, qkv, qkv, qkv, qkv, qkv, qkv, tabs)


EVEN_PROJ = A_IN + 2 * B_W


def _even_col_scale():
    cs = np.ones((1, EVEN_PROJ), np.float32)
    cs[0, :3 * A_HEADS * HEAD_DIM] = HEAD_DIM ** -0.5
    cs[0, A_IN:A_IN + B_W] = B_QK_DIM ** -0.5 * LOG2E
    return jnp.asarray(cs)


def _even_tables(cfg, rel_bias):
    dil = [_dil_bias(cfg, rel_bias, gi, d) for gi, (_, d) in enumerate(A_CONFIGS)]
    tile_lo, tile_hi = _diff_tile_range(cfg.pc * cfg.ch)
    return dil, _diff_bias_tiles(rel_bias, tile_lo, tile_hi), tile_lo, tile_hi


def _even_mixer(cfg, x, g, mods, w_in, wvt, w_out, lam, subln_g, tables, lam_init, tm):
    w = A_HEADS * HEAD_DIM
    dil_bias, diff_tiles, tile_lo, tile_hi = tables
    proj = _norm_matmul(cfg, x, g, mods, 1, w_in, _even_col_scale(), BF16, tm, 1024, "even_in")
    vt = _norm_matmul_t(cfg, x, g, mods, 1, wvt, tm, "even_in_vt")
    outs, lses = [], []
    for gi, (_, dil) in enumerate(A_CONFIGS):
        if dil == 1:
            o, l = _dilated_branch(cfg, dil, proj, proj, proj, gi, 3 + gi, 6 + gi, dil_bias[gi])
        else:
            q, k, v = (_to_residue_major(cfg, proj[:, (s * 3 + gi) * w:(s * 3 + gi + 1) * w], dil)
                       for s in range(3))
            o, l = _dilated_branch(cfg, dil, q, k, v, 0, 0, 0, dil_bias[gi])
            o, l = _from_residue_major(cfg, o, dil), _from_residue_major(cfg, l, dil)
        outs.append(o)
        lses.append(l)
    o_a = _merge_branches(outs, lses)
    cb = A_IN // B_W
    o_b = _diff_attention(cfg, proj, cb, cb + 1, vt, diff_tiles, tile_lo, tile_hi, lam, subln_g,
                          lam_init)
    return _resid_matmul(cfg, [o_a, o_b], w_out, x, mods, 1, 1.0, tm, 512, "even_out")


def _odd_weights(w_in, w_q_up, w_kv_up):
    half = C_ROPE // 2
    swap = np.concatenate([np.arange(half, C_ROPE), np.arange(half)])
    o2 = 2 * C_LORA
    zpad = jnp.zeros((D_MODEL, HEAD_DIM - C_ROPE), w_in.dtype)
    kr = w_in[:, o2:o2 + C_ROPE]
    w_c = jnp.concatenate([w_in[:, :o2], kr, zpad, kr[:, swap], zpad], axis=1)
    w_d = w_in[:, o2 + C_ROPE:]
    q3 = w_q_up.reshape(C_LORA, C_HEADS, C_NOPE + C_ROPE)
    zq = jnp.zeros((C_LORA, C_HEADS, C_QK_PAD - C_NOPE - C_ROPE), w_q_up.dtype)
    wqa = jnp.concatenate([q3, zq], axis=2).reshape(C_LORA, C_HEADS * C_QK_PAD)
    wqb = jnp.concatenate([q3[:, :, C_NOPE:][:, :, swap], zq], axis=2).reshape(
        C_LORA, C_HEADS * HEAD_DIM)
    kv3 = w_kv_up.reshape(C_LORA, C_HEADS, 2 * HEAD_DIM)
    wk = kv3[:, :, :C_NOPE].reshape(C_LORA, C_HEADS * HEAD_DIM)
    wvt = kv3[:, :, C_NOPE:].reshape(C_LORA, C_HEADS * HEAD_DIM).T
    return w_c, w_d, wqa, wqb, wk, wvt


def _odd_mixer(cfg, x, g, mods, weights, gq, gkv, rpb, w_out, rope, tm):
    w_c, w_d, wqa, wqb, wk, wvt = weights
    ones_c = jnp.ones((1, w_c.shape[1]), F32)
    proj_c = _norm_matmul(cfg, x, g, mods, 1, w_c, ones_c, F32, tm, w_c.shape[1], "odd_in_latent")
    cs = np.ones((1, w_d.shape[1]), np.float32)
    cs[0, :D_HEADS * HEAD_DIM] = HEAD_DIM ** -0.5
    qkv_d = _norm_matmul(cfg, x, g, mods, 1, w_d, jnp.asarray(cs), BF16, tm, w_d.shape[1],
                         "odd_in_na")
    q, k, vt = _mla_prep(cfg, proj_c, gq, gkv, wqa, wqb, wk, wvt, *rope)
    o_c = _mla_attention(cfg, q, k, vt)
    o_d = _na_attention(cfg, qkv_d, _na_bias_tables(rpb))
    return _resid_matmul(cfg, [o_c, o_d], w_out, x, mods, 1, 1.0, tm, 512, "odd_out")


def _trunk(cfg, x, c_pad, ada_w, ada_b, norm_g, ffn_w_in, ffn_w_out, rel_bias, ev_w_in, ev_w_out,
           diff_lambda, diff_subln_g, od_w_in, mla_q_norm_g, mla_kv_norm_g, mla_w_q_up,
           mla_w_kv_up, na_rpb, od_w_out, final_norm_g):
    depth = ada_w.shape[0]
    tm = 1024
    nseq = 1 + cfg.sb
    mod_all = _modulation(c_pad, ada_w, ada_b)[:, :nseq]
    chunk_seq = np.array([0] * cfg.pc + list(range(1, nseq)))
    mod_all = mod_all[:, chunk_seq].reshape(depth, (cfg.pc + cfg.sb) * 9, 1, D_MODEL)
    rope = _rope_tables(cfg.pc * cfg.ch)
    even_tables = _even_tables(cfg, rel_bias)
    for i in range(depth):
        mods = mod_all[i]
        j = i // 2
        hid = _norm_swiglu(cfg, x, norm_g[i, 0], mods, 0, ffn_w_in[i, 0].astype(BF16), tm, 512)
        x = _resid_matmul(cfg, [hid], ffn_w_out[i, 0].astype(BF16), x, mods, 0, 0.5, tm, 512,
                          "ffn_out")
        if i % 2 == 0:
            w_in = ev_w_in[j].astype(BF16)
            x = _even_mixer(cfg, x, norm_g[i, 1], mods, w_in[:, :EVEN_PROJ], w_in[:, EVEN_PROJ:].T,
                            ev_w_out[j].astype(BF16), diff_lambda[j], diff_subln_g[j], even_tables,
                            0.8 - 0.6 * math.exp(-0.3 * i), tm)
        else:
            weights = tuple(a.astype(BF16) for a in
                            _odd_weights(od_w_in[j], mla_w_q_up[j], mla_w_kv_up[j]))
            x = _odd_mixer(cfg, x, norm_g[i, 1], mods, weights, mla_q_norm_g[j], mla_kv_norm_g[j],
                           na_rpb[j], od_w_out[j].astype(BF16), rope, tm)
        hid = _norm_swiglu(cfg, x, norm_g[i, 2], mods, 2, ffn_w_in[i, 1].astype(BF16), tm, 512)
        x = _resid_matmul(cfg, [hid], ffn_w_out[i, 1].astype(BF16), x, mods, 2, 0.5, tm, 512,
                          "ffn_out")
    return _final_norm(x, final_norm_g)


def kernel(x_prompt, x_sample, c_prompt, c_sample, ada_w, ada_b, norm_g, ffn_w_in, ffn_w_out, rel_bias, ev_w_in, ev_w_out, diff_lambda, diff_subln_g, od_w_in, mla_q_norm_g, mla_kv_norm_g, mla_w_q_up, mla_w_kv_up, na_rpb, od_w_out, final_norm_g):
    pb, pt, _ = x_prompt.shape
    sb, st, _ = x_sample.shape
    assert pb == 1 and pt % st == 0
    cfg = Cfg(ch=st, pc=pt // st, sb=sb)
    x = jnp.concatenate([x_prompt.reshape(-1, D_MODEL), x_sample.reshape(-1, D_MODEL)], axis=0)
    c = jnp.concatenate([c_prompt, c_sample], axis=0)
    c_pad = jnp.pad(c, ((0, -c.shape[0] % 8), (0, 0)))
    y = _trunk(cfg, x, c_pad, ada_w, ada_b, norm_g, ffn_w_in, ffn_w_out, rel_bias, ev_w_in,
               ev_w_out, diff_lambda, diff_subln_g, od_w_in, mla_q_norm_g, mla_kv_norm_g,
               mla_w_q_up, mla_w_kv_up, na_rpb, od_w_out, final_norm_g)
    return (y[:pt].reshape(x_prompt.shape), y[pt:].reshape(x_sample.shape))
```

```python
import functools
import math
from typing import NamedTuple

import numpy as np
import jax
import jax.numpy as jnp
from jax import lax
from jax.experimental import pallas as pl
from jax.experimental.pallas import tpu as pltpu

F32 = jnp.float32
BF16 = jnp.bfloat16

D_MODEL = 2048
D_FF = 5632
HEAD_DIM = 128
A_HEADS = 8
A_CONFIGS = ((128, 1), (512, 4), (2048, 16))
A_HALF = 64
A_IN = 3 * 3 * A_HEADS * HEAD_DIM
B_HEADS = 8
B_QK_DIM = 64
B_W = B_HEADS * 2 * B_QK_DIM
EVEN_IN = A_IN + 3 * B_W
C_HEADS = 12
C_LORA = 512
C_NOPE = 128
C_ROPE = 64
C_QK_PAD = 256
ROPE_THETA = 10000.0
D_HEADS = 4
GRID_W = 64
NA_ROWS = 8
NA_COLS = 16
NA_BLOCK = NA_ROWS * GRID_W
REL_BUCKETS = 32
REL_MAX_DIST = 1024
EPS = 1e-6
NEG = -1e30
LOG2E = math.log2(math.e)
ONES_ROWS = 16

V7X_VMEM_BYTES = 64 * 1024 * 1024
VMEM_LIMIT = V7X_VMEM_BYTES - 8 * 1024 * 1024


class Cfg(NamedTuple):
    ch: int
    pc: int
    sb: int

    @property
    def n(self):
        return self.ch * (self.pc + self.sb)

    def seqs(self):
        out = [(0, self.pc * self.ch)]
        out += [((self.pc + b) * self.ch, self.ch) for b in range(self.sb)]
        return out


def _params(sem):
    return pltpu.CompilerParams(dimension_semantics=sem, vmem_limit_bytes=VMEM_LIMIT)


def _mod_body(c_ref, w_ref, b_ref, o_ref):
    c = c_ref[...]
    act = (c * jax.nn.sigmoid(c)).astype(BF16)
    o_ref[0] = jnp.dot(act, w_ref[0].astype(BF16), preferred_element_type=F32) + b_ref[0]


def _modulation(c_pad, ada_w, ada_b):
    depth, _, nout = ada_w.shape
    r = c_pad.shape[0]
    tn = 1024
    return pl.pallas_call(
        _mod_body,
        grid=(depth, nout // tn),
        in_specs=[pl.BlockSpec((r, D_MODEL), lambda l, j: (0, 0)),
                  pl.BlockSpec((1, D_MODEL, tn), lambda l, j: (l, 0, j)),
                  pl.BlockSpec((1, 1, tn), lambda l, j: (l, 0, j))],
        out_specs=pl.BlockSpec((1, r, tn), lambda l, j: (l, 0, j)),
        out_shape=jax.ShapeDtypeStruct((depth, r, nout), F32),
        compiler_params=_params(("arbitrary", "arbitrary")),
        name="modulation",
    )(c_pad, ada_w, ada_b.reshape(depth, 1, nout))


NORM_ROWS = 32


def _norm_rows(x_ref, g_ref, sh_ref, sc_ref, h_scr, tm):
    gain = g_ref[...] * (1.0 + sc_ref[0])
    shift = sh_ref[0]

    def body(r, carry):
        rows = pl.ds(pl.multiple_of(r * NORM_ROWS, NORM_ROWS), NORM_ROWS)
        x = x_ref[rows, :]
        inv = lax.rsqrt(jnp.mean(x * x, axis=-1, keepdims=True) + EPS)
        h_scr[rows, :] = ((x * inv) * gain + shift).astype(BF16)
        return carry
    lax.fori_loop(0, tm // NORM_ROWS, body, 0, unroll=4)


def _norm_mm_body(x_ref, g_ref, sh_ref, sc_ref, w_ref, cs_ref, o_ref, h_scr, *, tm):
    @pl.when(pl.program_id(1) == 0)
    def _():
        _norm_rows(x_ref, g_ref, sh_ref, sc_ref, h_scr, tm)
    acc = jnp.dot(h_scr[...], w_ref[...], preferred_element_type=F32)
    o_ref[...] = (acc * cs_ref[...]).astype(o_ref.dtype)


def _norm_mm_t_body(x_ref, g_ref, sh_ref, sc_ref, wt_ref, o_ref, h_scr, *, tm):
    _norm_rows(x_ref, g_ref, sh_ref, sc_ref, h_scr, tm)
    acc = lax.dot_general(wt_ref[...], h_scr[...], (((1,), (1,)), ((), ())),
                          preferred_element_type=F32)
    o_ref[...] = acc.astype(o_ref.dtype)


def _norm_swiglu_body(x_ref, g_ref, sh_ref, sc_ref, wg_ref, wu_ref, o_ref, h_scr, *, tm):
    @pl.when(pl.program_id(1) == 0)
    def _():
        _norm_rows(x_ref, g_ref, sh_ref, sc_ref, h_scr, tm)
    h = h_scr[...]
    gate = jnp.dot(h, wg_ref[...], preferred_element_type=F32)
    up = jnp.dot(h, wu_ref[...], preferred_element_type=F32)
    o_ref[...] = (gate * jax.nn.sigmoid(gate) * up).astype(o_ref.dtype)


def _mod_specs(cfg, tm, sub):
    per = cfg.ch // tm
    shift = pl.BlockSpec((1, 1, D_MODEL), lambda i, *_: ((i // per) * 9 + sub * 3, 0, 0))
    scale = pl.BlockSpec((1, 1, D_MODEL), lambda i, *_: ((i // per) * 9 + sub * 3 + 1, 0, 0))
    return shift, scale


def _norm_matmul(cfg, x, g, mods, sub, w, col_scale, out_dtype, tm, tn, name):
    n = x.shape[0]
    nout = w.shape[1]
    shift, scale = _mod_specs(cfg, tm, sub)
    return pl.pallas_call(
        functools.partial(_norm_mm_body, tm=tm),
        grid=(n // tm, nout // tn),
        in_specs=[pl.BlockSpec((tm, D_MODEL), lambda i, j: (i, 0)),
                  pl.BlockSpec((1, D_MODEL), lambda i, j: (0, 0)),
                  shift, scale,
                  pl.BlockSpec((D_MODEL, tn), lambda i, j: (0, j)),
                  pl.BlockSpec((1, tn), lambda i, j: (0, j))],
        out_specs=pl.BlockSpec((tm, tn), lambda i, j: (i, j)),
        out_shape=jax.ShapeDtypeStruct((n, nout), out_dtype),
        scratch_shapes=[pltpu.VMEM((tm, D_MODEL), BF16)],
        compiler_params=_params(("arbitrary", "arbitrary")),
        name=name,
    )(x, g.reshape(1, D_MODEL), mods, mods, w, col_scale)


def _norm_matmul_t(cfg, x, g, mods, sub, wt, tm, name):
    n = x.shape[0]
    nout = wt.shape[0]
    shift, scale = _mod_specs(cfg, tm, sub)
    return pl.pallas_call(
        functools.partial(_norm_mm_t_body, tm=tm),
        grid=(n // tm,),
        in_specs=[pl.BlockSpec((tm, D_MODEL), lambda i: (i, 0)),
                  pl.BlockSpec((1, D_MODEL), lambda i: (0, 0)),
                  shift, scale,
                  pl.BlockSpec((nout, D_MODEL), lambda i: (0, 0))],
        out_specs=pl.BlockSpec((nout, tm), lambda i: (0, i)),
        out_shape=jax.ShapeDtypeStruct((nout, n), BF16),
        scratch_shapes=[pltpu.VMEM((tm, D_MODEL), BF16)],
        compiler_params=_params(("arbitrary",)),
        name=name,
    )(x, g.reshape(1, D_MODEL), mods, mods, wt)


def _norm_swiglu(cfg, x, g, mods, sub, w_in, tm, tn):
    n = x.shape[0]
    nj = D_FF // tn
    shift, scale = _mod_specs(cfg, tm, sub)
    return pl.pallas_call(
        functools.partial(_norm_swiglu_body, tm=tm),
        grid=(n // tm, nj),
        in_specs=[pl.BlockSpec((tm, D_MODEL), lambda i, j: (i, 0)),
                  pl.BlockSpec((1, D_MODEL), lambda i, j: (0, 0)),
                  shift, scale,
                  pl.BlockSpec((D_MODEL, tn), lambda i, j: (0, j)),
                  pl.BlockSpec((D_MODEL, tn), lambda i, j: (0, j + nj))],
        out_specs=pl.BlockSpec((tm, tn), lambda i, j: (i, j)),
        out_shape=jax.ShapeDtypeStruct((n, D_FF), BF16),
        scratch_shapes=[pltpu.VMEM((tm, D_MODEL), BF16)],
        compiler_params=_params(("arbitrary", "arbitrary")),
        name="ffn_in",
    )(x, g.reshape(1, D_MODEL), mods, mods, w_in, w_in)


def _resid_mm_body(*refs, pieces, coef):
    lhs = refs[:pieces]
    ws = refs[pieces:2 * pieces]
    x_ref, gate_ref, o_ref = refs[2 * pieces:]
    acc = jnp.dot(lhs[0][...], ws[0][...], preferred_element_type=F32)
    for p in range(1, pieces):
        acc = acc + jnp.dot(lhs[p][...], ws[p][...], preferred_element_type=F32)
    o_ref[...] = x_ref[...] + (coef * gate_ref[0]) * acc


def _resid_matmul(cfg, lhs_list, w, x, mods, sub, coef, tm, tn, name):
    n = x.shape[0]
    per = cfg.ch // tm
    widths = [a.shape[1] for a in lhs_list]
    offs = np.cumsum([0] + widths[:-1])
    in_specs = [pl.BlockSpec((tm, k), lambda i, j: (i, 0)) for k in widths]
    for k, off in zip(widths, offs):
        assert off % k == 0
        in_specs.append(pl.BlockSpec((k, tn), lambda i, j, b=int(off // k): (b, j)))
    in_specs += [pl.BlockSpec((tm, tn), lambda i, j: (i, j)),
                 pl.BlockSpec((1, 1, tn), lambda i, j: ((i // per) * 9 + sub * 3 + 2, 0, j))]
    return pl.pallas_call(
        functools.partial(_resid_mm_body, pieces=len(lhs_list), coef=coef),
        grid=(n // tm, D_MODEL // tn),
        in_specs=in_specs,
        out_specs=pl.BlockSpec((tm, tn), lambda i, j: (i, j)),
        out_shape=jax.ShapeDtypeStruct((n, D_MODEL), F32),
        compiler_params=_params(("arbitrary", "arbitrary")),
        name=name,
    )(*lhs_list, *([w] * len(lhs_list)), x, mods)


def _final_norm_body(x_ref, g_ref, o_ref):
    x = x_ref[...]
    inv = lax.rsqrt(jnp.mean(x * x, axis=-1, keepdims=True) + EPS)
    o_ref[...] = x * inv * g_ref[...]


def _final_norm(x, g, tm=512):
    n = x.shape[0]
    return pl.pallas_call(
        _final_norm_body,
        grid=(n // tm,),
        in_specs=[pl.BlockSpec((tm, D_MODEL), lambda i: (i, 0)),
                  pl.BlockSpec((1, D_MODEL), lambda i: (0, 0))],
        out_specs=pl.BlockSpec((tm, D_MODEL), lambda i: (i, 0)),
        out_shape=jax.ShapeDtypeStruct((n, D_MODEL), F32),
        compiler_params=_params(("arbitrary",)),
        name="final_norm",
    )(x, g.reshape(1, D_MODEL))


def _t5_bucket_np(rel):
    nb = REL_BUCKETS // 2
    max_exact = nb // 2
    rel = np.asarray(rel, np.int64)
    base = np.where(rel > 0, nb, 0)
    n = np.abs(rel)
    nf = np.maximum(n, 1).astype(np.float64)
    large = max_exact + (np.log(nf / max_exact) / math.log(REL_MAX_DIST / max_exact)
                         * (nb - max_exact)).astype(np.int64)
    large = np.minimum(large, nb - 1)
    return (base + np.where(n < max_exact, n, large)).astype(np.int32)


EXPAND_ROWS = 32


def _expand_body(tab_ref, idx_ref, o_ref, *, nb):
    t = pl.program_id(1)

    def chunk(r, carry):
        rows = pl.ds(pl.multiple_of(r * EXPAND_ROWS, EXPAND_ROWS), EXPAND_ROWS)
        idx = idx_ref[0, rows, :]
        pick = lambda b, acc: jnp.where(idx == b, tab_ref[t, b], acc)
        o_ref[0, 0, rows, :] = lax.fori_loop(0, nb, pick, jnp.zeros(idx.shape, F32))
        return carry
    lax.fori_loop(0, idx_ref.shape[1] // EXPAND_ROWS, chunk, 0)


def _expand(tab, idx):
    ntab, nb = tab.shape
    ni, r, c = idx.shape
    assert r % EXPAND_ROWS == 0 and idx.min() >= 0 and idx.max() < nb
    return pl.pallas_call(
        functools.partial(_expand_body, nb=nb),
        grid=(ni, ntab),
        in_specs=[pl.BlockSpec(memory_space=pltpu.SMEM),
                  pl.BlockSpec((1, r, c), lambda i, t: (i, 0, 0))],
        out_specs=pl.BlockSpec((1, 1, r, c), lambda i, t: (i, t, 0, 0)),
        out_shape=jax.ShapeDtypeStruct((ni, ntab, r, c), F32),
        compiler_params=_params(("arbitrary", "arbitrary")),
        name="expand_table",
    )(tab, jnp.asarray(idx.astype(np.int32)))


DIL_BQ_MAX = 256


def _banded_heads(heads, scores, finish):
    s_next = scores(0)
    for h in range(heads):
        s_cur = s_next
        if h + 1 < heads:
            s_next = scores(h + 1)
        finish(h, s_cur)


def _dil_block(cfg, dil):
    return min(DIL_BQ_MAX, cfg.ch // dil)


def _dil_tables(cfg, dil):
    bq = _dil_block(cfg, dil)
    lc = cfg.ch // dil
    nbp = lc // bq
    assert lc % bq == 0 and bq % A_HALF == 0
    per_chunk = cfg.ch // bq
    nblk = cfg.n // bq
    halos = bq // A_HALF
    prev = np.zeros(nblk, np.int32)
    nxt = np.zeros(nblk, np.int32)
    lo = np.zeros(nblk, np.int32)
    hi = np.zeros(nblk, np.int32)
    for i in range(nblk):
        c, blk = divmod(i, per_chunk)
        r, nn = divmod(blk, nbp)
        first_chunk, last_chunk = (0, cfg.pc - 1) if c < cfg.pc else (c, c)
        if nn > 0:
            prev[i], lo[i] = halos * i - 1, 0
        elif c > first_chunk:
            prev[i], lo[i] = ((c - 1) * cfg.ch + r * lc + lc - A_HALF) // A_HALF, 0
        else:
            prev[i], lo[i] = halos * i, A_HALF
        if nn < nbp - 1:
            nxt[i], hi[i] = halos * (i + 1), bq + 2 * A_HALF
        elif c < last_chunk:
            nxt[i], hi[i] = ((c + 1) * cfg.ch + r * lc) // A_HALF, bq + 2 * A_HALF
        else:
            nxt[i], hi[i] = halos * (i + 1) - 1, bq + A_HALF
    return prev, nxt, lo, hi


def _dil_body(prev_ref, nxt_ref, lo_ref, hi_ref, q_ref, kp_ref, km_ref, kn_ref,
              vp_ref, vm_ref, vn_ref, bias_ref, o_ref, lse_ref):
    i = pl.program_id(0)
    bq = q_ref.shape[0]
    nk = bq + 2 * A_HALF
    row = lax.broadcasted_iota(jnp.int32, (bq, nk), 0)
    col = lax.broadcasted_iota(jnp.int32, (bq, nk), 1)
    off = col - row
    valid = (off >= 0) & (off <= 2 * A_HALF) & (col >= lo_ref[i]) & (col < hi_ref[i])
    k = jnp.concatenate([kp_ref[...], km_ref[...], kn_ref[...]], axis=0)
    v = jnp.concatenate([vp_ref[...], vm_ref[...], vn_ref[...]], axis=0)

    def scores(h):
        cols = slice(h * HEAD_DIM, (h + 1) * HEAD_DIM)
        s = lax.dot_general(q_ref[:, cols], k[:, cols], (((1,), (1,)), ((), ())),
                            preferred_element_type=F32)
        return jnp.where(valid, s + bias_ref[h], NEG)

    def finish(h, s):
        cols = slice(h * HEAD_DIM, (h + 1) * HEAD_DIM)
        m = jnp.max(s, axis=-1, keepdims=True)
        e = jnp.exp(s - m)
        den = jnp.sum(e, axis=-1, keepdims=True)
        o = jnp.dot(e.astype(BF16), v[:, cols], preferred_element_type=F32)
        o_ref[:, cols] = (o / den).astype(o_ref.dtype)
        lse_ref[:, h:h + 1] = m + jnp.log(den)

    _banded_heads(A_HEADS, scores, finish)


def _dilated_branch(cfg, dil, q_arr, k_arr, v_arr, qcol, kcol, vcol, bias):
    n = cfg.n
    w = A_HEADS * HEAD_DIM
    bq = _dil_block(cfg, dil)
    prev, nxt, lo, hi = _dil_tables(cfg, dil)
    main = lambda cb: pl.BlockSpec((bq, w), lambda i, p, x, l, h: (i, cb))
    before = lambda cb: pl.BlockSpec((A_HALF, w), lambda i, p, x, l, h: (p[i], cb))
    after = lambda cb: pl.BlockSpec((A_HALF, w), lambda i, p, x, l, h: (x[i], cb))
    out_spec = pl.BlockSpec((bq, w), lambda i, p, x, l, h: (i, 0))
    lse_spec = pl.BlockSpec((bq, A_HEADS), lambda i, p, x, l, h: (i, 0))
    grid_spec = pltpu.PrefetchScalarGridSpec(
        num_scalar_prefetch=4,
        grid=(n // bq,),
        in_specs=[main(qcol), before(kcol), main(kcol), after(kcol),
                  before(vcol), main(vcol), after(vcol),
                  pl.BlockSpec((A_HEADS, bq, bq + 2 * A_HALF),
                               lambda i, p, x, l, h: (0, 0, 0))],
        out_specs=[out_spec, lse_spec],
    )
    return pl.pallas_call(
        _dil_body,
        grid_spec=grid_spec,
        out_shape=[jax.ShapeDtypeStruct((n, w), BF16), jax.ShapeDtypeStruct((n, A_HEADS), F32)],
        compiler_params=_params(("arbitrary",)),
        name=f"dilated_d{dil}",
    )(jnp.asarray(prev), jnp.asarray(nxt), jnp.asarray(lo), jnp.asarray(hi),
      q_arr, k_arr, k_arr, k_arr, v_arr, v_arr, v_arr, bias)


def _dil_bias(cfg, rel_bias, g, dil):
    bq = _dil_block(cfg, dil)
    off = np.arange(bq + 2 * A_HALF)[None, :] - np.arange(bq)[:, None] - A_HALF
    bucket = _t5_bucket_np(dil * np.clip(off, -A_HALF, A_HALF))
    return _expand(rel_bias[:, g * A_HEADS:(g + 1) * A_HEADS].T, bucket[None])[0]


def _merge_body(o1, l1, o2, l2, o3, l3, out_ref):
    a, b, c = l1[...], l2[...], l3[...]
    m = jnp.maximum(jnp.maximum(a, b), c)
    ea, eb, ec = jnp.exp(a - m), jnp.exp(b - m), jnp.exp(c - m)
    tot = ea + eb + ec
    wa, wb, wc = ea / tot, eb / tot, ec / tot
    for h in range(A_HEADS):
        cols = slice(h * HEAD_DIM, (h + 1) * HEAD_DIM)
        hh = slice(h, h + 1)
        out = (wa[:, hh] * o1[:, cols].astype(F32) + wb[:, hh] * o2[:, cols].astype(F32)
               + wc[:, hh] * o3[:, cols].astype(F32))
        out_ref[:, cols] = out.astype(out_ref.dtype)


def _merge_branches(outs, lses, tm=512):
    n, w = outs[0].shape
    spec = pl.BlockSpec((tm, w), lambda i: (i, 0))
    lspec = pl.BlockSpec((tm, A_HEADS), lambda i: (i, 0))
    args = [a for pair in zip(outs, lses) for a in pair]
    return pl.pallas_call(
        _merge_body,
        grid=(n // tm,),
        in_specs=[spec, lspec] * 3,
        out_specs=spec,
        out_shape=jax.ShapeDtypeStruct((n, w), BF16),
        compiler_params=_params(("arbitrary",)),
        name="dilated_merge",
    )(*args)


def _to_residue_major(cfg, a, dil):
    n, w = a.shape
    return a.reshape(n // cfg.ch, cfg.ch // dil, dil, w).transpose(0, 2, 1, 3).reshape(n, w)


def _from_residue_major(cfg, a, dil):
    n, w = a.shape
    return a.reshape(n // cfg.ch, dil, cfg.ch // dil, w).transpose(0, 2, 1, 3).reshape(n, w)


DILF_BQ = 256
DILF_GROUP = 4


def _dilf_flags(cfg):
    nchunks = cfg.pc + cfg.sb
    prev = np.array([1 if 0 < c < cfg.pc else 0 for c in range(nchunks)], np.int32)
    nxt = np.array([1 if c < cfg.pc - 1 else 0 for c in range(nchunks)], np.int32)
    return jnp.asarray(prev), jnp.asarray(nxt)


def _dilf_scores(q, k, bias, valid):
    s = lax.dot_general(q, k, (((1,), (1,)), ((), ())), preferred_element_type=F32)
    return jnp.where(valid, s + bias, NEG)


def _dilf_finish(s, v):
    m = jnp.max(s, axis=-1, keepdims=True)
    e = jnp.exp(s - m)
    den = jnp.sum(e, axis=-1, keepdims=True)
    o = jnp.dot(e.astype(BF16), v, preferred_element_type=F32)
    return o / den, m + jnp.log(den)


def _pipelined(tasks):
    s_next = tasks[0][0]()
    for t, (_, finish) in enumerate(tasks):
        s_cur = s_next
        if t + 1 < len(tasks):
            s_next = tasks[t + 1][0]()
        finish(s_cur)


def _dilf_valid(bq, lo, hi):
    nk = bq + 2 * A_HALF
    row = lax.broadcasted_iota(jnp.int32, (bq, nk), 0)
    col = lax.broadcasted_iota(jnp.int32, (bq, nk), 1)
    off = col - row
    return (off >= 0) & (off <= 2 * A_HALF) & (col >= lo) & (col < hi)


def _dilf_body(hp_ref, hn_ref,
               q1_ref, k1p_ref, k1_ref, k1n_ref, v1p_ref, v1_ref, v1n_ref,
               q2_ref, k2p_ref, k2_ref, k2n_ref, v2p_ref, v2_ref, v2n_ref,
               q3_ref, k3p_ref, k3_ref, k3n_ref, v3p_ref, v3_ref, v3n_ref,
               b1_ref, b2_ref, b3_ref, o_ref, oacc, lacc, *, ch):
    c = pl.program_id(0)
    lo0 = jnp.where(hp_ref[c] != 0, 0, A_HALF)
    hi_cut = jnp.where(hn_ref[c] != 0, 0, A_HALF)

    def block_valid(bq, first, last):
        nk = bq + 2 * A_HALF
        return _dilf_valid(bq, lo0 if first else 0, nk - hi_cut if last else nk)

    def keep(g, rows, o, lse):
        oacc[g, rows, :] = o
        lacc[g, rows, :] = jnp.broadcast_to(lse, o.shape)

    def task(g, q_rows, k, v, bias_ref, valid, out_rows):
        scores = lambda: _dilf_scores(q_rows(), k, bias_ref[0], valid)
        finish = lambda s: keep(g, out_rows, *_dilf_finish(s, v))
        return scores, finish

    bq = DILF_BQ
    tasks = []
    nblk = ch // bq
    k = jnp.concatenate([k1p_ref[...], k1_ref[...], k1n_ref[...]], axis=0)
    v = jnp.concatenate([v1p_ref[...], v1_ref[...], v1n_ref[...]], axis=0)
    for b in range(nblk):
        rows = slice(b * bq, (b + 1) * bq)
        keys = slice(b * bq, (b + 1) * bq + 2 * A_HALF)
        tasks.append(task(0, lambda rows=rows: q1_ref[rows, :], k[keys], v[keys], b1_ref,
                          block_valid(bq, b == 0, b == nblk - 1), rows))
    dil = A_CONFIGS[1][1]
    lc = ch // dil
    nblk = lc // bq
    for r in range(dil):
        sub = lambda ref, n: ref[pl.ds(r, n, stride=dil), :].astype(BF16)
        k = jnp.concatenate([sub(k2p_ref, A_HALF), sub(k2_ref, lc), sub(k2n_ref, A_HALF)], axis=0)
        v = jnp.concatenate([sub(v2p_ref, A_HALF), sub(v2_ref, lc), sub(v2n_ref, A_HALF)], axis=0)
        for b in range(nblk):
            rows = pl.ds(r + b * bq * dil, bq, stride=dil)
            keys = slice(b * bq, (b + 1) * bq + 2 * A_HALF)
            tasks.append(task(1, lambda rows=rows: q2_ref[rows, :].astype(BF16), k[keys], v[keys],
                              b2_ref, block_valid(bq, b == 0, b == nblk - 1), rows))
    _pipelined(tasks)

    dil3 = A_CONFIGS[2][1]
    lc3 = ch // dil3
    valid3 = block_valid(lc3, True, True)

    def residues(i, carry):
        group = []
        for u in range(DILF_GROUP):
            r = i * DILF_GROUP + u
            sub = lambda ref, n, r=r: ref[pl.ds(r, n, stride=dil3), :].astype(BF16)
            k = jnp.concatenate([sub(k3p_ref, A_HALF), sub(k3_ref, lc3), sub(k3n_ref, A_HALF)],
                                axis=0)
            v = jnp.concatenate([sub(v3p_ref, A_HALF), sub(v3_ref, lc3), sub(v3n_ref, A_HALF)],
                                axis=0)
            rows = pl.ds(r, lc3, stride=dil3)
            group.append(task(2, lambda rows=rows: q3_ref[rows, :].astype(BF16), k, v, b3_ref,
                              valid3, rows))
        _pipelined(group)
        return carry
    lax.fori_loop(0, dil3 // DILF_GROUP, residues, 0)

    def combine(i, carry):
        rows = pl.ds(pl.multiple_of(i * DILF_BQ, DILF_BQ), DILF_BQ)
        l0, l1, l2 = lacc[0, rows, :], lacc[1, rows, :], lacc[2, rows, :]
        m = jnp.maximum(jnp.maximum(l0, l1), l2)
        e0, e1, e2 = jnp.exp(l0 - m), jnp.exp(l1 - m), jnp.exp(l2 - m)
        tot = e0 + e1 + e2
        out = (e0 / tot) * oacc[0, rows, :] + (e1 / tot) * oacc[1, rows, :] \
            + (e2 / tot) * oacc[2, rows, :]
        o_ref[rows, :] = out.astype(o_ref.dtype)
        return carry
    lax.fori_loop(0, ch // DILF_BQ, combine, 0)


def _dilated_fused(cfg, p1, p23, biases):
    n = cfg.n
    ch = cfg.ch
    hp, hn = _dilf_flags(cfg)
    nh = A_HEADS
    assert ch // A_CONFIGS[2][1] == 2 * A_HALF and ch % DILF_BQ == 0

    def specs(dil, qb, kb, vb):
        halo = A_HALF * dil
        per = ch // halo
        last = n // halo - 1
        own = lambda cb: pl.BlockSpec((ch, HEAD_DIM), lambda c, h, p, x: (c, cb + h))
        before = lambda cb: pl.BlockSpec(
            (halo, HEAD_DIM), lambda c, h, p, x: (jnp.maximum(c * per - 1, 0), cb + h))
        after = lambda cb: pl.BlockSpec(
            (halo, HEAD_DIM), lambda c, h, p, x: (jnp.minimum((c + 1) * per, last), cb + h))
        return [own(qb), before(kb), own(kb), after(kb), before(vb), own(vb), after(vb)]

    bias_spec = lambda b: pl.BlockSpec((1,) + b.shape[1:], lambda c, h, p, x: (h, 0, 0))
    grid_spec = pltpu.PrefetchScalarGridSpec(
        num_scalar_prefetch=2,
        grid=(n // ch, nh),
        in_specs=(specs(1, 0, nh, 2 * nh) + specs(A_CONFIGS[1][1], 0, 2 * nh, 4 * nh)
                  + specs(A_CONFIGS[2][1], nh, 3 * nh, 5 * nh)
                  + [bias_spec(b) for b in biases]),
        out_specs=pl.BlockSpec((ch, HEAD_DIM), lambda c, h, p, x: (c, h)),
        scratch_shapes=[pltpu.VMEM((3, ch, HEAD_DIM), F32), pltpu.VMEM((3, ch, HEAD_DIM), F32)],
    )
    return pl.pallas_call(
        functools.partial(_dilf_body, ch=ch),
        grid_spec=grid_spec,
        out_shape=jax.ShapeDtypeStruct((n, nh * HEAD_DIM), BF16),
        compiler_params=_params(("arbitrary", "arbitrary")),
        name="dilated_fused",
    )(hp, hn, *([p1] * 7), *([p23] * 14), *biases)


def _dilf_bias(rel_bias, g, dil, bq):
    off = np.arange(bq + 2 * A_HALF)[None, :] - np.arange(bq)[:, None] - A_HALF
    bucket = _t5_bucket_np(dil * np.clip(off, -A_HALF, A_HALF))
    return _expand(rel_bias[:, g * A_HEADS:(g + 1) * A_HEADS].T, bucket[None])[0]


def _flash_worklist(cfg, tq, tk, rel_lo=None, rel_hi=None):
    qb, kb, tile, flags = [], [], [], []
    for start, length in cfg.seqs():
        assert length % tq == 0 and length % tk == 0 and start % tq == 0 and start % tk == 0
        nk = length // tk
        for qi in range(length // tq):
            for kj in range(nk):
                qb.append(start // tq + qi)
                kb.append(start // tk + kj)
                if rel_lo is not None:
                    assert (kj * tk) % tq == 0
                    d = (kj * tk - qi * tq) // tq
                    tile.append(min(max(d, rel_lo), rel_hi) - rel_lo)
                else:
                    tile.append(0)
                flags.append((1 if kj == 0 else 0) | (2 if kj == nk - 1 else 0))
    as_i32 = lambda a: jnp.asarray(np.asarray(a, np.int32))
    return as_i32(qb), as_i32(kb), as_i32(tile), as_i32(flags)


def _flash_init(flags, m_scr, acc_scr):
    @pl.when((flags & 1) != 0)
    def _():
        m_scr[...] = jnp.full(m_scr.shape, -jnp.inf, F32)
        acc_scr[...] = jnp.zeros(acc_scr.shape, F32)


def _flash_softmax(h, s, m_scr):
    m_prev = m_scr[h]
    m_new = jnp.maximum(m_prev, jnp.max(s, axis=0, keepdims=True))
    m_scr[h] = m_new
    return jnp.exp2(m_prev - m_new), jnp.exp2(s - m_new).astype(BF16)


def _flash_accumulate(h, alpha, p, vt, acc_scr):
    lhs = jnp.concatenate([vt, jnp.ones((ONES_ROWS, vt.shape[1]), BF16)], axis=0)
    acc_scr[h] = alpha * acc_scr[h] + jnp.dot(lhs, p, preferred_element_type=F32)


def _flash_heads(heads, scores, vt_rows, m_scr, acc_scr):
    s_next = scores(0)
    pending = None
    for h in range(heads):
        s_cur = s_next
        if h + 1 < heads:
            s_next = scores(h + 1)
        alpha, p = _flash_softmax(h, s_cur, m_scr)
        if pending is not None:
            _flash_accumulate(*pending, vt_rows(pending[0]), acc_scr)
        pending = (h, alpha, p)
    _flash_accumulate(*pending, vt_rows(pending[0]), acc_scr)


DIFF_TQ = 256
DIFF_TK = 512


def _diff_tile_range(max_len):
    rel = np.arange(-max_len + 1, max_len)
    b = _t5_bucket_np(rel)
    sat_pos = int(rel[b != b[-1]].max()) + 1
    sat_neg = int(rel[b != b[0]].min()) - 1
    hi = -(-(sat_pos + DIFF_TQ - 1) // DIFF_TQ)
    lo = (sat_neg - (DIFF_TK - 1)) // DIFF_TQ
    return lo, hi


def _diff_bias_tiles(rel_bias, lo, hi):
    d = np.arange(lo, hi + 1)[:, None, None] * DIFF_TQ
    rel = d + np.arange(DIFF_TK)[None, :, None] - np.arange(DIFF_TQ)[None, None, :]
    return _expand(rel_bias[:, 3 * A_HEADS:].T * LOG2E, _t5_bucket_np(rel))


def _diff_body(qb_ref, kb_ref, tile_ref, fl_ref, q_ref, k_ref, vt_ref, bias_ref, lam_ref, g_ref,
               o_ref, m_scr, acc_scr, *, lam_init):
    flags = fl_ref[pl.program_id(0)]
    tq = DIFF_TQ
    _flash_init(flags, m_scr, acc_scr)

    lane = lax.broadcasted_iota(jnp.int32, (tq, HEAD_DIM), 1)

    def scores(h):
        cols = slice(h * HEAD_DIM, (h + 1) * HEAD_DIM)
        qh = q_ref[:, cols]
        zero = jnp.zeros_like(qh)
        q2 = jnp.concatenate([jnp.where(lane < B_QK_DIM, qh, zero),
                              jnp.where(lane >= B_QK_DIM, qh, zero)], axis=0)
        s = lax.dot_general(k_ref[:, cols], q2, (((1,), (1,)), ((), ())),
                            preferred_element_type=F32)
        b = bias_ref[0, h]
        return s + jnp.concatenate([b, b], axis=1)

    vt_rows = lambda h: vt_ref[h * HEAD_DIM:(h + 1) * HEAD_DIM, :]
    _flash_heads(B_HEADS, scores, vt_rows, m_scr, acc_scr)

    @pl.when((flags & 2) != 0)
    def _():
        lf = lam_ref[...]
        lam = (jnp.exp(jnp.sum(lf[0:1] * lf[1:2], axis=-1, keepdims=True))
               - jnp.exp(jnp.sum(lf[2:3] * lf[3:4], axis=-1, keepdims=True)) + lam_init)
        for h in range(B_HEADS):
            cols = slice(h * HEAD_DIM, (h + 1) * HEAD_DIM)
            acc = acc_scr[h]
            att = acc[:HEAD_DIM] / acc[HEAD_DIM:HEAD_DIM + 1]
            o = att[:, :tq] - lam * att[:, tq:]
            inv = lax.rsqrt(jnp.mean(o * o, axis=0, keepdims=True) + EPS)
            o_ref[:, cols] = (((o * inv).T * g_ref[...]) * (1.0 - lam_init)).astype(o_ref.dtype)


def _diff_attention(cfg, proj, qcol, kcol, vt, bias_tiles, tile_lo, tile_hi, lam, subln_g,
                    lam_init):
    n = cfg.n
    tq, tk = DIFF_TQ, DIFF_TK
    qb, kb, tile, flags = _flash_worklist(cfg, tq, tk, tile_lo, tile_hi)
    grid_spec = pltpu.PrefetchScalarGridSpec(
        num_scalar_prefetch=4,
        grid=(qb.shape[0],),
        in_specs=[pl.BlockSpec((tq, B_W), lambda s, q, k, t, f: (q[s], qcol)),
                  pl.BlockSpec((tk, B_W), lambda s, q, k, t, f: (k[s], kcol)),
                  pl.BlockSpec((B_W, tk), lambda s, q, k, t, f: (0, k[s])),
                  pl.BlockSpec((1, B_HEADS, tk, tq), lambda s, q, k, t, f: (t[s], 0, 0, 0)),
                  pl.BlockSpec((4, B_QK_DIM), lambda s, q, k, t, f: (0, 0)),
                  pl.BlockSpec((1, HEAD_DIM), lambda s, q, k, t, f: (0, 0))],
        out_specs=pl.BlockSpec((tq, B_W), lambda s, q, k, t, f: (q[s], 0)),
        scratch_shapes=[pltpu.VMEM((B_HEADS, 1, 2 * tq), F32),
                        pltpu.VMEM((B_HEADS, HEAD_DIM + ONES_ROWS, 2 * tq), F32)],
    )
    return pl.pallas_call(
        functools.partial(_diff_body, lam_init=lam_init),
        grid_spec=grid_spec,
        out_shape=jax.ShapeDtypeStruct((n, B_W), BF16),
        compiler_params=_params(("arbitrary",)),
        name="diff_attention",
    )(qb, kb, tile, flags, proj, proj, vt, bias_tiles, lam, subln_g.reshape(1, HEAD_DIM))


MLA_TM = 512
MLA_TQ = 512
MLA_TK = 512


def _rope_tables(max_len):
    inv = ROPE_THETA ** (-jnp.arange(0, C_ROPE, 2, dtype=F32) / C_ROPE)
    ang = jnp.arange(max_len, dtype=F32)[:, None] * inv[None, :]
    cos, sin = jnp.cos(ang), jnp.sin(ang)
    zero = jnp.zeros((max_len, HEAD_DIM - C_ROPE), F32)
    return (jnp.concatenate([cos, cos, zero], axis=1),
            jnp.concatenate([-sin, sin, zero], axis=1))


def _mla_prep_body(p_ref, gq_ref, gkv_ref, wqa_ref, wqb_ref, wk_ref, wvt_ref, cos_ref, sin_ref,
                   q_ref, k_ref, vt_ref, *, scale):
    def normed(x, g):
        inv = lax.rsqrt(jnp.mean(x * x, axis=-1, keepdims=True) + EPS)
        return (x * inv * g).astype(BF16)

    cq = normed(p_ref[:, 0:C_LORA], gq_ref[...])
    ckv = normed(p_ref[:, C_LORA:2 * C_LORA], gkv_ref[...])
    cos = cos_ref[...]
    sin = sin_ref[...]
    qa = jnp.dot(cq, wqa_ref[...], preferred_element_type=F32)
    qb = jnp.dot(cq, wqb_ref[...], preferred_element_type=F32)
    kn = jnp.dot(ckv, wk_ref[...], preferred_element_type=F32)
    vt_ref[...] = lax.dot_general(wvt_ref[...], ckv, (((1,), (1,)), ((), ())),
                                  preferred_element_type=F32).astype(BF16)
    kr = p_ref[:, 2 * C_LORA:2 * C_LORA + HEAD_DIM]
    kr_sw = p_ref[:, 2 * C_LORA + HEAD_DIM:2 * C_LORA + 2 * HEAD_DIM]
    k_rope = (kr * cos + kr_sw * sin).astype(BF16)
    for h in range(C_HEADS):
        a0 = h * C_QK_PAD
        a1 = a0 + HEAD_DIM
        a2 = a0 + C_QK_PAD
        b = slice(h * HEAD_DIM, (h + 1) * HEAD_DIM)
        q_ref[:, a0:a1] = (qa[:, a0:a1] * scale).astype(BF16)
        q_ref[:, a1:a2] = ((qa[:, a1:a2] * cos + qb[:, b] * sin) * scale).astype(BF16)
        k_ref[:, a0:a1] = kn[:, b].astype(BF16)
        k_ref[:, a1:a2] = k_rope


def _mla_prep(cfg, proj_c, gq, gkv, wqa, wqb, wk, wvt, cos_tab, sin_tab):
    n = cfg.n
    tm = MLA_TM
    pblocks = cfg.pc * cfg.ch // tm
    per = cfg.ch // tm
    pos = lambda i: (jnp.where(i < pblocks, i, (i - pblocks) % per), 0)
    full = lambda shape: pl.BlockSpec(shape, lambda i: (0, 0))
    qk_w = C_HEADS * C_QK_PAD
    v_w = C_HEADS * HEAD_DIM
    return pl.pallas_call(
        functools.partial(_mla_prep_body, scale=(C_NOPE + C_ROPE) ** -0.5 * LOG2E),
        grid=(n // tm,),
        in_specs=[pl.BlockSpec((tm, proj_c.shape[1]), lambda i: (i, 0)),
                  full((1, C_LORA)), full((1, C_LORA)),
                  full(wqa.shape), full(wqb.shape), full(wk.shape), full(wvt.shape),
                  pl.BlockSpec((tm, HEAD_DIM), pos), pl.BlockSpec((tm, HEAD_DIM), pos)],
        out_specs=[pl.BlockSpec((tm, qk_w), lambda i: (i, 0)),
                   pl.BlockSpec((tm, qk_w), lambda i: (i, 0)),
                   pl.BlockSpec((v_w, tm), lambda i: (0, i))],
        out_shape=[jax.ShapeDtypeStruct((n, qk_w), BF16),
                   jax.ShapeDtypeStruct((n, qk_w), BF16),
                   jax.ShapeDtypeStruct((v_w, n), BF16)],
        compiler_params=_params(("arbitrary",)),
        name="mla_prep",
    )(proj_c, gq.reshape(1, C_LORA), gkv.reshape(1, C_LORA), wqa, wqb, wk, wvt, cos_tab, sin_tab)


def _mla_body(qb_ref, kb_ref, tile_ref, fl_ref, q_ref, k_ref, vt_ref, o_ref, m_scr, acc_scr):
    flags = fl_ref[pl.program_id(0)]
    _flash_init(flags, m_scr, acc_scr)

    def scores(h):
        qk = slice(h * C_QK_PAD, (h + 1) * C_QK_PAD)
        return lax.dot_general(k_ref[:, qk], q_ref[:, qk], (((1,), (1,)), ((), ())),
                               preferred_element_type=F32)

    vt_rows = lambda h: vt_ref[h * HEAD_DIM:(h + 1) * HEAD_DIM, :]
    _flash_heads(C_HEADS, scores, vt_rows, m_scr, acc_scr)

    @pl.when((flags & 2) != 0)
    def _():
        for h in range(C_HEADS):
            vc = slice(h * HEAD_DIM, (h + 1) * HEAD_DIM)
            acc = acc_scr[h]
            o_ref[:, vc] = (acc[:HEAD_DIM] / acc[HEAD_DIM:HEAD_DIM + 1]).T.astype(o_ref.dtype)


def _mla_attention(cfg, q, k, vt):
    n = cfg.n
    tq, tk = MLA_TQ, MLA_TK
    qb, kb, tile, flags = _flash_worklist(cfg, tq, tk)
    qk_w = C_HEADS * C_QK_PAD
    v_w = C_HEADS * HEAD_DIM
    grid_spec = pltpu.PrefetchScalarGridSpec(
        num_scalar_prefetch=4,
        grid=(qb.shape[0],),
        in_specs=[pl.BlockSpec((tq, qk_w), lambda s, q_, k_, t, f: (q_[s], 0)),
                  pl.BlockSpec((tk, qk_w), lambda s, q_, k_, t, f: (k_[s], 0)),
                  pl.BlockSpec((v_w, tk), lambda s, q_, k_, t, f: (0, k_[s]))],
        out_specs=pl.BlockSpec((tq, v_w), lambda s, q_, k_, t, f: (q_[s], 0)),
        scratch_shapes=[pltpu.VMEM((C_HEADS, 1, tq), F32),
                        pltpu.VMEM((C_HEADS, HEAD_DIM + ONES_ROWS, tq), F32)],
    )
    return pl.pallas_call(
        _mla_body,
        grid_spec=grid_spec,
        out_shape=jax.ShapeDtypeStruct((n, v_w), BF16),
        compiler_params=_params(("arbitrary",)),
        name="mla_attention",
    )(qb, kb, tile, flags, q, k, vt)


def _na_tables(cfg):
    prev, nxt, var = [], [], []
    for start, length in cfg.seqs():
        nb = length // NA_BLOCK
        assert length % NA_BLOCK == 0 and start % NA_BLOCK == 0 and nb >= 3
        b0 = start // NA_BLOCK
        for r in range(nb):
            prev.append(b0 + max(r - 1, 0))
            nxt.append(b0 + min(r + 1, nb - 1))
            var.append(0 if r == 0 else (2 if r == nb - 1 else 1))
    as_i32 = lambda a: jnp.asarray(np.asarray(a, np.int32))
    return as_i32(prev), as_i32(nxt), as_i32(var)


def _na_window_start(a, variant):
    centred = a - NA_ROWS // 2
    return (max(centred, 0), centred, min(centred, 0))[variant]


def _na_bias_tables(rpb):
    ncol = 2 * NA_COLS - 1
    c = np.arange(GRID_W)[:, None, None]
    b = np.arange(NA_ROWS)[None, :, None]
    kc = np.arange(GRID_W)[None, None, :]
    cstart = np.clip(c - NA_COLS // 2, 0, GRID_W - NA_COLS)
    col_ok = (kc >= cstart) & (kc < cstart + NA_COLS)
    dc = np.clip(kc - c, -(NA_COLS - 1), NA_COLS - 1) + (NA_COLS - 1)
    masked = NA_ROWS * ncol
    idx = np.where(col_ok, b * ncol + dc, masked).reshape(1, GRID_W, NA_BLOCK)
    rows = np.arange(NA_ROWS)[:, None] + np.arange(NA_ROWS)[None, :]
    tab = rpb[:, rows, :].reshape(D_HEADS * NA_ROWS, NA_ROWS * ncol)
    tab = jnp.concatenate([tab, jnp.full((tab.shape[0], 1), NEG, F32)], axis=1)
    slabs = _expand(tab, idx)[0].reshape(D_HEADS, NA_ROWS, GRID_W, NA_BLOCK)
    variants = []
    for variant in range(3):
        row_blocks = []
        for a in range(NA_ROWS):
            start = _na_window_start(a, variant)
            left = (NA_ROWS + start) * GRID_W
            row_blocks.append(jnp.pad(slabs[:, start - a + NA_ROWS - 1],
                                      ((0, 0), (0, 0), (left, 2 * NA_BLOCK - left)),
                                      constant_values=NEG))
        variants.append(jnp.concatenate(row_blocks, axis=1))
    return jnp.stack(variants, axis=0)


def _na_body(prev_ref, nxt_ref, var_ref, q_ref, kp_ref, km_ref, kn_ref, vp_ref, vm_ref, vn_ref,
             tab_ref, o_ref):
    k = jnp.concatenate([kp_ref[...], km_ref[...], kn_ref[...]], axis=0)
    v = jnp.concatenate([vp_ref[...], vm_ref[...], vn_ref[...]], axis=0)
    def scores(h):
        cols = slice(h * HEAD_DIM, (h + 1) * HEAD_DIM)
        s = lax.dot_general(q_ref[:, cols], k[:, cols], (((1,), (1,)), ((), ())),
                            preferred_element_type=F32)
        return s + tab_ref[0, h]

    def finish(h, s):
        cols = slice(h * HEAD_DIM, (h + 1) * HEAD_DIM)
        m = jnp.max(s, axis=-1, keepdims=True)
        e = jnp.exp(s - m)
        den = jnp.sum(e, axis=-1, keepdims=True)
        o = jnp.dot(e.astype(BF16), v[:, cols], preferred_element_type=F32)
        o_ref[:, cols] = (o / den).astype(o_ref.dtype)

    _banded_heads(D_HEADS, scores, finish)


def _na_attention(cfg, qkv, tabs):
    n = cfg.n
    w = D_HEADS * HEAD_DIM
    prev, nxt, var = _na_tables(cfg)
    own = lambda cb: pl.BlockSpec((NA_BLOCK, w), lambda i, p, x, t: (i, cb))
    before = lambda cb: pl.BlockSpec((NA_BLOCK, w), lambda i, p, x, t: (p[i], cb))
    after = lambda cb: pl.BlockSpec((NA_BLOCK, w), lambda i, p, x, t: (x[i], cb))
    grid_spec = pltpu.PrefetchScalarGridSpec(
        num_scalar_prefetch=3,
        grid=(n // NA_BLOCK,),
        in_specs=[own(0), before(1), own(1), after(1), before(2), own(2), after(2),
                  pl.BlockSpec((1, D_HEADS, NA_BLOCK, 3 * NA_BLOCK),
                               lambda i, p, x, t: (t[i], 0, 0, 0))],
        out_specs=pl.BlockSpec((NA_BLOCK, w), lambda i, p, x, t: (i, 0)),
    )
    return pl.pallas_call(
        _na_body,
        grid_spec=grid_spec,
        out_shape=jax.ShapeDtypeStruct((n, w), BF16),
        compiler_params=_params(("arbitrary",)),
        name="na_attention",
    )(prev, nxt, var, qkv, qkv, qkv, qkv, qkv, qkv, qkv, tabs)


EVEN_BLOCKS_ROWMAJOR = (0, 3, 6, 9, 10)
EVEN_BLOCKS_STRIDED = (1, 2, 4, 5, 7, 8)
EVEN_BLOCK_VT = 11


def _even_weights(w_in):
    blk = lambda b: w_in[:, b * B_W:(b + 1) * B_W]
    w_a = jnp.concatenate([blk(b) for b in EVEN_BLOCKS_ROWMAJOR], axis=1)
    w_b = jnp.concatenate([blk(b) for b in EVEN_BLOCKS_STRIDED], axis=1)
    return w_a, w_b, blk(EVEN_BLOCK_VT).T


def _even_col_scales():
    cs_a = np.ones((1, len(EVEN_BLOCKS_ROWMAJOR) * B_W), np.float32)
    cs_a[0, :B_W] = HEAD_DIM ** -0.5
    cs_a[0, 3 * B_W:4 * B_W] = B_QK_DIM ** -0.5 * LOG2E
    cs_b = np.ones((1, len(EVEN_BLOCKS_STRIDED) * B_W), np.float32)
    cs_b[0, :2 * B_W] = HEAD_DIM ** -0.5
    return jnp.asarray(cs_a), jnp.asarray(cs_b)


def _even_tables(cfg, rel_bias):
    dil = [_dilf_bias(rel_bias, gi, d, min(DILF_BQ, cfg.ch // d))
           for gi, (_, d) in enumerate(A_CONFIGS)]
    tile_lo, tile_hi = _diff_tile_range(cfg.pc * cfg.ch)
    return dil, _diff_bias_tiles(rel_bias, tile_lo, tile_hi), tile_lo, tile_hi


def _even_mixer(cfg, x, g, mods, weights, w_out, lam, subln_g, tables, lam_init, tm):
    w_a, w_b, wvt = weights
    dil_bias, diff_tiles, tile_lo, tile_hi = tables
    cs_a, cs_b = _even_col_scales()
    p1 = _norm_matmul(cfg, x, g, mods, 1, w_a, cs_a, BF16, tm, 1024, "even_in_a")
    p23 = _norm_matmul(cfg, x, g, mods, 1, w_b, cs_b, F32, tm, 1024, "even_in_b")
    vt = _norm_matmul_t(cfg, x, g, mods, 1, wvt, tm, "even_in_vt")
    o_a = _dilated_fused(cfg, p1, p23, dil_bias)
    o_b = _diff_attention(cfg, p1, 3, 4, vt, diff_tiles, tile_lo, tile_hi, lam, subln_g, lam_init)
    return _resid_matmul(cfg, [o_a, o_b], w_out, x, mods, 1, 1.0, tm, 512, "even_out")


def _odd_weights(w_in, w_q_up, w_kv_up):
    half = C_ROPE // 2
    swap = np.concatenate([np.arange(half, C_ROPE), np.arange(half)])
    o2 = 2 * C_LORA
    zpad = jnp.zeros((D_MODEL, HEAD_DIM - C_ROPE), w_in.dtype)
    kr = w_in[:, o2:o2 + C_ROPE]
    w_c = jnp.concatenate([w_in[:, :o2], kr, zpad, kr[:, swap], zpad], axis=1)
    w_d = w_in[:, o2 + C_ROPE:]
    q3 = w_q_up.reshape(C_LORA, C_HEADS, C_NOPE + C_ROPE)
    zq = jnp.zeros((C_LORA, C_HEADS, C_QK_PAD - C_NOPE - C_ROPE), w_q_up.dtype)
    wqa = jnp.concatenate([q3, zq], axis=2).reshape(C_LORA, C_HEADS * C_QK_PAD)
    wqb = jnp.concatenate([q3[:, :, C_NOPE:][:, :, swap], zq], axis=2).reshape(
        C_LORA, C_HEADS * HEAD_DIM)
    kv3 = w_kv_up.reshape(C_LORA, C_HEADS, 2 * HEAD_DIM)
    wk = kv3[:, :, :C_NOPE].reshape(C_LORA, C_HEADS * HEAD_DIM)
    wvt = kv3[:, :, C_NOPE:].reshape(C_LORA, C_HEADS * HEAD_DIM).T
    return w_c, w_d, wqa, wqb, wk, wvt


def _odd_mixer(cfg, x, g, mods, weights, gq, gkv, rpb, w_out, rope, tm):
    w_c, w_d, wqa, wqb, wk, wvt = weights
    ones_c = jnp.ones((1, w_c.shape[1]), F32)
    proj_c = _norm_matmul(cfg, x, g, mods, 1, w_c, ones_c, F32, tm, w_c.shape[1], "odd_in_latent")
    cs = np.ones((1, w_d.shape[1]), np.float32)
    cs[0, :D_HEADS * HEAD_DIM] = HEAD_DIM ** -0.5
    qkv_d = _norm_matmul(cfg, x, g, mods, 1, w_d, jnp.asarray(cs), BF16, tm, w_d.shape[1],
                         "odd_in_na")
    q, k, vt = _mla_prep(cfg, proj_c, gq, gkv, wqa, wqb, wk, wvt, *rope)
    o_c = _mla_attention(cfg, q, k, vt)
    o_d = _na_attention(cfg, qkv_d, _na_bias_tables(rpb))
    return _resid_matmul(cfg, [o_c, o_d], w_out, x, mods, 1, 1.0, tm, 512, "odd_out")


def _trunk(cfg, x, c_pad, ada_w, ada_b, norm_g, ffn_w_in, ffn_w_out, rel_bias, ev_w_in, ev_w_out,
           diff_lambda, diff_subln_g, od_w_in, mla_q_norm_g, mla_kv_norm_g, mla_w_q_up,
           mla_w_kv_up, na_rpb, od_w_out, final_norm_g):
    depth = ada_w.shape[0]
    tm = 1024
    nseq = 1 + cfg.sb
    mod_all = _modulation(c_pad, ada_w, ada_b)[:, :nseq]
    chunk_seq = np.array([0] * cfg.pc + list(range(1, nseq)))
    mod_all = mod_all[:, chunk_seq].reshape(depth, (cfg.pc + cfg.sb) * 9, 1, D_MODEL)
    rope = _rope_tables(cfg.pc * cfg.ch)
    even_tables = _even_tables(cfg, rel_bias)
    for i in range(depth):
        mods = mod_all[i]
        j = i // 2
        hid = _norm_swiglu(cfg, x, norm_g[i, 0], mods, 0, ffn_w_in[i, 0].astype(BF16), tm, 512)
        x = _resid_matmul(cfg, [hid], ffn_w_out[i, 0].astype(BF16), x, mods, 0, 0.5, tm, 512,
                          "ffn_out")
        if i % 2 == 0:
            x = _even_mixer(cfg, x, norm_g[i, 1], mods, _even_weights(ev_w_in[j].astype(BF16)),
                            ev_w_out[j].astype(BF16), diff_lambda[j], diff_subln_g[j], even_tables,
                            0.8 - 0.6 * math.exp(-0.3 * i), tm)
        else:
            weights = tuple(a.astype(BF16) for a in
                            _odd_weights(od_w_in[j], mla_w_q_up[j], mla_w_kv_up[j]))
            x = _odd_mixer(cfg, x, norm_g[i, 1], mods, weights, mla_q_norm_g[j], mla_kv_norm_g[j],
                           na_rpb[j], od_w_out[j].astype(BF16), rope, tm)
        hid = _norm_swiglu(cfg, x, norm_g[i, 2], mods, 2, ffn_w_in[i, 1].astype(BF16), tm, 512)
        x = _resid_matmul(cfg, [hid], ffn_w_out[i, 1].astype(BF16), x, mods, 2, 0.5, tm, 512,
                          "ffn_out")
    return _final_norm(x, final_norm_g)


def kernel(x_prompt, x_sample, c_prompt, c_sample, ada_w, ada_b, norm_g, ffn_w_in, ffn_w_out, rel_bias, ev_w_in, ev_w_out, diff_lambda, diff_subln_g, od_w_in, mla_q_norm_g, mla_kv_norm_g, mla_w_q_up, mla_w_kv_up, na_rpb, od_w_out, final_norm_g):
    pb, pt, _ = x_prompt.shape
    sb, st, _ = x_sample.shape
    assert pb == 1 and pt % st == 0
    cfg = Cfg(ch=st, pc=pt // st, sb=sb)
    x = jnp.concatenate([x_prompt.reshape(-1, D_MODEL), x_sample.reshape(-1, D_MODEL)], axis=0)
    c = jnp.concatenate([c_prompt, c_sample], axis=0)
    c_pad = jnp.pad(c, ((0, -c.shape[0] % 8), (0, 0)))
    y = _trunk(cfg, x, c_pad, ada_w, ada_b, norm_g, ffn_w_in, ffn_w_out, rel_bias, ev_w_in,
               ev_w_out, diff_lambda, diff_subln_g, od_w_in, mla_q_norm_g, mla_kv_norm_g,
               mla_w_q_up, mla_w_kv_up, na_rpb, od_w_out, final_norm_g)
    return (y[:pt].reshape(x_prompt.shape), y[pt:].reshape(x_sample.shape))
```

```python
import functools
import math
from typing import NamedTuple

import numpy as np
import jax
import jax.numpy as jnp
from jax import lax
from jax.experimental import pallas as pl
from jax.experimental.pallas import tpu as pltpu

F32 = jnp.float32
BF16 = jnp.bfloat16

D_MODEL = 2048
D_FF = 5632
HEAD_DIM = 128
A_HEADS = 8
A_CONFIGS = ((128, 1), (512, 4), (2048, 16))
A_HALF = 64
A_IN = 3 * 3 * A_HEADS * HEAD_DIM
B_HEADS = 8
B_QK_DIM = 64
B_W = B_HEADS * 2 * B_QK_DIM
EVEN_IN = A_IN + 3 * B_W
C_HEADS = 12
C_LORA = 512
C_NOPE = 128
C_ROPE = 64
C_QK_PAD = 256
ROPE_THETA = 10000.0
D_HEADS = 4
GRID_W = 64
NA_ROWS = 8
NA_COLS = 16
NA_BLOCK = NA_ROWS * GRID_W
REL_BUCKETS = 32
REL_MAX_DIST = 1024
EPS = 1e-6
NEG = -1e30
LOG2E = math.log2(math.e)
ONES_ROWS = 16

V7X_VMEM_BYTES = 64 * 1024 * 1024
VMEM_LIMIT = V7X_VMEM_BYTES - 8 * 1024 * 1024


class Cfg(NamedTuple):
    ch: int
    pc: int
    sb: int

    @property
    def n(self):
        return self.ch * (self.pc + self.sb)

    def seqs(self):
        out = [(0, self.pc * self.ch)]
        out += [((self.pc + b) * self.ch, self.ch) for b in range(self.sb)]
        return out


def _params(sem):
    return pltpu.CompilerParams(dimension_semantics=sem, vmem_limit_bytes=VMEM_LIMIT)


def _mod_body(c_ref, w_ref, b_ref, o_ref):
    c = c_ref[...]
    act = (c * jax.nn.sigmoid(c)).astype(BF16)
    o_ref[0] = jnp.dot(act, w_ref[0].astype(BF16), preferred_element_type=F32) + b_ref[0]


def _modulation(c_pad, ada_w, ada_b):
    depth, _, nout = ada_w.shape
    r = c_pad.shape[0]
    tn = 1024
    return pl.pallas_call(
        _mod_body,
        grid=(depth, nout // tn),
        in_specs=[pl.BlockSpec((r, D_MODEL), lambda l, j: (0, 0)),
                  pl.BlockSpec((1, D_MODEL, tn), lambda l, j: (l, 0, j)),
                  pl.BlockSpec((1, 1, tn), lambda l, j: (l, 0, j))],
        out_specs=pl.BlockSpec((1, r, tn), lambda l, j: (l, 0, j)),
        out_shape=jax.ShapeDtypeStruct((depth, r, nout), F32),
        compiler_params=_params(("arbitrary", "arbitrary")),
        name="modulation",
    )(c_pad, ada_w, ada_b.reshape(depth, 1, nout))


NORM_ROWS = 32


def _norm_rows(x_ref, g_ref, sh_ref, sc_ref, h_scr, tm):
    gain = g_ref[...] * (1.0 + sc_ref[0])
    shift = sh_ref[0]

    def body(r, carry):
        rows = pl.ds(pl.multiple_of(r * NORM_ROWS, NORM_ROWS), NORM_ROWS)
        x = x_ref[rows, :]
        inv = lax.rsqrt(jnp.mean(x * x, axis=-1, keepdims=True) + EPS)
        h_scr[rows, :] = ((x * inv) * gain + shift).astype(BF16)
        return carry
    lax.fori_loop(0, tm // NORM_ROWS, body, 0, unroll=4)


def _norm_mm_body(x_ref, g_ref, sh_ref, sc_ref, w_ref, cs_ref, o_ref, h_scr, *, tm):
    @pl.when(pl.program_id(1) == 0)
    def _():
        _norm_rows(x_ref, g_ref, sh_ref, sc_ref, h_scr, tm)
    acc = jnp.dot(h_scr[...], w_ref[...], preferred_element_type=F32)
    o_ref[...] = (acc * cs_ref[...]).astype(o_ref.dtype)


def _norm_mm_t_body(x_ref, g_ref, sh_ref, sc_ref, wt_ref, o_ref, h_scr, *, tm):
    _norm_rows(x_ref, g_ref, sh_ref, sc_ref, h_scr, tm)
    acc = lax.dot_general(wt_ref[...], h_scr[...], (((1,), (1,)), ((), ())),
                          preferred_element_type=F32)
    o_ref[...] = acc.astype(o_ref.dtype)


def _norm_swiglu_body(x_ref, g_ref, sh_ref, sc_ref, w_ref, o_ref, h_scr, *, tm):
    @pl.when(pl.program_id(1) == 0)
    def _():
        _norm_rows(x_ref, g_ref, sh_ref, sc_ref, h_scr, tm)
    tn = o_ref.shape[1]
    acc = jnp.dot(h_scr[...], w_ref[...], preferred_element_type=F32)
    gate, up = acc[:, :tn], acc[:, tn:]
    o_ref[...] = (gate * jax.nn.sigmoid(gate) * up).astype(o_ref.dtype)


def _interleave_gate_up(w_in, tn):
    d = w_in.shape[0]
    return w_in.reshape(d, 2, D_FF // tn, tn).transpose(0, 2, 1, 3).reshape(d, 2 * D_FF)


def _mod_specs(cfg, tm, sub):
    per = cfg.ch // tm
    shift = pl.BlockSpec((1, 1, D_MODEL), lambda i, *_: ((i // per) * 9 + sub * 3, 0, 0))
    scale = pl.BlockSpec((1, 1, D_MODEL), lambda i, *_: ((i // per) * 9 + sub * 3 + 1, 0, 0))
    return shift, scale


def _norm_matmul(cfg, x, g, mods, sub, w, col_scale, out_dtype, tm, tn, name):
    n = x.shape[0]
    nout = w.shape[1]
    shift, scale = _mod_specs(cfg, tm, sub)
    return pl.pallas_call(
        functools.partial(_norm_mm_body, tm=tm),
        grid=(n // tm, nout // tn),
        in_specs=[pl.BlockSpec((tm, D_MODEL), lambda i, j: (i, 0)),
                  pl.BlockSpec((1, D_MODEL), lambda i, j: (0, 0)),
                  shift, scale,
                  pl.BlockSpec((D_MODEL, tn), lambda i, j: (0, j)),
                  pl.BlockSpec((1, tn), lambda i, j: (0, j))],
        out_specs=pl.BlockSpec((tm, tn), lambda i, j: (i, j)),
        out_shape=jax.ShapeDtypeStruct((n, nout), out_dtype),
        scratch_shapes=[pltpu.VMEM((tm, D_MODEL), BF16)],
        compiler_params=_params(("arbitrary", "arbitrary")),
        name=name,
    )(x, g.reshape(1, D_MODEL), mods, mods, w, col_scale)


def _norm_matmul_t(cfg, x, g, mods, sub, wt, tm, name):
    n = x.shape[0]
    nout = wt.shape[0]
    shift, scale = _mod_specs(cfg, tm, sub)
    return pl.pallas_call(
        functools.partial(_norm_mm_t_body, tm=tm),
        grid=(n // tm,),
        in_specs=[pl.BlockSpec((tm, D_MODEL), lambda i: (i, 0)),
                  pl.BlockSpec((1, D_MODEL), lambda i: (0, 0)),
                  shift, scale,
                  pl.BlockSpec((nout, D_MODEL), lambda i: (0, 0))],
        out_specs=pl.BlockSpec((nout, tm), lambda i: (0, i)),
        out_shape=jax.ShapeDtypeStruct((nout, n), BF16),
        scratch_shapes=[pltpu.VMEM((tm, D_MODEL), BF16)],
        compiler_params=_params(("arbitrary",)),
        name=name,
    )(x, g.reshape(1, D_MODEL), mods, mods, wt)


def _norm_swiglu(cfg, x, g, mods, sub, w_tiles, tm, tn):
    n = x.shape[0]
    nj = D_FF // tn
    shift, scale = _mod_specs(cfg, tm, sub)
    return pl.pallas_call(
        functools.partial(_norm_swiglu_body, tm=tm),
        grid=(n // tm, nj),
        in_specs=[pl.BlockSpec((tm, D_MODEL), lambda i, j: (i, 0)),
                  pl.BlockSpec((1, D_MODEL), lambda i, j: (0, 0)),
                  shift, scale,
                  pl.BlockSpec((D_MODEL, 2 * tn), lambda i, j: (0, j))],
        out_specs=pl.BlockSpec((tm, tn), lambda i, j: (i, j)),
        out_shape=jax.ShapeDtypeStruct((n, D_FF), BF16),
        scratch_shapes=[pltpu.VMEM((tm, D_MODEL), BF16)],
        compiler_params=_params(("arbitrary", "arbitrary")),
        name="ffn_in",
    )(x, g.reshape(1, D_MODEL), mods, mods, w_tiles)


def _resid_mm_body(*refs, pieces, coef):
    lhs = refs[:pieces]
    ws = refs[pieces:2 * pieces]
    x_ref, gate_ref, o_ref = refs[2 * pieces:]
    acc = jnp.dot(lhs[0][...], ws[0][...], preferred_element_type=F32)
    for p in range(1, pieces):
        acc = acc + jnp.dot(lhs[p][...], ws[p][...], preferred_element_type=F32)
    o_ref[...] = x_ref[...] + (coef * gate_ref[0]) * acc


def _resid_matmul(cfg, lhs_list, w, x, mods, sub, coef, tm, tn, name):
    n = x.shape[0]
    per = cfg.ch // tm
    widths = [a.shape[1] for a in lhs_list]
    offs = np.cumsum([0] + widths[:-1])
    in_specs = [pl.BlockSpec((tm, k), lambda i, j: (i, 0)) for k in widths]
    for k, off in zip(widths, offs):
        assert off % k == 0
        in_specs.append(pl.BlockSpec((k, tn), lambda i, j, b=int(off // k): (b, j)))
    in_specs += [pl.BlockSpec((tm, tn), lambda i, j: (i, j)),
                 pl.BlockSpec((1, 1, tn), lambda i, j: ((i // per) * 9 + sub * 3 + 2, 0, j))]
    return pl.pallas_call(
        functools.partial(_resid_mm_body, pieces=len(lhs_list), coef=coef),
        grid=(n // tm, D_MODEL // tn),
        in_specs=in_specs,
        out_specs=pl.BlockSpec((tm, tn), lambda i, j: (i, j)),
        out_shape=jax.ShapeDtypeStruct((n, D_MODEL), F32),
        compiler_params=_params(("arbitrary", "arbitrary")),
        name=name,
    )(*lhs_list, *([w] * len(lhs_list)), x, mods)


def _final_norm_body(x_ref, g_ref, op_ref, os_ref, *, pblocks):
    x = x_ref[...]
    inv = lax.rsqrt(jnp.mean(x * x, axis=-1, keepdims=True) + EPS)
    y = x * inv * g_ref[...]
    i = pl.program_id(0)

    @pl.when(i < pblocks)
    def _():
        op_ref[...] = y

    @pl.when(i >= pblocks)
    def _():
        os_ref[...] = y


def _final_norm(cfg, x, g, tm=512):
    n = x.shape[0]
    npr = cfg.pc * cfg.ch
    pblocks = npr // tm
    return pl.pallas_call(
        functools.partial(_final_norm_body, pblocks=pblocks),
        grid=(n // tm,),
        in_specs=[pl.BlockSpec((tm, D_MODEL), lambda i: (i, 0)),
                  pl.BlockSpec((1, D_MODEL), lambda i: (0, 0))],
        out_specs=[pl.BlockSpec((tm, D_MODEL), lambda i: (jnp.minimum(i, pblocks - 1), 0)),
                   pl.BlockSpec((tm, D_MODEL), lambda i: (jnp.maximum(i - pblocks, 0), 0))],
        out_shape=[jax.ShapeDtypeStruct((npr, D_MODEL), F32),
                   jax.ShapeDtypeStruct((n - npr, D_MODEL), F32)],
        compiler_params=_params(("arbitrary",)),
        name="final_norm",
    )(x, g.reshape(1, D_MODEL))


def _t5_bucket_np(rel):
    nb = REL_BUCKETS // 2
    max_exact = nb // 2
    rel = np.asarray(rel, np.int64)
    base = np.where(rel > 0, nb, 0)
    n = np.abs(rel)
    nf = np.maximum(n, 1).astype(np.float64)
    large = max_exact + (np.log(nf / max_exact) / math.log(REL_MAX_DIST / max_exact)
                         * (nb - max_exact)).astype(np.int64)
    large = np.minimum(large, nb - 1)
    return (base + np.where(n < max_exact, n, large)).astype(np.int32)


EXPAND_ROWS = 32
EXPAND_UNROLL_MAX = 64


def _expand_body(tab_ref, idx_ref, o_ref, *, nb):
    t = pl.program_id(1)

    def chunk(r, carry):
        rows = pl.ds(pl.multiple_of(r * EXPAND_ROWS, EXPAND_ROWS), EXPAND_ROWS)
        idx = idx_ref[0, rows, :]
        pick = lambda b, acc: jnp.where(idx == b, tab_ref[t, b], acc)
        o_ref[0, 0, rows, :] = lax.fori_loop(0, nb, pick, jnp.zeros(idx.shape, F32),
                                             unroll=True if nb <= EXPAND_UNROLL_MAX else 8)
        return carry
    lax.fori_loop(0, idx_ref.shape[1] // EXPAND_ROWS, chunk, 0)


def _expand(tab, idx):
    ntab, nb = tab.shape
    ni, r, c = idx.shape
    assert r % EXPAND_ROWS == 0 and idx.min() >= 0 and idx.max() < nb
    return pl.pallas_call(
        functools.partial(_expand_body, nb=nb),
        grid=(ni, ntab),
        in_specs=[pl.BlockSpec(memory_space=pltpu.SMEM),
                  pl.BlockSpec((1, r, c), lambda i, t: (i, 0, 0))],
        out_specs=pl.BlockSpec((1, 1, r, c), lambda i, t: (i, t, 0, 0)),
        out_shape=jax.ShapeDtypeStruct((ni, ntab, r, c), F32),
        compiler_params=_params(("arbitrary", "arbitrary")),
        name="expand_table",
    )(tab, jnp.asarray(idx.astype(np.int32)))


def _banded_heads(heads, scores, finish):
    s_next = scores(0)
    for h in range(heads):
        s_cur = s_next
        if h + 1 < heads:
            s_next = scores(h + 1)
        finish(h, s_cur)


DILF_BQ = 256
DILF_GROUP = 4


def _dilf_flags(cfg):
    nchunks = cfg.pc + cfg.sb
    prev = np.array([1 if 0 < c < cfg.pc else 0 for c in range(nchunks)], np.int32)
    nxt = np.array([1 if c < cfg.pc - 1 else 0 for c in range(nchunks)], np.int32)
    return jnp.asarray(prev), jnp.asarray(nxt)


def _dilf_scores(q, k, bias, valid):
    s = lax.dot_general(q, k, (((1,), (1,)), ((), ())), preferred_element_type=F32)
    return jnp.where(valid, s + bias, NEG)


def _dilf_finish(s, v):
    m = jnp.max(s, axis=-1, keepdims=True)
    e = jnp.exp(s - m)
    den = jnp.sum(e, axis=-1, keepdims=True)
    o = jnp.dot(e.astype(BF16), v, preferred_element_type=F32)
    return o / den, m + jnp.log(den)


def _pipelined(tasks):
    s_next = tasks[0][0]()
    for t, (_, finish) in enumerate(tasks):
        s_cur = s_next
        if t + 1 < len(tasks):
            s_next = tasks[t + 1][0]()
        finish(s_cur)


def _dilf_valid(bq, lo, hi):
    nk = bq + 2 * A_HALF
    row = lax.broadcasted_iota(jnp.int32, (bq, nk), 0)
    col = lax.broadcasted_iota(jnp.int32, (bq, nk), 1)
    off = col - row
    return (off >= 0) & (off <= 2 * A_HALF) & (col >= lo) & (col < hi)


def _dilf_body(hp_ref, hn_ref,
               q1_ref, k1p_ref, k1_ref, k1n_ref, v1p_ref, v1_ref, v1n_ref,
               q2_ref, k2p_ref, k2_ref, k2n_ref, v2p_ref, v2_ref, v2n_ref,
               q3_ref, k3p_ref, k3_ref, k3n_ref, v3p_ref, v3_ref, v3n_ref,
               b1_ref, b2_ref, b3_ref, o_ref, oacc, lacc, *, ch):
    c = pl.program_id(0)
    lo0 = jnp.where(hp_ref[c] != 0, 0, A_HALF)
    hi_cut = jnp.where(hn_ref[c] != 0, 0, A_HALF)

    def block_valid(bq, first, last):
        nk = bq + 2 * A_HALF
        return _dilf_valid(bq, lo0 if first else 0, nk - hi_cut if last else nk)

    def keep(g, rows, o, lse):
        oacc[g, rows, :] = o
        lacc[g, rows, :] = jnp.broadcast_to(lse, o.shape)

    def task(g, q_rows, k, v, bias_ref, valid, out_rows):
        scores = lambda: _dilf_scores(q_rows(), k, bias_ref[0], valid)
        finish = lambda s: keep(g, out_rows, *_dilf_finish(s, v))
        return scores, finish

    bq = DILF_BQ
    tasks = []
    nblk = ch // bq
    k = jnp.concatenate([k1p_ref[...], k1_ref[...], k1n_ref[...]], axis=0)
    v = jnp.concatenate([v1p_ref[...], v1_ref[...], v1n_ref[...]], axis=0)
    for b in range(nblk):
        rows = slice(b * bq, (b + 1) * bq)
        keys = slice(b * bq, (b + 1) * bq + 2 * A_HALF)
        tasks.append(task(0, lambda rows=rows: q1_ref[rows, :], k[keys], v[keys], b1_ref,
                          block_valid(bq, b == 0, b == nblk - 1), rows))
    dil = A_CONFIGS[1][1]
    lc = ch // dil
    nblk = lc // bq
    for r in range(dil):
        sub = lambda ref, n: ref[pl.ds(r, n, stride=dil), :].astype(BF16)
        k = jnp.concatenate([sub(k2p_ref, A_HALF), sub(k2_ref, lc), sub(k2n_ref, A_HALF)], axis=0)
        v = jnp.concatenate([sub(v2p_ref, A_HALF), sub(v2_ref, lc), sub(v2n_ref, A_HALF)], axis=0)
        for b in range(nblk):
            rows = pl.ds(r + b * bq * dil, bq, stride=dil)
            keys = slice(b * bq, (b + 1) * bq + 2 * A_HALF)
            tasks.append(task(1, lambda rows=rows: q2_ref[rows, :].astype(BF16), k[keys], v[keys],
                              b2_ref, block_valid(bq, b == 0, b == nblk - 1), rows))
    _pipelined(tasks)

    dil3 = A_CONFIGS[2][1]
    lc3 = ch // dil3
    valid3 = block_valid(lc3, True, True)

    def residues(i, carry):
        group = []
        for u in range(DILF_GROUP):
            r = i * DILF_GROUP + u
            sub = lambda ref, n, r=r: ref[pl.ds(r, n, stride=dil3), :].astype(BF16)
            k = jnp.concatenate([sub(k3p_ref, A_HALF), sub(k3_ref, lc3), sub(k3n_ref, A_HALF)],
                                axis=0)
            v = jnp.concatenate([sub(v3p_ref, A_HALF), sub(v3_ref, lc3), sub(v3n_ref, A_HALF)],
                                axis=0)
            rows = pl.ds(r, lc3, stride=dil3)
            group.append(task(2, lambda rows=rows: q3_ref[rows, :].astype(BF16), k, v, b3_ref,
                              valid3, rows))
        _pipelined(group)
        return carry
    lax.fori_loop(0, dil3 // DILF_GROUP, residues, 0)

    def combine(i, carry):
        rows = pl.ds(pl.multiple_of(i * DILF_BQ, DILF_BQ), DILF_BQ)
        l0, l1, l2 = lacc[0, rows, :], lacc[1, rows, :], lacc[2, rows, :]
        m = jnp.maximum(jnp.maximum(l0, l1), l2)
        e0, e1, e2 = jnp.exp(l0 - m), jnp.exp(l1 - m), jnp.exp(l2 - m)
        tot = e0 + e1 + e2
        out = (e0 / tot) * oacc[0, rows, :] + (e1 / tot) * oacc[1, rows, :] \
            + (e2 / tot) * oacc[2, rows, :]
        o_ref[rows, :] = out.astype(o_ref.dtype)
        return carry
    lax.fori_loop(0, ch // DILF_BQ, combine, 0)


def _dilated_fused(cfg, p1, p23, biases):
    n = cfg.n
    ch = cfg.ch
    hp, hn = _dilf_flags(cfg)
    nh = A_HEADS
    assert ch // A_CONFIGS[2][1] == 2 * A_HALF and ch % DILF_BQ == 0

    def specs(dil, qb, kb, vb):
        halo = A_HALF * dil
        per = ch // halo
        last = n // halo - 1
        own = lambda cb: pl.BlockSpec((ch, HEAD_DIM), lambda c, h, p, x: (c, cb + h))
        before = lambda cb: pl.BlockSpec(
            (halo, HEAD_DIM), lambda c, h, p, x: (jnp.maximum(c * per - 1, 0), cb + h))
        after = lambda cb: pl.BlockSpec(
            (halo, HEAD_DIM), lambda c, h, p, x: (jnp.minimum((c + 1) * per, last), cb + h))
        return [own(qb), before(kb), own(kb), after(kb), before(vb), own(vb), after(vb)]

    bias_spec = lambda b: pl.BlockSpec((1,) + b.shape[1:], lambda c, h, p, x: (h, 0, 0))
    grid_spec = pltpu.PrefetchScalarGridSpec(
        num_scalar_prefetch=2,
        grid=(n // ch, nh),
        in_specs=(specs(1, 0, nh, 2 * nh) + specs(A_CONFIGS[1][1], 0, 2 * nh, 4 * nh)
                  + specs(A_CONFIGS[2][1], nh, 3 * nh, 5 * nh)
                  + [bias_spec(b) for b in biases]),
        out_specs=pl.BlockSpec((ch, HEAD_DIM), lambda c, h, p, x: (c, h)),
        scratch_shapes=[pltpu.VMEM((3, ch, HEAD_DIM), F32), pltpu.VMEM((3, ch, HEAD_DIM), F32)],
    )
    return pl.pallas_call(
        functools.partial(_dilf_body, ch=ch),
        grid_spec=grid_spec,
        out_shape=jax.ShapeDtypeStruct((n, nh * HEAD_DIM), BF16),
        compiler_params=_params(("arbitrary", "arbitrary")),
        name="dilated_fused",
    )(hp, hn, *([p1] * 7), *([p23] * 14), *biases)


def _dilf_bias(rel_bias, g, dil, bq):
    off = np.arange(bq + 2 * A_HALF)[None, :] - np.arange(bq)[:, None] - A_HALF
    bucket = _t5_bucket_np(dil * np.clip(off, -A_HALF, A_HALF))
    return _expand(rel_bias[:, g * A_HEADS:(g + 1) * A_HEADS].T, bucket[None])[0]


def _flash_worklist(cfg, tq, tk, rel_lo=None, rel_hi=None):
    qb, kb, tile, flags = [], [], [], []
    for start, length in cfg.seqs():
        assert length % tq == 0 and length % tk == 0 and start % tq == 0 and start % tk == 0
        nk = length // tk
        for qi in range(length // tq):
            for kj in range(nk):
                qb.append(start // tq + qi)
                kb.append(start // tk + kj)
                if rel_lo is not None:
                    assert (kj * tk) % tq == 0
                    d = (kj * tk - qi * tq) // tq
                    tile.append(min(max(d, rel_lo), rel_hi) - rel_lo)
                else:
                    tile.append(0)
                flags.append((1 if kj == 0 else 0) | (2 if kj == nk - 1 else 0))
    as_i32 = lambda a: jnp.asarray(np.asarray(a, np.int32))
    return as_i32(qb), as_i32(kb), as_i32(tile), as_i32(flags)


def _flash_init(flags, m_scr, acc_scr):
    @pl.when((flags & 1) != 0)
    def _():
        m_scr[...] = jnp.full(m_scr.shape, -jnp.inf, F32)
        acc_scr[...] = jnp.zeros(acc_scr.shape, F32)


def _flash_softmax(h, s, m_scr):
    m_prev = m_scr[h]
    m_new = jnp.maximum(m_prev, jnp.max(s, axis=0, keepdims=True))
    m_scr[h] = m_new
    return jnp.exp2(m_prev - m_new), jnp.exp2(s - m_new).astype(BF16)


def _flash_accumulate(h, alpha, p, vt, acc_scr):
    lhs = jnp.concatenate([vt, jnp.ones((ONES_ROWS, vt.shape[1]), BF16)], axis=0)
    acc_scr[h] = alpha * acc_scr[h] + jnp.dot(lhs, p, preferred_element_type=F32)


def _flash_heads(heads, scores, vt_rows, m_scr, acc_scr):
    s_next = scores(0)
    pending = None
    for h in range(heads):
        s_cur = s_next
        if h + 1 < heads:
            s_next = scores(h + 1)
        alpha, p = _flash_softmax(h, s_cur, m_scr)
        if pending is not None:
            _flash_accumulate(*pending, vt_rows(pending[0]), acc_scr)
        pending = (h, alpha, p)
    _flash_accumulate(*pending, vt_rows(pending[0]), acc_scr)


DIFF_TQ = 512
DIFF_TK = 512


def _diff_tile_range(max_len):
    rel = np.arange(-max_len + 1, max_len)
    b = _t5_bucket_np(rel)
    sat_pos = int(rel[b != b[-1]].max()) + 1
    sat_neg = int(rel[b != b[0]].min()) - 1
    hi = -(-(sat_pos + DIFF_TQ - 1) // DIFF_TQ)
    lo = (sat_neg - (DIFF_TK - 1)) // DIFF_TQ
    return lo, hi


def _diff_bias_tiles(rel_bias, lo, hi):
    d = np.arange(lo, hi + 1)[:, None, None] * DIFF_TQ
    rel = d + np.arange(DIFF_TK)[None, :, None] - np.arange(DIFF_TQ)[None, None, :]
    return _expand(rel_bias[:, 3 * A_HEADS:].T * LOG2E, _t5_bucket_np(rel))


def _diff_body(qb_ref, kb_ref, tile_ref, fl_ref, q_ref, k_ref, vt_ref, bias_ref, lam_ref, g_ref,
               o_ref, m_scr, acc_scr, *, lam_init):
    flags = fl_ref[pl.program_id(0)]
    tq = DIFF_TQ
    _flash_init(flags, m_scr, acc_scr)

    lane = lax.broadcasted_iota(jnp.int32, (tq, HEAD_DIM), 1)

    def scores(h):
        cols = slice(h * HEAD_DIM, (h + 1) * HEAD_DIM)
        qh = q_ref[:, cols]
        zero = jnp.zeros_like(qh)
        q2 = jnp.concatenate([jnp.where(lane < B_QK_DIM, qh, zero),
                              jnp.where(lane >= B_QK_DIM, qh, zero)], axis=0)
        s = lax.dot_general(k_ref[:, cols], q2, (((1,), (1,)), ((), ())),
                            preferred_element_type=F32)
        b = bias_ref[0, h]
        return s + jnp.concatenate([b, b], axis=1)

    vt_rows = lambda h: vt_ref[h * HEAD_DIM:(h + 1) * HEAD_DIM, :]
    _flash_heads(B_HEADS, scores, vt_rows, m_scr, acc_scr)

    @pl.when((flags & 2) != 0)
    def _():
        lf = lam_ref[...]
        lam = (jnp.exp(jnp.sum(lf[0:1] * lf[1:2], axis=-1, keepdims=True))
               - jnp.exp(jnp.sum(lf[2:3] * lf[3:4], axis=-1, keepdims=True)) + lam_init)
        for h in range(B_HEADS):
            cols = slice(h * HEAD_DIM, (h + 1) * HEAD_DIM)
            acc = acc_scr[h]
            att = acc[:HEAD_DIM] / acc[HEAD_DIM:HEAD_DIM + 1]
            o = att[:, :tq] - lam * att[:, tq:]
            inv = lax.rsqrt(jnp.mean(o * o, axis=0, keepdims=True) + EPS)
            o_ref[:, cols] = (((o * inv).T * g_ref[...]) * (1.0 - lam_init)).astype(o_ref.dtype)


def _diff_attention(cfg, proj, qcol, kcol, vt, bias_tiles, tile_lo, tile_hi, lam, subln_g,
                    lam_init):
    n = cfg.n
    tq, tk = DIFF_TQ, DIFF_TK
    qb, kb, tile, flags = _flash_worklist(cfg, tq, tk, tile_lo, tile_hi)
    grid_spec = pltpu.PrefetchScalarGridSpec(
        num_scalar_prefetch=4,
        grid=(qb.shape[0],),
        in_specs=[pl.BlockSpec((tq, B_W), lambda s, q, k, t, f: (q[s], qcol)),
                  pl.BlockSpec((tk, B_W), lambda s, q, k, t, f: (k[s], kcol)),
                  pl.BlockSpec((B_W, tk), lambda s, q, k, t, f: (0, k[s])),
                  pl.BlockSpec((1, B_HEADS, tk, tq), lambda s, q, k, t, f: (t[s], 0, 0, 0)),
                  pl.BlockSpec((4, B_QK_DIM), lambda s, q, k, t, f: (0, 0)),
                  pl.BlockSpec((1, HEAD_DIM), lambda s, q, k, t, f: (0, 0))],
        out_specs=pl.BlockSpec((tq, B_W), lambda s, q, k, t, f: (q[s], 0)),
        scratch_shapes=[pltpu.VMEM((B_HEADS, 1, 2 * tq), F32),
                        pltpu.VMEM((B_HEADS, HEAD_DIM + ONES_ROWS, 2 * tq), F32)],
    )
    return pl.pallas_call(
        functools.partial(_diff_body, lam_init=lam_init),
        grid_spec=grid_spec,
        out_shape=jax.ShapeDtypeStruct((n, B_W), BF16),
        compiler_params=_params(("arbitrary",)),
        name="diff_attention",
    )(qb, kb, tile, flags, proj, proj, vt, bias_tiles, lam, subln_g.reshape(1, HEAD_DIM))


MLA_TM = 512
MLA_TQ = 512
MLA_TK = 512


def _rope_tables(max_len):
    inv = ROPE_THETA ** (-jnp.arange(0, C_ROPE, 2, dtype=F32) / C_ROPE)
    ang = jnp.arange(max_len, dtype=F32)[:, None] * inv[None, :]
    cos, sin = jnp.cos(ang), jnp.sin(ang)
    zero = jnp.zeros((max_len, HEAD_DIM - C_ROPE), F32)
    return (jnp.concatenate([cos, cos, zero], axis=1),
            jnp.concatenate([-sin, sin, zero], axis=1))


def _mla_prep_body(p_ref, gq_ref, gkv_ref, wqa_ref, wqb_ref, wk_ref, wvt_ref, cos_ref, sin_ref,
                   q_ref, k_ref, vt_ref, *, scale):
    def normed(x, g):
        inv = lax.rsqrt(jnp.mean(x * x, axis=-1, keepdims=True) + EPS)
        return (x * inv * g).astype(BF16)

    cq = normed(p_ref[:, 0:C_LORA], gq_ref[...])
    ckv = normed(p_ref[:, C_LORA:2 * C_LORA], gkv_ref[...])
    cos = cos_ref[...]
    sin = sin_ref[...]
    qa = jnp.dot(cq, wqa_ref[...], preferred_element_type=F32)
    qb = jnp.dot(cq, wqb_ref[...], preferred_element_type=F32)
    kn = jnp.dot(ckv, wk_ref[...], preferred_element_type=F32)
    vt_ref[...] = lax.dot_general(wvt_ref[...], ckv, (((1,), (1,)), ((), ())),
                                  preferred_element_type=F32).astype(BF16)
    kr = p_ref[:, 2 * C_LORA:2 * C_LORA + HEAD_DIM]
    kr_sw = p_ref[:, 2 * C_LORA + HEAD_DIM:2 * C_LORA + 2 * HEAD_DIM]
    k_rope = (kr * cos + kr_sw * sin).astype(BF16)
    for h in range(C_HEADS):
        a0 = h * C_QK_PAD
        a1 = a0 + HEAD_DIM
        a2 = a0 + C_QK_PAD
        b = slice(h * HEAD_DIM, (h + 1) * HEAD_DIM)
        q_ref[:, a0:a1] = (qa[:, a0:a1] * scale).astype(BF16)
        q_ref[:, a1:a2] = ((qa[:, a1:a2] * cos + qb[:, b] * sin) * scale).astype(BF16)
        k_ref[:, a0:a1] = kn[:, b].astype(BF16)
        k_ref[:, a1:a2] = k_rope


def _mla_prep(cfg, proj_c, gq, gkv, wqa, wqb, wk, wvt, cos_tab, sin_tab):
    n = cfg.n
    tm = MLA_TM
    pblocks = cfg.pc * cfg.ch // tm
    per = cfg.ch // tm
    pos = lambda i: (jnp.where(i < pblocks, i, (i - pblocks) % per), 0)
    full = lambda shape: pl.BlockSpec(shape, lambda i: (0, 0))
    qk_w = C_HEADS * C_QK_PAD
    v_w = C_HEADS * HEAD_DIM
    return pl.pallas_call(
        functools.partial(_mla_prep_body, scale=(C_NOPE + C_ROPE) ** -0.5 * LOG2E),
        grid=(n // tm,),
        in_specs=[pl.BlockSpec((tm, proj_c.shape[1]), lambda i: (i, 0)),
                  full((1, C_LORA)), full((1, C_LORA)),
                  full(wqa.shape), full(wqb.shape), full(wk.shape), full(wvt.shape),
                  pl.BlockSpec((tm, HEAD_DIM), pos), pl.BlockSpec((tm, HEAD_DIM), pos)],
        out_specs=[pl.BlockSpec((tm, qk_w), lambda i: (i, 0)),
                   pl.BlockSpec((tm, qk_w), lambda i: (i, 0)),
                   pl.BlockSpec((v_w, tm), lambda i: (0, i))],
        out_shape=[jax.ShapeDtypeStruct((n, qk_w), BF16),
                   jax.ShapeDtypeStruct((n, qk_w), BF16),
                   jax.ShapeDtypeStruct((v_w, n), BF16)],
        compiler_params=_params(("arbitrary",)),
        name="mla_prep",
    )(proj_c, gq.reshape(1, C_LORA), gkv.reshape(1, C_LORA), wqa, wqb, wk, wvt, cos_tab, sin_tab)


def _mla_body(qb_ref, kb_ref, tile_ref, fl_ref, q_ref, k_ref, vt_ref, o_ref, m_scr, acc_scr):
    flags = fl_ref[pl.program_id(0)]
    _flash_init(flags, m_scr, acc_scr)

    def scores(h):
        qk = slice(h * C_QK_PAD, (h + 1) * C_QK_PAD)
        return lax.dot_general(k_ref[:, qk], q_ref[:, qk], (((1,), (1,)), ((), ())),
                               preferred_element_type=F32)

    vt_rows = lambda h: vt_ref[h * HEAD_DIM:(h + 1) * HEAD_DIM, :]
    _flash_heads(C_HEADS, scores, vt_rows, m_scr, acc_scr)

    @pl.when((flags & 2) != 0)
    def _():
        for h in range(C_HEADS):
            vc = slice(h * HEAD_DIM, (h + 1) * HEAD_DIM)
            acc = acc_scr[h]
            o_ref[:, vc] = (acc[:HEAD_DIM] / acc[HEAD_DIM:HEAD_DIM + 1]).T.astype(o_ref.dtype)


def _mla_attention(cfg, q, k, vt):
    n = cfg.n
    tq, tk = MLA_TQ, MLA_TK
    qb, kb, tile, flags = _flash_worklist(cfg, tq, tk)
    qk_w = C_HEADS * C_QK_PAD
    v_w = C_HEADS * HEAD_DIM
    grid_spec = pltpu.PrefetchScalarGridSpec(
        num_scalar_prefetch=4,
        grid=(qb.shape[0],),
        in_specs=[pl.BlockSpec((tq, qk_w), lambda s, q_, k_, t, f: (q_[s], 0)),
                  pl.BlockSpec((tk, qk_w), lambda s, q_, k_, t, f: (k_[s], 0)),
                  pl.BlockSpec((v_w, tk), lambda s, q_, k_, t, f: (0, k_[s]))],
        out_specs=pl.BlockSpec((tq, v_w), lambda s, q_, k_, t, f: (q_[s], 0)),
        scratch_shapes=[pltpu.VMEM((C_HEADS, 1, tq), F32),
                        pltpu.VMEM((C_HEADS, HEAD_DIM + ONES_ROWS, tq), F32)],
    )
    return pl.pallas_call(
        _mla_body,
        grid_spec=grid_spec,
        out_shape=jax.ShapeDtypeStruct((n, v_w), BF16),
        compiler_params=_params(("arbitrary",)),
        name="mla_attention",
    )(qb, kb, tile, flags, q, k, vt)


def _na_tables(cfg):
    prev, nxt, var = [], [], []
    for start, length in cfg.seqs():
        nb = length // NA_BLOCK
        assert length % NA_BLOCK == 0 and start % NA_BLOCK == 0 and nb >= 3
        b0 = start // NA_BLOCK
        for r in range(nb):
            prev.append(b0 + max(r - 1, 0))
            nxt.append(b0 + min(r + 1, nb - 1))
            var.append(0 if r == 0 else (2 if r == nb - 1 else 1))
    as_i32 = lambda a: jnp.asarray(np.asarray(a, np.int32))
    return as_i32(prev), as_i32(nxt), as_i32(var)


def _na_window_start(a, variant):
    centred = a - NA_ROWS // 2
    return (max(centred, 0), centred, min(centred, 0))[variant]


def _na_bias_tables(rpb):
    ncol = 2 * NA_COLS - 1
    c = np.arange(GRID_W)[:, None, None]
    b = np.arange(NA_ROWS)[None, :, None]
    kc = np.arange(GRID_W)[None, None, :]
    cstart = np.clip(c - NA_COLS // 2, 0, GRID_W - NA_COLS)
    col_ok = (kc >= cstart) & (kc < cstart + NA_COLS)
    dc = np.clip(kc - c, -(NA_COLS - 1), NA_COLS - 1) + (NA_COLS - 1)
    masked = NA_ROWS * ncol
    idx = np.where(col_ok, b * ncol + dc, masked).reshape(1, GRID_W, NA_BLOCK)
    rows = np.arange(NA_ROWS)[:, None] + np.arange(NA_ROWS)[None, :]
    tab = rpb[:, rows, :].reshape(D_HEADS * NA_ROWS, NA_ROWS * ncol)
    tab = jnp.concatenate([tab, jnp.full((tab.shape[0], 1), NEG, F32)], axis=1)
    slabs = _expand(tab, idx)[0].reshape(D_HEADS, NA_ROWS, GRID_W, NA_BLOCK)
    variants = []
    for variant in range(3):
        row_blocks = []
        for a in range(NA_ROWS):
            start = _na_window_start(a, variant)
            left = (NA_ROWS + start) * GRID_W
            row_blocks.append(jnp.pad(slabs[:, start - a + NA_ROWS - 1],
                                      ((0, 0), (0, 0), (left, 2 * NA_BLOCK - left)),
                                      constant_values=NEG))
        variants.append(jnp.concatenate(row_blocks, axis=1))
    return jnp.stack(variants, axis=0)


def _na_body(prev_ref, nxt_ref, var_ref, q_ref, kp_ref, km_ref, kn_ref, vp_ref, vm_ref, vn_ref,
             tab_ref, o_ref):
    k = jnp.concatenate([kp_ref[...], km_ref[...], kn_ref[...]], axis=0)
    v = jnp.concatenate([vp_ref[...], vm_ref[...], vn_ref[...]], axis=0)
    def scores(h):
        cols = slice(h * HEAD_DIM, (h + 1) * HEAD_DIM)
        s = lax.dot_general(q_ref[:, cols], k[:, cols], (((1,), (1,)), ((), ())),
                            preferred_element_type=F32)
        return s + tab_ref[0, h]

    def finish(h, s):
        cols = slice(h * HEAD_DIM, (h + 1) * HEAD_DIM)
        m = jnp.max(s, axis=-1, keepdims=True)
        e = jnp.exp(s - m)
        den = jnp.sum(e, axis=-1, keepdims=True)
        o = jnp.dot(e.astype(BF16), v[:, cols], preferred_element_type=F32)
        o_ref[:, cols] = (o / den).astype(o_ref.dtype)

    _banded_heads(D_HEADS, scores, finish)


def _na_attention(cfg, qkv, tabs):
    n = cfg.n
    w = D_HEADS * HEAD_DIM
    prev, nxt, var = _na_tables(cfg)
    own = lambda cb: pl.BlockSpec((NA_BLOCK, w), lambda i, p, x, t: (i, cb))
    before = lambda cb: pl.BlockSpec((NA_BLOCK, w), lambda i, p, x, t: (p[i], cb))
    after = lambda cb: pl.BlockSpec((NA_BLOCK, w), lambda i, p, x, t: (x[i], cb))
    grid_spec = pltpu.PrefetchScalarGridSpec(
        num_scalar_prefetch=3,
        grid=(n // NA_BLOCK,),
        in_specs=[own(0), before(1), own(1), after(1), before(2), own(2), after(2),
                  pl.BlockSpec((1, D_HEADS, NA_BLOCK, 3 * NA_BLOCK),
                               lambda i, p, x, t: (t[i], 0, 0, 0))],
        out_specs=pl.BlockSpec((NA_BLOCK, w), lambda i, p, x, t: (i, 0)),
    )
    return pl.pallas_call(
        _na_body,
        grid_spec=grid_spec,
        out_shape=jax.ShapeDtypeStruct((n, w), BF16),
        compiler_params=_params(("arbitrary",)),
        name="na_attention",
    )(prev, nxt, var, qkv, qkv, qkv, qkv, qkv, qkv, qkv, tabs)


EVEN_BLOCKS_ROWMAJOR = (0, 3, 6, 9, 10)
EVEN_BLOCKS_STRIDED = (1, 2, 4, 5, 7, 8)
EVEN_BLOCK_VT = 11


def _even_weights(w_in):
    blk = lambda b: w_in[:, b * B_W:(b + 1) * B_W]
    w_a = jnp.concatenate([blk(b) for b in EVEN_BLOCKS_ROWMAJOR], axis=1)
    w_b = jnp.concatenate([blk(b) for b in EVEN_BLOCKS_STRIDED], axis=1)
    return w_a, w_b, blk(EVEN_BLOCK_VT).T


def _even_col_scales():
    cs_a = np.ones((1, len(EVEN_BLOCKS_ROWMAJOR) * B_W), np.float32)
    cs_a[0, :B_W] = HEAD_DIM ** -0.5
    cs_a[0, 3 * B_W:4 * B_W] = B_QK_DIM ** -0.5 * LOG2E
    cs_b = np.ones((1, len(EVEN_BLOCKS_STRIDED) * B_W), np.float32)
    cs_b[0, :2 * B_W] = HEAD_DIM ** -0.5
    return jnp.asarray(cs_a), jnp.asarray(cs_b)


def _even_tables(cfg, rel_bias):
    dil = [_dilf_bias(rel_bias, gi, d, min(DILF_BQ, cfg.ch // d))
           for gi, (_, d) in enumerate(A_CONFIGS)]
    tile_lo, tile_hi = _diff_tile_range(cfg.pc * cfg.ch)
    return dil, _diff_bias_tiles(rel_bias, tile_lo, tile_hi), tile_lo, tile_hi


def _even_mixer(cfg, x, g, mods, weights, w_out, lam, subln_g, tables, lam_init, tm):
    w_a, w_b, wvt = weights
    dil_bias, diff_tiles, tile_lo, tile_hi = tables
    cs_a, cs_b = _even_col_scales()
    p1 = _norm_matmul(cfg, x, g, mods, 1, w_a, cs_a, BF16, tm, 1024, "even_in_a")
    p23 = _norm_matmul(cfg, x, g, mods, 1, w_b, cs_b, F32, tm, 1024, "even_in_b")
    vt = _norm_matmul_t(cfg, x, g, mods, 1, wvt, tm, "even_in_vt")
    o_a = _dilated_fused(cfg, p1, p23, dil_bias)
    o_b = _diff_attention(cfg, p1, 3, 4, vt, diff_tiles, tile_lo, tile_hi, lam, subln_g, lam_init)
    return _resid_matmul(cfg, [o_a, o_b], w_out, x, mods, 1, 1.0, tm, 1024, "even_out")


def _odd_weights(w_in, w_q_up, w_kv_up):
    half = C_ROPE // 2
    swap = np.concatenate([np.arange(half, C_ROPE), np.arange(half)])
    o2 = 2 * C_LORA
    zpad = jnp.zeros((D_MODEL, HEAD_DIM - C_ROPE), w_in.dtype)
    kr = w_in[:, o2:o2 + C_ROPE]
    w_c = jnp.concatenate([w_in[:, :o2], kr, zpad, kr[:, swap], zpad], axis=1)
    w_d = w_in[:, o2 + C_ROPE:]
    q3 = w_q_up.reshape(C_LORA, C_HEADS, C_NOPE + C_ROPE)
    zq = jnp.zeros((C_LORA, C_HEADS, C_QK_PAD - C_NOPE - C_ROPE), w_q_up.dtype)
    wqa = jnp.concatenate([q3, zq], axis=2).reshape(C_LORA, C_HEADS * C_QK_PAD)
    wqb = jnp.concatenate([q3[:, :, C_NOPE:][:, :, swap], zq], axis=2).reshape(
        C_LORA, C_HEADS * HEAD_DIM)
    kv3 = w_kv_up.reshape(C_LORA, C_HEADS, 2 * HEAD_DIM)
    wk = kv3[:, :, :C_NOPE].reshape(C_LORA, C_HEADS * HEAD_DIM)
    wvt = kv3[:, :, C_NOPE:].reshape(C_LORA, C_HEADS * HEAD_DIM).T
    return w_c, w_d, wqa, wqb, wk, wvt


def _odd_mixer(cfg, x, g, mods, weights, gq, gkv, rpb, w_out, rope, tm):
    w_c, w_d, wqa, wqb, wk, wvt = weights
    ones_c = jnp.ones((1, w_c.shape[1]), F32)
    proj_c = _norm_matmul(cfg, x, g, mods, 1, w_c, ones_c, F32, tm, w_c.shape[1], "odd_in_latent")
    cs = np.ones((1, w_d.shape[1]), np.float32)
    cs[0, :D_HEADS * HEAD_DIM] = HEAD_DIM ** -0.5
    qkv_d = _norm_matmul(cfg, x, g, mods, 1, w_d, jnp.asarray(cs), BF16, tm, w_d.shape[1],
                         "odd_in_na")
    q, k, vt = _mla_prep(cfg, proj_c, gq, gkv, wqa, wqb, wk, wvt, *rope)
    o_c = _mla_attention(cfg, q, k, vt)
    o_d = _na_attention(cfg, qkv_d, _na_bias_tables(rpb))
    return _resid_matmul(cfg, [o_c, o_d], w_out, x, mods, 1, 1.0, tm, 1024, "odd_out")


def _trunk(cfg, x, c_pad, ada_w, ada_b, norm_g, ffn_w_in, ffn_w_out, rel_bias, ev_w_in, ev_w_out,
           diff_lambda, diff_subln_g, od_w_in, mla_q_norm_g, mla_kv_norm_g, mla_w_q_up,
           mla_w_kv_up, na_rpb, od_w_out, final_norm_g):
    depth = ada_w.shape[0]
    tm = 1024
    nseq = 1 + cfg.sb
    mod_all = _modulation(c_pad, ada_w, ada_b)[:, :nseq]
    chunk_seq = np.array([0] * cfg.pc + list(range(1, nseq)))
    mod_all = mod_all[:, chunk_seq].reshape(depth, (cfg.pc + cfg.sb) * 9, 1, D_MODEL)
    rope = _rope_tables(cfg.pc * cfg.ch)
    even_tables = _even_tables(cfg, rel_bias)
    for i in range(depth):
        mods = mod_all[i]
        j = i // 2
        hid = _norm_swiglu(cfg, x, norm_g[i, 0], mods, 0,
                           _interleave_gate_up(ffn_w_in[i, 0].astype(BF16), 512), tm, 512)
        x = _resid_matmul(cfg, [hid], ffn_w_out[i, 0].astype(BF16), x, mods, 0, 0.5, tm, 512,
                          "ffn_out")
        if i % 2 == 0:
            x = _even_mixer(cfg, x, norm_g[i, 1], mods, _even_weights(ev_w_in[j].astype(BF16)),
                            ev_w_out[j].astype(BF16), diff_lambda[j], diff_subln_g[j], even_tables,
                            0.8 - 0.6 * math.exp(-0.3 * i), tm)
        else:
            weights = tuple(a.astype(BF16) for a in
                            _odd_weights(od_w_in[j], mla_w_q_up[j], mla_w_kv_up[j]))
            x = _odd_mixer(cfg, x, norm_g[i, 1], mods, weights, mla_q_norm_g[j], mla_kv_norm_g[j],
                           na_rpb[j], od_w_out[j].astype(BF16), rope, tm)
        hid = _norm_swiglu(cfg, x, norm_g[i, 2], mods, 2,
                           _interleave_gate_up(ffn_w_in[i, 1].astype(BF16), 512), tm, 512)
        x = _resid_matmul(cfg, [hid], ffn_w_out[i, 1].astype(BF16), x, mods, 2, 0.5, tm, 512,
                          "ffn_out")
    return _final_norm(cfg, x, final_norm_g)


def kernel(x_prompt, x_sample, c_prompt, c_sample, ada_w, ada_b, norm_g, ffn_w_in, ffn_w_out, rel_bias, ev_w_in, ev_w_out, diff_lambda, diff_subln_g, od_w_in, mla_q_norm_g, mla_kv_norm_g, mla_w_q_up, mla_w_kv_up, na_rpb, od_w_out, final_norm_g):
    pb, pt, _ = x_prompt.shape
    sb, st, _ = x_sample.shape
    assert pb == 1 and pt % st == 0
    cfg = Cfg(ch=st, pc=pt // st, sb=sb)
    x = jnp.concatenate([x_prompt.reshape(-1, D_MODEL), x_sample.reshape(-1, D_MODEL)], axis=0)
    c = jnp.concatenate([c_prompt, c_sample], axis=0)
    c_pad = jnp.pad(c, ((0, -c.shape[0] % 8), (0, 0)))
    y_p, y_s = _trunk(cfg, x, c_pad, ada_w, ada_b, norm_g, ffn_w_in, ffn_w_out, rel_bias, ev_w_in,
               ev_w_out, diff_lambda, diff_subln_g, od_w_in, mla_q_norm_g, mla_kv_norm_g,
               mla_w_q_up, mla_w_kv_up, na_rpb, od_w_out, final_norm_g)
    return (y_p.reshape(x_prompt.shape), y_s.reshape(x_sample.shape))
```

```python
import functools
import math
from typing import NamedTuple

import numpy as np
import jax
import jax.numpy as jnp
from jax import lax
from jax.experimental import pallas as pl
from jax.experimental.pallas import tpu as pltpu

F32 = jnp.float32
BF16 = jnp.bfloat16

D_MODEL = 2048
D_FF = 5632
HEAD_DIM = 128
A_HEADS = 8
A_CONFIGS = ((128, 1), (512, 4), (2048, 16))
A_HALF = 64
A_IN = 3 * 3 * A_HEADS * HEAD_DIM
B_HEADS = 8
B_QK_DIM = 64
B_W = B_HEADS * 2 * B_QK_DIM
EVEN_IN = A_IN + 3 * B_W
C_HEADS = 12
C_LORA = 512
C_NOPE = 128
C_ROPE = 64
C_QK_PAD = 256
ROPE_THETA = 10000.0
D_HEADS = 4
GRID_W = 64
NA_ROWS = 8
NA_COLS = 16
NA_BLOCK = NA_ROWS * GRID_W
REL_BUCKETS = 32
REL_MAX_DIST = 1024
EPS = 1e-6
NEG = -1e30
LOG2E = math.log2(math.e)
ONES_ROWS = 16

V7X_VMEM_BYTES = 64 * 1024 * 1024
VMEM_LIMIT = V7X_VMEM_BYTES - 8 * 1024 * 1024


class Cfg(NamedTuple):
    ch: int
    pc: int
    sb: int

    @property
    def n(self):
        return self.ch * (self.pc + self.sb)

    def seqs(self):
        out = [(0, self.pc * self.ch)]
        out += [((self.pc + b) * self.ch, self.ch) for b in range(self.sb)]
        return out


def _params(sem):
    return pltpu.CompilerParams(dimension_semantics=sem, vmem_limit_bytes=VMEM_LIMIT)


def _mod_body(c_ref, w_ref, b_ref, o_ref):
    c = c_ref[...]
    act = (c * jax.nn.sigmoid(c)).astype(BF16)
    o_ref[0] = jnp.dot(act, w_ref[0].astype(BF16), preferred_element_type=F32) + b_ref[0]


def _modulation(c_pad, ada_w, ada_b):
    depth, _, nout = ada_w.shape
    r = c_pad.shape[0]
    tn = 1024
    return pl.pallas_call(
        _mod_body,
        grid=(depth, nout // tn),
        in_specs=[pl.BlockSpec((r, D_MODEL), lambda l, j: (0, 0)),
                  pl.BlockSpec((1, D_MODEL, tn), lambda l, j: (l, 0, j)),
                  pl.BlockSpec((1, 1, tn), lambda l, j: (l, 0, j))],
        out_specs=pl.BlockSpec((1, r, tn), lambda l, j: (l, 0, j)),
        out_shape=jax.ShapeDtypeStruct((depth, r, nout), F32),
        compiler_params=_params(("arbitrary", "arbitrary")),
        name="modulation",
    )(c_pad, ada_w, ada_b.reshape(depth, 1, nout))


NORM_ROWS = 32


def _norm_rows(x_ref, g_ref, sh_ref, sc_ref, h_scr, tm):
    gain = g_ref[...] * (1.0 + sc_ref[0])
    shift = sh_ref[0]

    def body(r, carry):
        rows = pl.ds(pl.multiple_of(r * NORM_ROWS, NORM_ROWS), NORM_ROWS)
        x = x_ref[rows, :]
        inv = lax.rsqrt(jnp.mean(x * x, axis=-1, keepdims=True) + EPS)
        h_scr[rows, :] = ((x * inv) * gain + shift).astype(BF16)
        return carry
    lax.fori_loop(0, tm // NORM_ROWS, body, 0, unroll=4)


def _norm_mm_body(x_ref, g_ref, sh_ref, sc_ref, w_ref, cs_ref, o_ref, h_scr, *, tm):
    @pl.when(pl.program_id(1) == 0)
    def _():
        _norm_rows(x_ref, g_ref, sh_ref, sc_ref, h_scr, tm)
    acc = jnp.dot(h_scr[...], w_ref[...], preferred_element_type=F32)
    o_ref[...] = (acc * cs_ref[...]).astype(o_ref.dtype)


def _norm_mm_t_body(x_ref, g_ref, sh_ref, sc_ref, wt_ref, o_ref, h_scr, *, tm):
    _norm_rows(x_ref, g_ref, sh_ref, sc_ref, h_scr, tm)
    acc = lax.dot_general(wt_ref[...], h_scr[...], (((1,), (1,)), ((), ())),
                          preferred_element_type=F32)
    o_ref[...] = acc.astype(o_ref.dtype)


def _norm_swiglu_body(x_ref, g_ref, sh_ref, sc_ref, wg_ref, wu_ref, o_ref, h_scr, *, tm):
    @pl.when(pl.program_id(1) == 0)
    def _():
        _norm_rows(x_ref, g_ref, sh_ref, sc_ref, h_scr, tm)
    h = h_scr[...]
    gate = jnp.dot(h, wg_ref[...], preferred_element_type=F32)
    up = jnp.dot(h, wu_ref[...], preferred_element_type=F32)
    o_ref[...] = (gate * jax.nn.sigmoid(gate) * up).astype(o_ref.dtype)


def _mod_specs(cfg, tm, sub):
    per = cfg.ch // tm
    shift = pl.BlockSpec((1, 1, D_MODEL), lambda i, *_: ((i // per) * 9 + sub * 3, 0, 0))
    scale = pl.BlockSpec((1, 1, D_MODEL), lambda i, *_: ((i // per) * 9 + sub * 3 + 1, 0, 0))
    return shift, scale


def _norm_matmul(cfg, x, g, mods, sub, w, col_scale, out_dtype, tm, tn, name):
    n = x.shape[0]
    nout = w.shape[1]
    shift, scale = _mod_specs(cfg, tm, sub)
    return pl.pallas_call(
        functools.partial(_norm_mm_body, tm=tm),
        grid=(n // tm, nout // tn),
        in_specs=[pl.BlockSpec((tm, D_MODEL), lambda i, j: (i, 0)),
                  pl.BlockSpec((1, D_MODEL), lambda i, j: (0, 0)),
                  shift, scale,
                  pl.BlockSpec((D_MODEL, tn), lambda i, j: (0, j)),
                  pl.BlockSpec((1, tn), lambda i, j: (0, j))],
        out_specs=pl.BlockSpec((tm, tn), lambda i, j: (i, j)),
        out_shape=jax.ShapeDtypeStruct((n, nout), out_dtype),
        scratch_shapes=[pltpu.VMEM((tm, D_MODEL), BF16)],
        compiler_params=_params(("arbitrary", "arbitrary")),
        name=name,
    )(x, g.reshape(1, D_MODEL), mods, mods, w, col_scale)


def _norm_matmul_t(cfg, x, g, mods, sub, wt, tm, name):
    n = x.shape[0]
    nout = wt.shape[0]
    shift, scale = _mod_specs(cfg, tm, sub)
    return pl.pallas_call(
        functools.partial(_norm_mm_t_body, tm=tm),
        grid=(n // tm,),
        in_specs=[pl.BlockSpec((tm, D_MODEL), lambda i: (i, 0)),
                  pl.BlockSpec((1, D_MODEL), lambda i: (0, 0)),
                  shift, scale,
                  pl.BlockSpec((nout, D_MODEL), lambda i: (0, 0))],
        out_specs=pl.BlockSpec((nout, tm), lambda i: (0, i)),
        out_shape=jax.ShapeDtypeStruct((nout, n), BF16),
        scratch_shapes=[pltpu.VMEM((tm, D_MODEL), BF16)],
        compiler_params=_params(("arbitrary",)),
        name=name,
    )(x, g.reshape(1, D_MODEL), mods, mods, wt)


def _norm_swiglu(cfg, x, g, mods, sub, w_in, tm, tn):
    n = x.shape[0]
    nj = D_FF // tn
    shift, scale = _mod_specs(cfg, tm, sub)
    return pl.pallas_call(
        functools.partial(_norm_swiglu_body, tm=tm),
        grid=(n // tm, nj),
        in_specs=[pl.BlockSpec((tm, D_MODEL), lambda i, j: (i, 0)),
                  pl.BlockSpec((1, D_MODEL), lambda i, j: (0, 0)),
                  shift, scale,
                  pl.BlockSpec((D_MODEL, tn), lambda i, j: (0, j)),
                  pl.BlockSpec((D_MODEL, tn), lambda i, j: (0, j + nj))],
        out_specs=pl.BlockSpec((tm, tn), lambda i, j: (i, j)),
        out_shape=jax.ShapeDtypeStruct((n, D_FF), BF16),
        scratch_shapes=[pltpu.VMEM((tm, D_MODEL), BF16)],
        compiler_params=_params(("arbitrary", "arbitrary")),
        name="ffn_in",
    )(x, g.reshape(1, D_MODEL), mods, mods, w_in, w_in)


def _resid_mm_body(*refs, pieces, coef):
    lhs = refs[:pieces]
    ws = refs[pieces:2 * pieces]
    x_ref, gate_ref, o_ref = refs[2 * pieces:]
    acc = jnp.dot(lhs[0][...], ws[0][...], preferred_element_type=F32)
    for p in range(1, pieces):
        acc = acc + jnp.dot(lhs[p][...], ws[p][...], preferred_element_type=F32)
    o_ref[...] = x_ref[...] + (coef * gate_ref[0]) * acc


def _resid_matmul(cfg, lhs_list, w, x, mods, sub, coef, tm, tn, name):
    n = x.shape[0]
    per = cfg.ch // tm
    widths = [a.shape[1] for a in lhs_list]
    offs = np.cumsum([0] + widths[:-1])
    in_specs = [pl.BlockSpec((tm, k), lambda i, j: (i, 0)) for k in widths]
    for k, off in zip(widths, offs):
        assert off % k == 0
        in_specs.append(pl.BlockSpec((k, tn), lambda i, j, b=int(off // k): (b, j)))
    in_specs += [pl.BlockSpec((tm, tn), lambda i, j: (i, j)),
                 pl.BlockSpec((1, 1, tn), lambda i, j: ((i // per) * 9 + sub * 3 + 2, 0, j))]
    return pl.pallas_call(
        functools.partial(_resid_mm_body, pieces=len(lhs_list), coef=coef),
        grid=(n // tm, D_MODEL // tn),
        in_specs=in_specs,
        out_specs=pl.BlockSpec((tm, tn), lambda i, j: (i, j)),
        out_shape=jax.ShapeDtypeStruct((n, D_MODEL), F32),
        compiler_params=_params(("arbitrary", "arbitrary")),
        name=name,
    )(*lhs_list, *([w] * len(lhs_list)), x, mods)


def _final_norm_body(x_ref, g_ref, op_ref, os_ref, *, pblocks):
    x = x_ref[...]
    inv = lax.rsqrt(jnp.mean(x * x, axis=-1, keepdims=True) + EPS)
    y = x * inv * g_ref[...]
    i = pl.program_id(0)

    @pl.when(i < pblocks)
    def _():
        op_ref[...] = y

    @pl.when(i >= pblocks)
    def _():
        os_ref[...] = y


def _final_norm(cfg, x, g, tm=512):
    n = x.shape[0]
    npr = cfg.pc * cfg.ch
    pblocks = npr // tm
    return pl.pallas_call(
        functools.partial(_final_norm_body, pblocks=pblocks),
        grid=(n // tm,),
        in_specs=[pl.BlockSpec((tm, D_MODEL), lambda i: (i, 0)),
                  pl.BlockSpec((1, D_MODEL), lambda i: (0, 0))],
        out_specs=[pl.BlockSpec((tm, D_MODEL), lambda i: (jnp.minimum(i, pblocks - 1), 0)),
                   pl.BlockSpec((tm, D_MODEL), lambda i: (jnp.maximum(i - pblocks, 0), 0))],
        out_shape=[jax.ShapeDtypeStruct((npr, D_MODEL), F32),
                   jax.ShapeDtypeStruct((n - npr, D_MODEL), F32)],
        compiler_params=_params(("arbitrary",)),
        name="final_norm",
    )(x, g.reshape(1, D_MODEL))


def _t5_bucket_np(rel):
    nb = REL_BUCKETS // 2
    max_exact = nb // 2
    rel = np.asarray(rel, np.int64)
    base = np.where(rel > 0, nb, 0)
    n = np.abs(rel)
    nf = np.maximum(n, 1).astype(np.float64)
    large = max_exact + (np.log(nf / max_exact) / math.log(REL_MAX_DIST / max_exact)
                         * (nb - max_exact)).astype(np.int64)
    large = np.minimum(large, nb - 1)
    return (base + np.where(n < max_exact, n, large)).astype(np.int32)


EXPAND_ROWS = 32
EXPAND_UNROLL_MAX = 64


def _expand_body(tab_ref, idx_ref, o_ref, *, nb):
    t = pl.program_id(1)

    def chunk(r, carry):
        rows = pl.ds(pl.multiple_of(r * EXPAND_ROWS, EXPAND_ROWS), EXPAND_ROWS)
        idx = idx_ref[0, rows, :]
        pick = lambda b, acc: jnp.where(idx == b, tab_ref[t, b], acc)
        o_ref[0, 0, rows, :] = lax.fori_loop(0, nb, pick, jnp.zeros(idx.shape, F32),
                                             unroll=True if nb <= EXPAND_UNROLL_MAX else 8)
        return carry
    lax.fori_loop(0, idx_ref.shape[1] // EXPAND_ROWS, chunk, 0)


def _expand(tab, idx):
    ntab, nb = tab.shape
    ni, r, c = idx.shape
    assert r % EXPAND_ROWS == 0 and idx.min() >= 0 and idx.max() < nb
    return pl.pallas_call(
        functools.partial(_expand_body, nb=nb),
        grid=(ni, ntab),
        in_specs=[pl.BlockSpec(memory_space=pltpu.SMEM),
                  pl.BlockSpec((1, r, c), lambda i, t: (i, 0, 0))],
        out_specs=pl.BlockSpec((1, 1, r, c), lambda i, t: (i, t, 0, 0)),
        out_shape=jax.ShapeDtypeStruct((ni, ntab, r, c), F32),
        compiler_params=_params(("arbitrary", "arbitrary")),
        name="expand_table",
    )(tab, jnp.asarray(idx.astype(np.int32)))


def _banded_heads(heads, scores, finish):
    s_next = scores(0)
    for h in range(heads):
        s_cur = s_next
        if h + 1 < heads:
            s_next = scores(h + 1)
        finish(h, s_cur)


DILF_BQ = 256
DILF_GROUP = 4


def _dilf_flags(cfg):
    nchunks = cfg.pc + cfg.sb
    prev = np.array([1 if 0 < c < cfg.pc else 0 for c in range(nchunks)], np.int32)
    nxt = np.array([1 if c < cfg.pc - 1 else 0 for c in range(nchunks)], np.int32)
    return jnp.asarray(prev), jnp.asarray(nxt)


def _dilf_scores(q, k, bias, valid):
    s = lax.dot_general(q, k, (((1,), (1,)), ((), ())), preferred_element_type=F32)
    return jnp.where(valid, s + bias, NEG)


def _dilf_finish(s, v):
    m = jnp.max(s, axis=-1, keepdims=True)
    e = jnp.exp(s - m)
    den = jnp.sum(e, axis=-1, keepdims=True)
    o = jnp.dot(e.astype(BF16), v, preferred_element_type=F32)
    return o / den, m + jnp.log(den)


def _pipelined(tasks):
    s_next = tasks[0][0]()
    for t, (_, finish) in enumerate(tasks):
        s_cur = s_next
        if t + 1 < len(tasks):
            s_next = tasks[t + 1][0]()
        finish(s_cur)


def _dilf_valid(bq, lo, hi):
    nk = bq + 2 * A_HALF
    row = lax.broadcasted_iota(jnp.int32, (bq, nk), 0)
    col = lax.broadcasted_iota(jnp.int32, (bq, nk), 1)
    off = col - row
    return (off >= 0) & (off <= 2 * A_HALF) & (col >= lo) & (col < hi)


def _dilf_body(hp_ref, hn_ref,
               q1_ref, k1p_ref, k1_ref, k1n_ref, v1p_ref, v1_ref, v1n_ref,
               q2_ref, k2p_ref, k2_ref, k2n_ref, v2p_ref, v2_ref, v2n_ref,
               q3_ref, k3p_ref, k3_ref, k3n_ref, v3p_ref, v3_ref, v3n_ref,
               b1_ref, b2_ref, b3_ref, o_ref, oacc, lacc, *, ch):
    c = pl.program_id(0)
    lo0 = jnp.where(hp_ref[c] != 0, 0, A_HALF)
    hi_cut = jnp.where(hn_ref[c] != 0, 0, A_HALF)

    def block_valid(bq, first, last):
        nk = bq + 2 * A_HALF
        return _dilf_valid(bq, lo0 if first else 0, nk - hi_cut if last else nk)

    def keep(g, rows, o, lse):
        oacc[g, rows, :] = o
        lacc[g, rows, :] = jnp.broadcast_to(lse, o.shape)

    def task(g, q_rows, k, v, bias_ref, valid, out_rows):
        scores = lambda: _dilf_scores(q_rows(), k, bias_ref[0], valid)
        finish = lambda s: keep(g, out_rows, *_dilf_finish(s, v))
        return scores, finish

    bq = DILF_BQ
    tasks = []
    nblk = ch // bq
    k = jnp.concatenate([k1p_ref[...], k1_ref[...], k1n_ref[...]], axis=0)
    v = jnp.concatenate([v1p_ref[...], v1_ref[...], v1n_ref[...]], axis=0)
    for b in range(nblk):
        rows = slice(b * bq, (b + 1) * bq)
        keys = slice(b * bq, (b + 1) * bq + 2 * A_HALF)
        tasks.append(task(0, lambda rows=rows: q1_ref[rows, :], k[keys], v[keys], b1_ref,
                          block_valid(bq, b == 0, b == nblk - 1), rows))
    dil = A_CONFIGS[1][1]
    lc = ch // dil
    nblk = lc // bq
    for r in range(dil):
        sub = lambda ref, n: ref[pl.ds(r, n, stride=dil), :].astype(BF16)
        k = jnp.concatenate([sub(k2p_ref, A_HALF), sub(k2_ref, lc), sub(k2n_ref, A_HALF)], axis=0)
        v = jnp.concatenate([sub(v2p_ref, A_HALF), sub(v2_ref, lc), sub(v2n_ref, A_HALF)], axis=0)
        for b in range(nblk):
            rows = pl.ds(r + b * bq * dil, bq, stride=dil)
            keys = slice(b * bq, (b + 1) * bq + 2 * A_HALF)
            tasks.append(task(1, lambda rows=rows: q2_ref[rows, :].astype(BF16), k[keys], v[keys],
                              b2_ref, block_valid(bq, b == 0, b == nblk - 1), rows))
    _pipelined(tasks)

    dil3 = A_CONFIGS[2][1]
    lc3 = ch // dil3
    valid3 = block_valid(lc3, True, True)

    def residues(i, carry):
        group = []
        for u in range(DILF_GROUP):
            r = i * DILF_GROUP + u
            sub = lambda ref, n, r=r: ref[pl.ds(r, n, stride=dil3), :].astype(BF16)
            k = jnp.concatenate([sub(k3p_ref, A_HALF), sub(k3_ref, lc3), sub(k3n_ref, A_HALF)],
                                axis=0)
            v = jnp.concatenate([sub(v3p_ref, A_HALF), sub(v3_ref, lc3), sub(v3n_ref, A_HALF)],
                                axis=0)
            rows = pl.ds(r, lc3, stride=dil3)
            group.append(task(2, lambda rows=rows: q3_ref[rows, :].astype(BF16), k, v, b3_ref,
                              valid3, rows))
        _pipelined(group)
        return carry
    lax.fori_loop(0, dil3 // DILF_GROUP, residues, 0)

    def combine(i, carry):
        rows = pl.ds(pl.multiple_of(i * DILF_BQ, DILF_BQ), DILF_BQ)
        l0, l1, l2 = lacc[0, rows, :], lacc[1, rows, :], lacc[2, rows, :]
        m = jnp.maximum(jnp.maximum(l0, l1), l2)
        e0, e1, e2 = jnp.exp(l0 - m), jnp.exp(l1 - m), jnp.exp(l2 - m)
        tot = e0 + e1 + e2
        out = (e0 / tot) * oacc[0, rows, :] + (e1 / tot) * oacc[1, rows, :] \
            + (e2 / tot) * oacc[2, rows, :]
        o_ref[rows, :] = out.astype(o_ref.dtype)
        return carry
    lax.fori_loop(0, ch // DILF_BQ, combine, 0)


def _dilated_fused(cfg, p1, p23, biases):
    n = cfg.n
    ch = cfg.ch
    hp, hn = _dilf_flags(cfg)
    nh = A_HEADS
    assert ch // A_CONFIGS[2][1] == 2 * A_HALF and ch % DILF_BQ == 0

    def specs(dil, qb, kb, vb):
        halo = A_HALF * dil
        per = ch // halo
        last = n // halo - 1
        own = lambda cb: pl.BlockSpec((ch, HEAD_DIM), lambda c, h, p, x: (c, cb + h))
        before = lambda cb: pl.BlockSpec(
            (halo, HEAD_DIM), lambda c, h, p, x: (jnp.maximum(c * per - 1, 0), cb + h))
        after = lambda cb: pl.BlockSpec(
            (halo, HEAD_DIM), lambda c, h, p, x: (jnp.minimum((c + 1) * per, last), cb + h))
        return [own(qb), before(kb), own(kb), after(kb), before(vb), own(vb), after(vb)]

    bias_spec = lambda b: pl.BlockSpec((1,) + b.shape[1:], lambda c, h, p, x: (h, 0, 0))
    grid_spec = pltpu.PrefetchScalarGridSpec(
        num_scalar_prefetch=2,
        grid=(n // ch, nh),
        in_specs=(specs(1, 0, nh, 2 * nh) + specs(A_CONFIGS[1][1], 0, 2 * nh, 4 * nh)
                  + specs(A_CONFIGS[2][1], nh, 3 * nh, 5 * nh)
                  + [bias_spec(b) for b in biases]),
        out_specs=pl.BlockSpec((ch, HEAD_DIM), lambda c, h, p, x: (c, h)),
        scratch_shapes=[pltpu.VMEM((3, ch, HEAD_DIM), F32), pltpu.VMEM((3, ch, HEAD_DIM), F32)],
    )
    return pl.pallas_call(
        functools.partial(_dilf_body, ch=ch),
        grid_spec=grid_spec,
        out_shape=jax.ShapeDtypeStruct((n, nh * HEAD_DIM), BF16),
        compiler_params=_params(("arbitrary", "arbitrary")),
        name="dilated_fused",
    )(hp, hn, *([p1] * 7), *([p23] * 14), *biases)


def _dilf_bias(rel_bias, g, dil, bq):
    off = np.arange(bq + 2 * A_HALF)[None, :] - np.arange(bq)[:, None] - A_HALF
    bucket = _t5_bucket_np(dil * np.clip(off, -A_HALF, A_HALF))
    return _expand(rel_bias[:, g * A_HEADS:(g + 1) * A_HEADS].T, bucket[None])[0]


def _flash_worklist(cfg, tq, tk, rel_lo=None, rel_hi=None):
    qb, kb, tile, flags = [], [], [], []
    for start, length in cfg.seqs():
        assert length % tq == 0 and length % tk == 0 and start % tq == 0 and start % tk == 0
        nk = length // tk
        for qi in range(length // tq):
            for kj in range(nk):
                qb.append(start // tq + qi)
                kb.append(start // tk + kj)
                if rel_lo is not None:
                    assert (kj * tk) % tq == 0
                    d = (kj * tk - qi * tq) // tq
                    tile.append(min(max(d, rel_lo), rel_hi) - rel_lo)
                else:
                    tile.append(0)
                flags.append((1 if kj == 0 else 0) | (2 if kj == nk - 1 else 0))
    as_i32 = lambda a: jnp.asarray(np.asarray(a, np.int32))
    return as_i32(qb), as_i32(kb), as_i32(tile), as_i32(flags)


def _flash_init(flags, m_scr, acc_scr):
    @pl.when((flags & 1) != 0)
    def _():
        m_scr[...] = jnp.full(m_scr.shape, -jnp.inf, F32)
        acc_scr[...] = jnp.zeros(acc_scr.shape, F32)


def _flash_softmax(h, s, m_scr):
    m_prev = m_scr[h]
    m_new = jnp.maximum(m_prev, jnp.max(s, axis=0, keepdims=True))
    m_scr[h] = m_new
    return jnp.exp2(m_prev - m_new), jnp.exp2(s - m_new).astype(BF16)


def _flash_accumulate(h, alpha, p, vt, acc_scr):
    lhs = jnp.concatenate([vt, jnp.ones((ONES_ROWS, vt.shape[1]), BF16)], axis=0)
    acc_scr[h] = alpha * acc_scr[h] + jnp.dot(lhs, p, preferred_element_type=F32)


def _flash_heads(heads, scores, vt_rows, m_scr, acc_scr):
    s_next = scores(0)
    pending = None
    for h in range(heads):
        s_cur = s_next
        if h + 1 < heads:
            s_next = scores(h + 1)
        alpha, p = _flash_softmax(h, s_cur, m_scr)
        if pending is not None:
            _flash_accumulate(*pending, vt_rows(pending[0]), acc_scr)
        pending = (h, alpha, p)
    _flash_accumulate(*pending, vt_rows(pending[0]), acc_scr)


DIFF_TQ = 512
DIFF_TK = 512


def _diff_tile_range(max_len):
    rel = np.arange(-max_len + 1, max_len)
    b = _t5_bucket_np(rel)
    sat_pos = int(rel[b != b[-1]].max()) + 1
    sat_neg = int(rel[b != b[0]].min()) - 1
    hi = -(-(sat_pos + DIFF_TQ - 1) // DIFF_TQ)
    lo = (sat_neg - (DIFF_TK - 1)) // DIFF_TQ
    return lo, hi


def _diff_bias_tiles(rel_bias, lo, hi):
    d = np.arange(lo, hi + 1)[:, None, None] * DIFF_TQ
    rel = d + np.arange(DIFF_TK)[None, :, None] - np.arange(DIFF_TQ)[None, None, :]
    return _expand(rel_bias[:, 3 * A_HEADS:].T * LOG2E, _t5_bucket_np(rel))


def _diff_body(qb_ref, kb_ref, tile_ref, fl_ref, q_ref, k_ref, vt_ref, bias_ref, lam_ref, g_ref,
               o_ref, m_scr, acc_scr, *, lam_init):
    flags = fl_ref[pl.program_id(0)]
    tq = DIFF_TQ
    _flash_init(flags, m_scr, acc_scr)

    lane = lax.broadcasted_iota(jnp.int32, (tq, HEAD_DIM), 1)

    def scores(h):
        cols = slice(h * HEAD_DIM, (h + 1) * HEAD_DIM)
        qh = q_ref[:, cols]
        zero = jnp.zeros_like(qh)
        q2 = jnp.concatenate([jnp.where(lane < B_QK_DIM, qh, zero),
                              jnp.where(lane >= B_QK_DIM, qh, zero)], axis=0)
        s = lax.dot_general(k_ref[:, cols], q2, (((1,), (1,)), ((), ())),
                            preferred_element_type=F32)
        b = bias_ref[0, h]
        return s + jnp.concatenate([b, b], axis=1)

    vt_rows = lambda h: vt_ref[h * HEAD_DIM:(h + 1) * HEAD_DIM, :]
    _flash_heads(B_HEADS, scores, vt_rows, m_scr, acc_scr)

    @pl.when((flags & 2) != 0)
    def _():
        lf = lam_ref[...]
        lam = (jnp.exp(jnp.sum(lf[0:1] * lf[1:2], axis=-1, keepdims=True))
               - jnp.exp(jnp.sum(lf[2:3] * lf[3:4], axis=-1, keepdims=True)) + lam_init)
        for h in range(B_HEADS):
            cols = slice(h * HEAD_DIM, (h + 1) * HEAD_DIM)
            acc = acc_scr[h]
            att = acc[:HEAD_DIM] / acc[HEAD_DIM:HEAD_DIM + 1]
            o = att[:, :tq] - lam * att[:, tq:]
            inv = lax.rsqrt(jnp.mean(o * o, axis=0, keepdims=True) + EPS)
            o_ref[:, cols] = (((o * inv).T * g_ref[...]) * (1.0 - lam_init)).astype(o_ref.dtype)


def _diff_attention(cfg, proj, qcol, kcol, vt, bias_tiles, tile_lo, tile_hi, lam, subln_g,
                    lam_init):
    n = cfg.n
    tq, tk = DIFF_TQ, DIFF_TK
    qb, kb, tile, flags = _flash_worklist(cfg, tq, tk, tile_lo, tile_hi)
    grid_spec = pltpu.PrefetchScalarGridSpec(
        num_scalar_prefetch=4,
        grid=(qb.shape[0],),
        in_specs=[pl.BlockSpec((tq, B_W), lambda s, q, k, t, f: (q[s], qcol)),
                  pl.BlockSpec((tk, B_W), lambda s, q, k, t, f: (k[s], kcol)),
                  pl.BlockSpec((B_W, tk), lambda s, q, k, t, f: (0, k[s])),
                  pl.BlockSpec((1, B_HEADS, tk, tq), lambda s, q, k, t, f: (t[s], 0, 0, 0)),
                  pl.BlockSpec((4, B_QK_DIM), lambda s, q, k, t, f: (0, 0)),
                  pl.BlockSpec((1, HEAD_DIM), lambda s, q, k, t, f: (0, 0))],
        out_specs=pl.BlockSpec((tq, B_W), lambda s, q, k, t, f: (q[s], 0)),
        scratch_shapes=[pltpu.VMEM((B_HEADS, 1, 2 * tq), F32),
                        pltpu.VMEM((B_HEADS, HEAD_DIM + ONES_ROWS, 2 * tq), F32)],
    )
    return pl.pallas_call(
        functools.partial(_diff_body, lam_init=lam_init),
        grid_spec=grid_spec,
        out_shape=jax.ShapeDtypeStruct((n, B_W), BF16),
        compiler_params=_params(("arbitrary",)),
        name="diff_attention",
    )(qb, kb, tile, flags, proj, proj, vt, bias_tiles, lam, subln_g.reshape(1, HEAD_DIM))


MLA_TM = 512
MLA_TQ = 1024
MLA_TK = 512


def _rope_tables(max_len):
    inv = ROPE_THETA ** (-jnp.arange(0, C_ROPE, 2, dtype=F32) / C_ROPE)
    ang = jnp.arange(max_len, dtype=F32)[:, None] * inv[None, :]
    cos, sin = jnp.cos(ang), jnp.sin(ang)
    zero = jnp.zeros((max_len, HEAD_DIM - C_ROPE), F32)
    return (jnp.concatenate([cos, cos, zero], axis=1),
            jnp.concatenate([-sin, sin, zero], axis=1))


def _mla_prep_body(p_ref, gq_ref, gkv_ref, wqa_ref, wqb_ref, wk_ref, wvt_ref, cos_ref, sin_ref,
                   q_ref, k_ref, vt_ref, *, scale):
    def normed(x, g):
        inv = lax.rsqrt(jnp.mean(x * x, axis=-1, keepdims=True) + EPS)
        return (x * inv * g).astype(BF16)

    cq = normed(p_ref[:, 0:C_LORA], gq_ref[...])
    ckv = normed(p_ref[:, C_LORA:2 * C_LORA], gkv_ref[...])
    cos = cos_ref[...]
    sin = sin_ref[...]
    qa = jnp.dot(cq, wqa_ref[...], preferred_element_type=F32)
    qb = jnp.dot(cq, wqb_ref[...], preferred_element_type=F32)
    kn = jnp.dot(ckv, wk_ref[...], preferred_element_type=F32)
    vt_ref[...] = lax.dot_general(wvt_ref[...], ckv, (((1,), (1,)), ((), ())),
                                  preferred_element_type=F32).astype(BF16)
    kr = p_ref[:, 2 * C_LORA:2 * C_LORA + HEAD_DIM]
    kr_sw = p_ref[:, 2 * C_LORA + HEAD_DIM:2 * C_LORA + 2 * HEAD_DIM]
    k_rope = (kr * cos + kr_sw * sin).astype(BF16)
    for h in range(C_HEADS):
        a0 = h * C_QK_PAD
        a1 = a0 + HEAD_DIM
        a2 = a0 + C_QK_PAD
        b = slice(h * HEAD_DIM, (h + 1) * HEAD_DIM)
        q_ref[:, a0:a1] = (qa[:, a0:a1] * scale).astype(BF16)
        q_ref[:, a1:a2] = ((qa[:, a1:a2] * cos + qb[:, b] * sin) * scale).astype(BF16)
        k_ref[:, a0:a1] = kn[:, b].astype(BF16)
        k_ref[:, a1:a2] = k_rope


def _mla_prep(cfg, proj_c, gq, gkv, wqa, wqb, wk, wvt, cos_tab, sin_tab):
    n = cfg.n
    tm = MLA_TM
    pblocks = cfg.pc * cfg.ch // tm
    per = cfg.ch // tm
    pos = lambda i: (jnp.where(i < pblocks, i, (i - pblocks) % per), 0)
    full = lambda shape: pl.BlockSpec(shape, lambda i: (0, 0))
    qk_w = C_HEADS * C_QK_PAD
    v_w = C_HEADS * HEAD_DIM
    return pl.pallas_call(
        functools.partial(_mla_prep_body, scale=(C_NOPE + C_ROPE) ** -0.5 * LOG2E),
        grid=(n // tm,),
        in_specs=[pl.BlockSpec((tm, proj_c.shape[1]), lambda i: (i, 0)),
                  full((1, C_LORA)), full((1, C_LORA)),
                  full(wqa.shape), full(wqb.shape), full(wk.shape), full(wvt.shape),
                  pl.BlockSpec((tm, HEAD_DIM), pos), pl.BlockSpec((tm, HEAD_DIM), pos)],
        out_specs=[pl.BlockSpec((tm, qk_w), lambda i: (i, 0)),
                   pl.BlockSpec((tm, qk_w), lambda i: (i, 0)),
                   pl.BlockSpec((v_w, tm), lambda i: (0, i))],
        out_shape=[jax.ShapeDtypeStruct((n, qk_w), BF16),
                   jax.ShapeDtypeStruct((n, qk_w), BF16),
                   jax.ShapeDtypeStruct((v_w, n), BF16)],
        compiler_params=_params(("arbitrary",)),
        name="mla_prep",
    )(proj_c, gq.reshape(1, C_LORA), gkv.reshape(1, C_LORA), wqa, wqb, wk, wvt, cos_tab, sin_tab)


def _mla_body(qb_ref, kb_ref, tile_ref, fl_ref, q_ref, k_ref, vt_ref, o_ref, m_scr, acc_scr):
    flags = fl_ref[pl.program_id(0)]
    _flash_init(flags, m_scr, acc_scr)

    def scores(h):
        qk = slice(h * C_QK_PAD, (h + 1) * C_QK_PAD)
        return lax.dot_general(k_ref[:, qk], q_ref[:, qk], (((1,), (1,)), ((), ())),
                               preferred_element_type=F32)

    vt_rows = lambda h: vt_ref[h * HEAD_DIM:(h + 1) * HEAD_DIM, :]
    _flash_heads(C_HEADS, scores, vt_rows, m_scr, acc_scr)

    @pl.when((flags & 2) != 0)
    def _():
        for h in range(C_HEADS):
            vc = slice(h * HEAD_DIM, (h + 1) * HEAD_DIM)
            acc = acc_scr[h]
            o_ref[:, vc] = (acc[:HEAD_DIM] / acc[HEAD_DIM:HEAD_DIM + 1]).T.astype(o_ref.dtype)


def _mla_attention(cfg, q, k, vt):
    n = cfg.n
    tq, tk = MLA_TQ, MLA_TK
    qb, kb, tile, flags = _flash_worklist(cfg, tq, tk)
    qk_w = C_HEADS * C_QK_PAD
    v_w = C_HEADS * HEAD_DIM
    grid_spec = pltpu.PrefetchScalarGridSpec(
        num_scalar_prefetch=4,
        grid=(qb.shape[0],),
        in_specs=[pl.BlockSpec((tq, qk_w), lambda s, q_, k_, t, f: (q_[s], 0)),
                  pl.BlockSpec((tk, qk_w), lambda s, q_, k_, t, f: (k_[s], 0)),
                  pl.BlockSpec((v_w, tk), lambda s, q_, k_, t, f: (0, k_[s]))],
        out_specs=pl.BlockSpec((tq, v_w), lambda s, q_, k_, t, f: (q_[s], 0)),
        scratch_shapes=[pltpu.VMEM((C_HEADS, 1, tq), F32),
                        pltpu.VMEM((C_HEADS, HEAD_DIM + ONES_ROWS, tq), F32)],
    )
    return pl.pallas_call(
        _mla_body,
        grid_spec=grid_spec,
        out_shape=jax.ShapeDtypeStruct((n, v_w), BF16),
        compiler_params=_params(("arbitrary",)),
        name="mla_attention",
    )(qb, kb, tile, flags, q, k, vt)


def _na_tables(cfg):
    prev, nxt, var = [], [], []
    for start, length in cfg.seqs():
        nb = length // NA_BLOCK
        assert length % NA_BLOCK == 0 and start % NA_BLOCK == 0 and nb >= 3
        b0 = start // NA_BLOCK
        for r in range(nb):
            prev.append(b0 + max(r - 1, 0))
            nxt.append(b0 + min(r + 1, nb - 1))
            var.append(0 if r == 0 else (2 if r == nb - 1 else 1))
    as_i32 = lambda a: jnp.asarray(np.asarray(a, np.int32))
    return as_i32(prev), as_i32(nxt), as_i32(var)


def _na_window_start(a, variant):
    centred = a - NA_ROWS // 2
    return (max(centred, 0), centred, min(centred, 0))[variant]


def _na_bias_tables(rpb):
    ncol = 2 * NA_COLS - 1
    c = np.arange(GRID_W)[:, None, None]
    b = np.arange(NA_ROWS)[None, :, None]
    kc = np.arange(GRID_W)[None, None, :]
    cstart = np.clip(c - NA_COLS // 2, 0, GRID_W - NA_COLS)
    col_ok = (kc >= cstart) & (kc < cstart + NA_COLS)
    dc = np.clip(kc - c, -(NA_COLS - 1), NA_COLS - 1) + (NA_COLS - 1)
    masked = NA_ROWS * ncol
    idx = np.where(col_ok, b * ncol + dc, masked).reshape(1, GRID_W, NA_BLOCK)
    rows = np.arange(NA_ROWS)[:, None] + np.arange(NA_ROWS)[None, :]
    tab = rpb[:, rows, :].reshape(D_HEADS * NA_ROWS, NA_ROWS * ncol)
    tab = jnp.concatenate([tab, jnp.full((tab.shape[0], 1), NEG, F32)], axis=1)
    slabs = _expand(tab, idx)[0].reshape(D_HEADS, NA_ROWS, GRID_W, NA_BLOCK)
    variants = []
    for variant in range(3):
        row_blocks = []
        for a in range(NA_ROWS):
            start = _na_window_start(a, variant)
            left = (NA_ROWS + start) * GRID_W
            row_blocks.append(jnp.pad(slabs[:, start - a + NA_ROWS - 1],
                                      ((0, 0), (0, 0), (left, 2 * NA_BLOCK - left)),
                                      constant_values=NEG))
        variants.append(jnp.concatenate(row_blocks, axis=1))
    return jnp.stack(variants, axis=0)


def _na_body(prev_ref, nxt_ref, var_ref, q_ref, kp_ref, km_ref, kn_ref, vp_ref, vm_ref, vn_ref,
             tab_ref, o_ref):
    k = jnp.concatenate([kp_ref[...], km_ref[...], kn_ref[...]], axis=0)
    v = jnp.concatenate([vp_ref[...], vm_ref[...], vn_ref[...]], axis=0)
    def scores(h):
        cols = slice(h * HEAD_DIM, (h + 1) * HEAD_DIM)
        s = lax.dot_general(q_ref[:, cols], k[:, cols], (((1,), (1,)), ((), ())),
                            preferred_element_type=F32)
        return s + tab_ref[0, h]

    def finish(h, s):
        cols = slice(h * HEAD_DIM, (h + 1) * HEAD_DIM)
        m = jnp.max(s, axis=-1, keepdims=True)
        e = jnp.exp(s - m)
        den = jnp.sum(e, axis=-1, keepdims=True)
        o = jnp.dot(e.astype(BF16), v[:, cols], preferred_element_type=F32)
        o_ref[:, cols] = (o / den).astype(o_ref.dtype)

    _banded_heads(D_HEADS, scores, finish)


def _na_attention(cfg, qkv, tabs):
    n = cfg.n
    w = D_HEADS * HEAD_DIM
    prev, nxt, var = _na_tables(cfg)
    own = lambda cb: pl.BlockSpec((NA_BLOCK, w), lambda i, p, x, t: (i, cb))
    before = lambda cb: pl.BlockSpec((NA_BLOCK, w), lambda i, p, x, t: (p[i], cb))
    after = lambda cb: pl.BlockSpec((NA_BLOCK, w), lambda i, p, x, t: (x[i], cb))
    grid_spec = pltpu.PrefetchScalarGridSpec(
        num_scalar_prefetch=3,
        grid=(n // NA_BLOCK,),
        in_specs=[own(0), before(1), own(1), after(1), before(2), own(2), after(2),
                  pl.BlockSpec((1, D_HEADS, NA_BLOCK, 3 * NA_BLOCK),
                               lambda i, p, x, t: (t[i], 0, 0, 0))],
        out_specs=pl.BlockSpec((NA_BLOCK, w), lambda i, p, x, t: (i, 0)),
    )
    return pl.pallas_call(
        _na_body,
        grid_spec=grid_spec,
        out_shape=jax.ShapeDtypeStruct((n, w), BF16),
        compiler_params=_params(("arbitrary",)),
        name="na_attention",
    )(prev, nxt, var, qkv, qkv, qkv, qkv, qkv, qkv, qkv, tabs)


EVEN_BLOCKS_ROWMAJOR = (0, 3, 6, 9, 10)
EVEN_BLOCKS_STRIDED = (1, 2, 4, 5, 7, 8)
EVEN_BLOCK_VT = 11


def _even_weights(w_in):
    blk = lambda b: w_in[:, b * B_W:(b + 1) * B_W]
    w_a = jnp.concatenate([blk(b) for b in EVEN_BLOCKS_ROWMAJOR], axis=1)
    w_b = jnp.concatenate([blk(b) for b in EVEN_BLOCKS_STRIDED], axis=1)
    return w_a, w_b, blk(EVEN_BLOCK_VT).T


def _even_col_scales():
    cs_a = np.ones((1, len(EVEN_BLOCKS_ROWMAJOR) * B_W), np.float32)
    cs_a[0, :B_W] = HEAD_DIM ** -0.5
    cs_a[0, 3 * B_W:4 * B_W] = B_QK_DIM ** -0.5 * LOG2E
    cs_b = np.ones((1, len(EVEN_BLOCKS_STRIDED) * B_W), np.float32)
    cs_b[0, :2 * B_W] = HEAD_DIM ** -0.5
    return jnp.asarray(cs_a), jnp.asarray(cs_b)


def _even_tables(cfg, rel_bias):
    dil = [_dilf_bias(rel_bias, gi, d, min(DILF_BQ, cfg.ch // d))
           for gi, (_, d) in enumerate(A_CONFIGS)]
    tile_lo, tile_hi = _diff_tile_range(cfg.pc * cfg.ch)
    return dil, _diff_bias_tiles(rel_bias, tile_lo, tile_hi), tile_lo, tile_hi


def _even_mixer(cfg, x, g, mods, weights, w_out, lam, subln_g, tables, lam_init, tm):
    w_a, w_b, wvt = weights
    dil_bias, diff_tiles, tile_lo, tile_hi = tables
    cs_a, cs_b = _even_col_scales()
    p1 = _norm_matmul(cfg, x, g, mods, 1, w_a, cs_a, BF16, tm, 1024, "even_in_a")
    p23 = _norm_matmul(cfg, x, g, mods, 1, w_b, cs_b, F32, tm, 1024, "even_in_b")
    vt = _norm_matmul_t(cfg, x, g, mods, 1, wvt, tm, "even_in_vt")
    o_a = _dilated_fused(cfg, p1, p23, dil_bias)
    o_b = _diff_attention(cfg, p1, 3, 4, vt, diff_tiles, tile_lo, tile_hi, lam, subln_g, lam_init)
    return _resid_matmul(cfg, [o_a, o_b], w_out, x, mods, 1, 1.0, tm, 1024, "even_out")


def _odd_weights(w_in, w_q_up, w_kv_up):
    half = C_ROPE // 2
    swap = np.concatenate([np.arange(half, C_ROPE), np.arange(half)])
    o2 = 2 * C_LORA
    zpad = jnp.zeros((D_MODEL, HEAD_DIM - C_ROPE), w_in.dtype)
    kr = w_in[:, o2:o2 + C_ROPE]
    w_c = jnp.concatenate([w_in[:, :o2], kr, zpad, kr[:, swap], zpad], axis=1)
    w_d = w_in[:, o2 + C_ROPE:]
    q3 = w_q_up.reshape(C_LORA, C_HEADS, C_NOPE + C_ROPE)
    zq = jnp.zeros((C_LORA, C_HEADS, C_QK_PAD - C_NOPE - C_ROPE), w_q_up.dtype)
    wqa = jnp.concatenate([q3, zq], axis=2).reshape(C_LORA, C_HEADS * C_QK_PAD)
    wqb = jnp.concatenate([q3[:, :, C_NOPE:][:, :, swap], zq], axis=2).reshape(
        C_LORA, C_HEADS * HEAD_DIM)
    kv3 = w_kv_up.reshape(C_LORA, C_HEADS, 2 * HEAD_DIM)
    wk = kv3[:, :, :C_NOPE].reshape(C_LORA, C_HEADS * HEAD_DIM)
    wvt = kv3[:, :, C_NOPE:].reshape(C_LORA, C_HEADS * HEAD_DIM).T
    return w_c, w_d, wqa, wqb, wk, wvt


def _odd_mixer(cfg, x, g, mods, weights, gq, gkv, rpb, w_out, rope, tm):
    w_c, w_d, wqa, wqb, wk, wvt = weights
    ones_c = jnp.ones((1, w_c.shape[1]), F32)
    proj_c = _norm_matmul(cfg, x, g, mods, 1, w_c, ones_c, F32, tm, w_c.shape[1], "odd_in_latent")
    cs = np.ones((1, w_d.shape[1]), np.float32)
    cs[0, :D_HEADS * HEAD_DIM] = HEAD_DIM ** -0.5
    qkv_d = _norm_matmul(cfg, x, g, mods, 1, w_d, jnp.asarray(cs), BF16, tm, w_d.shape[1],
                         "odd_in_na")
    q, k, vt = _mla_prep(cfg, proj_c, gq, gkv, wqa, wqb, wk, wvt, *rope)
    o_c = _mla_attention(cfg, q, k, vt)
    o_d = _na_attention(cfg, qkv_d, _na_bias_tables(rpb))
    return _resid_matmul(cfg, [o_c, o_d], w_out, x, mods, 1, 1.0, tm, 1024, "odd_out")


def _trunk(cfg, x, c_pad, ada_w, ada_b, norm_g, ffn_w_in, ffn_w_out, rel_bias, ev_w_in, ev_w_out,
           diff_lambda, diff_subln_g, od_w_in, mla_q_norm_g, mla_kv_norm_g, mla_w_q_up,
           mla_w_kv_up, na_rpb, od_w_out, final_norm_g):
    depth = ada_w.shape[0]
    tm = 1024
    nseq = 1 + cfg.sb
    mod_all = _modulation(c_pad, ada_w, ada_b)[:, :nseq]
    chunk_seq = np.array([0] * cfg.pc + list(range(1, nseq)))
    mod_all = mod_all[:, chunk_seq].reshape(depth, (cfg.pc + cfg.sb) * 9, 1, D_MODEL)
    rope = _rope_tables(cfg.pc * cfg.ch)
    even_tables = _even_tables(cfg, rel_bias)
    for i in range(depth):
        mods = mod_all[i]
        j = i // 2
        hid = _norm_swiglu(cfg, x, norm_g[i, 0], mods, 0, ffn_w_in[i, 0].astype(BF16), tm, 512)
        x = _resid_matmul(cfg, [hid], ffn_w_out[i, 0].astype(BF16), x, mods, 0, 0.5, tm, 512,
                          "ffn_out")
        if i % 2 == 0:
            x = _even_mixer(cfg, x, norm_g[i, 1], mods, _even_weights(ev_w_in[j].astype(BF16)),
                            ev_w_out[j].astype(BF16), diff_lambda[j], diff_subln_g[j], even_tables,
                            0.8 - 0.6 * math.exp(-0.3 * i), tm)
        else:
            weights = tuple(a.astype(BF16) for a in
                            _odd_weights(od_w_in[j], mla_w_q_up[j], mla_w_kv_up[j]))
            x = _odd_mixer(cfg, x, norm_g[i, 1], mods, weights, mla_q_norm_g[j], mla_kv_norm_g[j],
                           na_rpb[j], od_w_out[j].astype(BF16), rope, tm)
        hid = _norm_swiglu(cfg, x, norm_g[i, 2], mods, 2, ffn_w_in[i, 1].astype(BF16), tm, 512)
        x = _resid_matmul(cfg, [hid], ffn_w_out[i, 1].astype(BF16), x, mods, 2, 0.5, tm, 512,
                          "ffn_out")
    return _final_norm(cfg, x, final_norm_g)


def kernel(x_prompt, x_sample, c_prompt, c_sample, ada_w, ada_b, norm_g, ffn_w_in, ffn_w_out, rel_bias, ev_w_in, ev_w_out, diff_lambda, diff_subln_g, od_w_in, mla_q_norm_g, mla_kv_norm_g, mla_w_q_up, mla_w_kv_up, na_rpb, od_w_out, final_norm_g):
    pb, pt, _ = x_prompt.shape
    sb, st, _ = x_sample.shape
    assert pb == 1 and pt % st == 0
    cfg = Cfg(ch=st, pc=pt // st, sb=sb)
    x = jnp.concatenate([x_prompt.reshape(-1, D_MODEL), x_sample.reshape(-1, D_MODEL)], axis=0)
    c = jnp.concatenate([c_prompt, c_sample], axis=0)
    c_pad = jnp.pad(c, ((0, -c.shape[0] % 8), (0, 0)))
    y_p, y_s = _trunk(cfg, x, c_pad, ada_w, ada_b, norm_g, ffn_w_in, ffn_w_out, rel_bias, ev_w_in,
               ev_w_out, diff_lambda, diff_subln_g, od_w_in, mla_q_norm_g, mla_kv_norm_g,
               mla_w_q_up, mla_w_kv_up, na_rpb, od_w_out, final_norm_g)
    return (y_p.reshape(x_prompt.shape), y_s.reshape(x_sample.shape))
```

```python
import functools
import math
from typing import NamedTuple

import numpy as np
import jax
import jax.numpy as jnp
from jax import lax
from jax.experimental import pallas as pl
from jax.experimental.pallas import tpu as pltpu

F32 = jnp.float32
BF16 = jnp.bfloat16

D_MODEL = 2048
D_FF = 5632
HEAD_DIM = 128
A_HEADS = 8
A_CONFIGS = ((128, 1), (512, 4), (2048, 16))
A_HALF = 64
A_IN = 3 * 3 * A_HEADS * HEAD_DIM
B_HEADS = 8
B_QK_DIM = 64
B_W = B_HEADS * 2 * B_QK_DIM
EVEN_IN = A_IN + 3 * B_W
C_HEADS = 12
C_LORA = 512
C_NOPE = 128
C_ROPE = 64
C_QK_PAD = 256
ROPE_THETA = 10000.0
D_HEADS = 4
GRID_W = 64
NA_ROWS = 8
NA_COLS = 16
NA_BLOCK = NA_ROWS * GRID_W
REL_BUCKETS = 32
REL_MAX_DIST = 1024
EPS = 1e-6
NEG = -1e30
LOG2E = math.log2(math.e)
ONES_ROWS = 16

V7X_VMEM_BYTES = 64 * 1024 * 1024
VMEM_LIMIT = V7X_VMEM_BYTES - 8 * 1024 * 1024


class Cfg(NamedTuple):
    ch: int
    pc: int
    sb: int

    @property
    def n(self):
        return self.ch * (self.pc + self.sb)

    def seqs(self):
        out = [(0, self.pc * self.ch)]
        out += [((self.pc + b) * self.ch, self.ch) for b in range(self.sb)]
        return out


def _params(sem):
    return pltpu.CompilerParams(dimension_semantics=sem, vmem_limit_bytes=VMEM_LIMIT)


def _mod_body(c_ref, w_ref, b_ref, o_ref):
    c = c_ref[...]
    act = (c * jax.nn.sigmoid(c)).astype(BF16)
    o_ref[0] = jnp.dot(act, w_ref[0].astype(BF16), preferred_element_type=F32) + b_ref[0]


def _modulation(c_pad, ada_w, ada_b):
    depth, _, nout = ada_w.shape
    r = c_pad.shape[0]
    tn = 1024
    return pl.pallas_call(
        _mod_body,
        grid=(depth, nout // tn),
        in_specs=[pl.BlockSpec((r, D_MODEL), lambda l, j: (0, 0)),
                  pl.BlockSpec((1, D_MODEL, tn), lambda l, j: (l, 0, j)),
                  pl.BlockSpec((1, 1, tn), lambda l, j: (l, 0, j))],
        out_specs=pl.BlockSpec((1, r, tn), lambda l, j: (l, 0, j)),
        out_shape=jax.ShapeDtypeStruct((depth, r, nout), F32),
        compiler_params=_params(("arbitrary", "arbitrary")),
        name="modulation",
    )(c_pad, ada_w, ada_b.reshape(depth, 1, nout))


NORM_ROWS = 32


def _norm_rows(x_ref, g_ref, sh_ref, sc_ref, h_scr, tm):
    gain = g_ref[...] * (1.0 + sc_ref[0])
    shift = sh_ref[0]

    def body(r, carry):
        rows = pl.ds(pl.multiple_of(r * NORM_ROWS, NORM_ROWS), NORM_ROWS)
        x = x_ref[rows, :]
        inv = lax.rsqrt(jnp.mean(x * x, axis=-1, keepdims=True) + EPS)
        h_scr[rows, :] = ((x * inv) * gain + shift).astype(BF16)
        return carry
    lax.fori_loop(0, tm // NORM_ROWS, body, 0, unroll=4)


def _norm_mm_body(x_ref, g_ref, sh_ref, sc_ref, w_ref, cs_ref, o_ref, h_scr, *, tm):
    @pl.when(pl.program_id(1) == 0)
    def _():
        _norm_rows(x_ref, g_ref, sh_ref, sc_ref, h_scr, tm)
    acc = jnp.dot(h_scr[...], w_ref[...], preferred_element_type=F32)
    o_ref[...] = (acc * cs_ref[...]).astype(o_ref.dtype)


def _norm_mm_t_body(x_ref, g_ref, sh_ref, sc_ref, wt_ref, o_ref, h_scr, *, tm):
    _norm_rows(x_ref, g_ref, sh_ref, sc_ref, h_scr, tm)
    acc = lax.dot_general(wt_ref[...], h_scr[...], (((1,), (1,)), ((), ())),
                          preferred_element_type=F32)
    o_ref[...] = acc.astype(o_ref.dtype)


def _norm_swiglu_body(x_ref, g_ref, sh_ref, sc_ref, wg_ref, wu_ref, o_ref, h_scr, *, tm):
    @pl.when(pl.program_id(1) == 0)
    def _():
        _norm_rows(x_ref, g_ref, sh_ref, sc_ref, h_scr, tm)
    h = h_scr[...]
    gate = jnp.dot(h, wg_ref[...], preferred_element_type=F32)
    up = jnp.dot(h, wu_ref[...], preferred_element_type=F32)
    o_ref[...] = (gate * jax.nn.sigmoid(gate) * up).astype(o_ref.dtype)


def _mod_specs(cfg, tm, sub):
    per = cfg.ch // tm
    shift = pl.BlockSpec((1, 1, D_MODEL), lambda i, *_: ((i // per) * 9 + sub * 3, 0, 0))
    scale = pl.BlockSpec((1, 1, D_MODEL), lambda i, *_: ((i // per) * 9 + sub * 3 + 1, 0, 0))
    return shift, scale


def _norm_matmul(cfg, x, g, mods, sub, w, col_scale, out_dtype, tm, tn, name):
    n = x.shape[0]
    nout = w.shape[1]
    shift, scale = _mod_specs(cfg, tm, sub)
    return pl.pallas_call(
        functools.partial(_norm_mm_body, tm=tm),
        grid=(n // tm, nout // tn),
        in_specs=[pl.BlockSpec((tm, D_MODEL), lambda i, j: (i, 0)),
                  pl.BlockSpec((1, D_MODEL), lambda i, j: (0, 0)),
                  shift, scale,
                  pl.BlockSpec((D_MODEL, tn), lambda i, j: (0, j)),
                  pl.BlockSpec((1, tn), lambda i, j: (0, j))],
        out_specs=pl.BlockSpec((tm, tn), lambda i, j: (i, j)),
        out_shape=jax.ShapeDtypeStruct((n, nout), out_dtype),
        scratch_shapes=[pltpu.VMEM((tm, D_MODEL), BF16)],
        compiler_params=_params(("arbitrary", "arbitrary")),
        name=name,
    )(x, g.reshape(1, D_MODEL), mods, mods, w, col_scale)


def _norm_matmul_t(cfg, x, g, mods, sub, wt, tm, name):
    n = x.shape[0]
    nout = wt.shape[0]
    shift, scale = _mod_specs(cfg, tm, sub)
    return pl.pallas_call(
        functools.partial(_norm_mm_t_body, tm=tm),
        grid=(n // tm,),
        in_specs=[pl.BlockSpec((tm, D_MODEL), lambda i: (i, 0)),
                  pl.BlockSpec((1, D_MODEL), lambda i: (0, 0)),
                  shift, scale,
                  pl.BlockSpec((nout, D_MODEL), lambda i: (0, 0))],
        out_specs=pl.BlockSpec((nout, tm), lambda i: (0, i)),
        out_shape=jax.ShapeDtypeStruct((nout, n), BF16),
        scratch_shapes=[pltpu.VMEM((tm, D_MODEL), BF16)],
        compiler_params=_params(("arbitrary",)),
        name=name,
    )(x, g.reshape(1, D_MODEL), mods, mods, wt)


def _norm_swiglu(cfg, x, g, mods, sub, w_in, tm, tn):
    n = x.shape[0]
    nj = D_FF // tn
    shift, scale = _mod_specs(cfg, tm, sub)
    return pl.pallas_call(
        functools.partial(_norm_swiglu_body, tm=tm),
        grid=(n // tm, nj),
        in_specs=[pl.BlockSpec((tm, D_MODEL), lambda i, j: (i, 0)),
                  pl.BlockSpec((1, D_MODEL), lambda i, j: (0, 0)),
                  shift, scale,
                  pl.BlockSpec((D_MODEL, tn), lambda i, j: (0, j)),
                  pl.BlockSpec((D_MODEL, tn), lambda i, j: (0, j + nj))],
        out_specs=pl.BlockSpec((tm, tn), lambda i, j: (i, j)),
        out_shape=jax.ShapeDtypeStruct((n, D_FF), BF16),
        scratch_shapes=[pltpu.VMEM((tm, D_MODEL), BF16)],
        compiler_params=_params(("arbitrary", "arbitrary")),
        name="ffn_in",
    )(x, g.reshape(1, D_MODEL), mods, mods, w_in, w_in)


def _resid_mm_body(*refs, pieces, coef):
    lhs = refs[:pieces]
    ws = refs[pieces:2 * pieces]
    x_ref, gate_ref, o_ref = refs[2 * pieces:]
    acc = jnp.dot(lhs[0][...], ws[0][...], preferred_element_type=F32)
    for p in range(1, pieces):
        acc = acc + jnp.dot(lhs[p][...], ws[p][...], preferred_element_type=F32)
    o_ref[...] = x_ref[...] + (coef * gate_ref[0]) * acc


def _resid_matmul(cfg, lhs_list, w, x, mods, sub, coef, tm, tn, name):
    n = x.shape[0]
    per = cfg.ch // tm
    widths = [a.shape[1] for a in lhs_list]
    offs = np.cumsum([0] + widths[:-1])
    in_specs = [pl.BlockSpec((tm, k), lambda i, j: (i, 0)) for k in widths]
    for k, off in zip(widths, offs):
        assert off % k == 0
        in_specs.append(pl.BlockSpec((k, tn), lambda i, j, b=int(off // k): (b, j)))
    in_specs += [pl.BlockSpec((tm, tn), lambda i, j: (i, j)),
                 pl.BlockSpec((1, 1, tn), lambda i, j: ((i // per) * 9 + sub * 3 + 2, 0, j))]
    return pl.pallas_call(
        functools.partial(_resid_mm_body, pieces=len(lhs_list), coef=coef),
        grid=(n // tm, D_MODEL // tn),
        in_specs=in_specs,
        out_specs=pl.BlockSpec((tm, tn), lambda i, j: (i, j)),
        out_shape=jax.ShapeDtypeStruct((n, D_MODEL), F32),
        compiler_params=_params(("arbitrary", "arbitrary")),
        name=name,
    )(*lhs_list, *([w] * len(lhs_list)), x, mods)


def _final_norm_body(x_ref, g_ref, op_ref, os_ref, *, pblocks):
    x = x_ref[...]
    inv = lax.rsqrt(jnp.mean(x * x, axis=-1, keepdims=True) + EPS)
    y = x * inv * g_ref[...]
    i = pl.program_id(0)

    @pl.when(i < pblocks)
    def _():
        op_ref[...] = y

    @pl.when(i >= pblocks)
    def _():
        os_ref[...] = y


def _final_norm(cfg, x, g, tm=512):
    n = x.shape[0]
    npr = cfg.pc * cfg.ch
    pblocks = npr // tm
    return pl.pallas_call(
        functools.partial(_final_norm_body, pblocks=pblocks),
        grid=(n // tm,),
        in_specs=[pl.BlockSpec((tm, D_MODEL), lambda i: (i, 0)),
                  pl.BlockSpec((1, D_MODEL), lambda i: (0, 0))],
        out_specs=[pl.BlockSpec((tm, D_MODEL), lambda i: (jnp.minimum(i, pblocks - 1), 0)),
                   pl.BlockSpec((tm, D_MODEL), lambda i: (jnp.maximum(i - pblocks, 0), 0))],
        out_shape=[jax.ShapeDtypeStruct((npr, D_MODEL), F32),
                   jax.ShapeDtypeStruct((n - npr, D_MODEL), F32)],
        compiler_params=_params(("arbitrary",)),
        name="final_norm",
    )(x, g.reshape(1, D_MODEL))


def _t5_bucket_np(rel):
    nb = REL_BUCKETS // 2
    max_exact = nb // 2
    rel = np.asarray(rel, np.int64)
    base = np.where(rel > 0, nb, 0)
    n = np.abs(rel)
    nf = np.maximum(n, 1).astype(np.float64)
    large = max_exact + (np.log(nf / max_exact) / math.log(REL_MAX_DIST / max_exact)
                         * (nb - max_exact)).astype(np.int64)
    large = np.minimum(large, nb - 1)
    return (base + np.where(n < max_exact, n, large)).astype(np.int32)


EXPAND_ROWS = 32
EXPAND_UNROLL_MAX = 64


def _expand_body(tab_ref, idx_ref, o_ref, *, nb):
    t = pl.program_id(1)

    def chunk(r, carry):
        rows = pl.ds(pl.multiple_of(r * EXPAND_ROWS, EXPAND_ROWS), EXPAND_ROWS)
        idx = idx_ref[0, rows, :]
        pick = lambda b, acc: jnp.where(idx == b, tab_ref[t, b], acc)
        o_ref[0, 0, rows, :] = lax.fori_loop(0, nb, pick, jnp.zeros(idx.shape, F32),
                                             unroll=True if nb <= EXPAND_UNROLL_MAX else 8)
        return carry
    lax.fori_loop(0, idx_ref.shape[1] // EXPAND_ROWS, chunk, 0)


def _expand(tab, idx):
    ntab, nb = tab.shape
    ni, r, c = idx.shape
    assert r % EXPAND_ROWS == 0 and idx.min() >= 0 and idx.max() < nb
    return pl.pallas_call(
        functools.partial(_expand_body, nb=nb),
        grid=(ni, ntab),
        in_specs=[pl.BlockSpec(memory_space=pltpu.SMEM),
                  pl.BlockSpec((1, r, c), lambda i, t: (i, 0, 0))],
        out_specs=pl.BlockSpec((1, 1, r, c), lambda i, t: (i, t, 0, 0)),
        out_shape=jax.ShapeDtypeStruct((ni, ntab, r, c), F32),
        compiler_params=_params(("arbitrary", "arbitrary")),
        name="expand_table",
    )(tab, jnp.asarray(idx.astype(np.int32)))


def _banded_heads(heads, scores, finish):
    s_next = scores(0)
    for h in range(heads):
        s_cur = s_next
        if h + 1 < heads:
            s_next = scores(h + 1)
        finish(h, s_cur)


DILF_BQ = 256
DILF_GROUP = 4


def _dilf_flags(cfg):
    nchunks = cfg.pc + cfg.sb
    prev = np.array([1 if 0 < c < cfg.pc else 0 for c in range(nchunks)], np.int32)
    nxt = np.array([1 if c < cfg.pc - 1 else 0 for c in range(nchunks)], np.int32)
    return jnp.asarray(prev), jnp.asarray(nxt)


def _dilf_scores(q, k, bias, valid):
    s = lax.dot_general(q, k, (((1,), (1,)), ((), ())), preferred_element_type=F32)
    return jnp.where(valid, s + bias, NEG)


def _dilf_finish(s, v):
    m = jnp.max(s, axis=-1, keepdims=True)
    e = jnp.exp(s - m)
    den = jnp.sum(e, axis=-1, keepdims=True)
    o = jnp.dot(e.astype(BF16), v, preferred_element_type=F32)
    return o / den, m + jnp.log(den)


def _pipelined(tasks):
    s_next = tasks[0][0]()
    for t, (_, finish) in enumerate(tasks):
        s_cur = s_next
        if t + 1 < len(tasks):
            s_next = tasks[t + 1][0]()
        finish(s_cur)


def _dilf_valid(bq, lo, hi):
    nk = bq + 2 * A_HALF
    row = lax.broadcasted_iota(jnp.int32, (bq, nk), 0)
    col = lax.broadcasted_iota(jnp.int32, (bq, nk), 1)
    off = col - row
    return (off >= 0) & (off <= 2 * A_HALF) & (col >= lo) & (col < hi)


def _dilf_body(hp_ref, hn_ref,
               q1_ref, k1p_ref, k1_ref, k1n_ref, v1p_ref, v1_ref, v1n_ref,
               q2_ref, k2p_ref, k2_ref, k2n_ref, v2p_ref, v2_ref, v2n_ref,
               q3_ref, k3p_ref, k3_ref, k3n_ref, v3p_ref, v3_ref, v3n_ref,
               b1_ref, b2_ref, b3_ref, o_ref, oacc, lacc, *, ch):
    c = pl.program_id(0)
    lo0 = jnp.where(hp_ref[c] != 0, 0, A_HALF)
    hi_cut = jnp.where(hn_ref[c] != 0, 0, A_HALF)

    def block_valid(bq, first, last):
        nk = bq + 2 * A_HALF
        return _dilf_valid(bq, lo0 if first else 0, nk - hi_cut if last else nk)

    def keep(g, rows, o, lse):
        oacc[g, rows, :] = o
        lacc[g, rows, :] = jnp.broadcast_to(lse, o.shape)

    def task(g, q_rows, k, v, bias_ref, valid, out_rows):
        scores = lambda: _dilf_scores(q_rows(), k, bias_ref[0], valid)
        finish = lambda s: keep(g, out_rows, *_dilf_finish(s, v))
        return scores, finish

    bq = DILF_BQ
    tasks = []
    nblk = ch // bq
    k = jnp.concatenate([k1p_ref[...], k1_ref[...], k1n_ref[...]], axis=0)
    v = jnp.concatenate([v1p_ref[...], v1_ref[...], v1n_ref[...]], axis=0)
    for b in range(nblk):
        rows = slice(b * bq, (b + 1) * bq)
        keys = slice(b * bq, (b + 1) * bq + 2 * A_HALF)
        tasks.append(task(0, lambda rows=rows: q1_ref[rows, :], k[keys], v[keys], b1_ref,
                          block_valid(bq, b == 0, b == nblk - 1), rows))
    dil = A_CONFIGS[1][1]
    lc = ch // dil
    nblk = lc // bq
    for r in range(dil):
        sub = lambda ref, n: ref[pl.ds(r, n, stride=dil), :].astype(BF16)
        k = jnp.concatenate([sub(k2p_ref, A_HALF), sub(k2_ref, lc), sub(k2n_ref, A_HALF)], axis=0)
        v = jnp.concatenate([sub(v2p_ref, A_HALF), sub(v2_ref, lc), sub(v2n_ref, A_HALF)], axis=0)
        for b in range(nblk):
            rows = pl.ds(r + b * bq * dil, bq, stride=dil)
            keys = slice(b * bq, (b + 1) * bq + 2 * A_HALF)
            tasks.append(task(1, lambda rows=rows: q2_ref[rows, :].astype(BF16), k[keys], v[keys],
                              b2_ref, block_valid(bq, b == 0, b == nblk - 1), rows))
    _pipelined(tasks)

    dil3 = A_CONFIGS[2][1]
    lc3 = ch // dil3
    valid3 = block_valid(lc3, True, True)

    def residues(i, carry):
        group = []
        for u in range(DILF_GROUP):
            r = i * DILF_GROUP + u
            sub = lambda ref, n, r=r: ref[pl.ds(r, n, stride=dil3), :].astype(BF16)
            k = jnp.concatenate([sub(k3p_ref, A_HALF), sub(k3_ref, lc3), sub(k3n_ref, A_HALF)],
                                axis=0)
            v = jnp.concatenate([sub(v3p_ref, A_HALF), sub(v3_ref, lc3), sub(v3n_ref, A_HALF)],
                                axis=0)
            rows = pl.ds(r, lc3, stride=dil3)
            group.append(task(2, lambda rows=rows: q3_ref[rows, :].astype(BF16), k, v, b3_ref,
                              valid3, rows))
        _pipelined(group)
        return carry
    lax.fori_loop(0, dil3 // DILF_GROUP, residues, 0)

    def combine(i, carry):
        rows = pl.ds(pl.multiple_of(i * DILF_BQ, DILF_BQ), DILF_BQ)
        l0, l1, l2 = lacc[0, rows, :], lacc[1, rows, :], lacc[2, rows, :]
        m = jnp.maximum(jnp.maximum(l0, l1), l2)
        e0, e1, e2 = jnp.exp(l0 - m), jnp.exp(l1 - m), jnp.exp(l2 - m)
        tot = e0 + e1 + e2
        out = (e0 / tot) * oacc[0, rows, :] + (e1 / tot) * oacc[1, rows, :] \
            + (e2 / tot) * oacc[2, rows, :]
        o_ref[rows, :] = out.astype(o_ref.dtype)
        return carry
    lax.fori_loop(0, ch // DILF_BQ, combine, 0)


def _dilated_fused(cfg, p1, p23, biases):
    n = cfg.n
    ch = cfg.ch
    hp, hn = _dilf_flags(cfg)
    nh = A_HEADS
    assert ch // A_CONFIGS[2][1] == 2 * A_HALF and ch % DILF_BQ == 0

    def specs(dil, qb, kb, vb):
        halo = A_HALF * dil
        per = ch // halo
        last = n // halo - 1
        own = lambda cb: pl.BlockSpec((ch, HEAD_DIM), lambda c, h, p, x: (c, cb + h))
        before = lambda cb: pl.BlockSpec(
            (halo, HEAD_DIM), lambda c, h, p, x: (jnp.maximum(c * per - 1, 0), cb + h))
        after = lambda cb: pl.BlockSpec(
            (halo, HEAD_DIM), lambda c, h, p, x: (jnp.minimum((c + 1) * per, last), cb + h))
        return [own(qb), before(kb), own(kb), after(kb), before(vb), own(vb), after(vb)]

    bias_spec = lambda b: pl.BlockSpec((1,) + b.shape[1:], lambda c, h, p, x: (h, 0, 0))
    grid_spec = pltpu.PrefetchScalarGridSpec(
        num_scalar_prefetch=2,
        grid=(n // ch, nh),
        in_specs=(specs(1, 0, nh, 2 * nh) + specs(A_CONFIGS[1][1], 0, 2 * nh, 4 * nh)
                  + specs(A_CONFIGS[2][1], nh, 3 * nh, 5 * nh)
                  + [bias_spec(b) for b in biases]),
        out_specs=pl.BlockSpec((ch, HEAD_DIM), lambda c, h, p, x: (c, h)),
        scratch_shapes=[pltpu.VMEM((3, ch, HEAD_DIM), F32), pltpu.VMEM((3, ch, HEAD_DIM), F32)],
    )
    return pl.pallas_call(
        functools.partial(_dilf_body, ch=ch),
        grid_spec=grid_spec,
        out_shape=jax.ShapeDtypeStruct((n, nh * HEAD_DIM), BF16),
        compiler_params=_params(("arbitrary", "arbitrary")),
        name="dilated_fused",
    )(hp, hn, *([p1] * 7), *([p23] * 14), *biases)


def _dilf_bias(rel_bias, g, dil, bq):
    off = np.arange(bq + 2 * A_HALF)[None, :] - np.arange(bq)[:, None] - A_HALF
    bucket = _t5_bucket_np(dil * np.clip(off, -A_HALF, A_HALF))
    return _expand(rel_bias[:, g * A_HEADS:(g + 1) * A_HEADS].T, bucket[None])[0]


def _flash_worklist(cfg, tq, tk, rel_lo=None, rel_hi=None):
    qb, kb, tile, flags = [], [], [], []
    for start, length in cfg.seqs():
        assert length % tq == 0 and length % tk == 0 and start % tq == 0 and start % tk == 0
        nk = length // tk
        for qi in range(length // tq):
            for kj in range(nk):
                qb.append(start // tq + qi)
                kb.append(start // tk + kj)
                if rel_lo is not None:
                    assert (kj * tk) % tq == 0
                    d = (kj * tk - qi * tq) // tq
                    tile.append(min(max(d, rel_lo), rel_hi) - rel_lo)
                    far = 4 if (d <= rel_lo or d >= rel_hi) else 0
                else:
                    tile.append(0)
                    far = 0
                flags.append((1 if kj == 0 else 0) | (2 if kj == nk - 1 else 0) | far)
    as_i32 = lambda a: jnp.asarray(np.asarray(a, np.int32))
    return as_i32(qb), as_i32(kb), as_i32(tile), as_i32(flags)


def _flash_init(flags, m_scr, acc_scr):
    @pl.when((flags & 1) != 0)
    def _():
        m_scr[...] = jnp.full(m_scr.shape, -jnp.inf, F32)
        acc_scr[...] = jnp.zeros(acc_scr.shape, F32)


def _flash_softmax(h, s, m_scr, shift=None):
    m_prev = m_scr[h]
    m_tile = jnp.max(s, axis=0, keepdims=True)
    if shift is not None:
        m_tile = m_tile + shift
    m_new = jnp.maximum(m_prev, m_tile)
    m_scr[h] = m_new
    m_ref = m_new if shift is None else m_new - shift
    return jnp.exp2(m_prev - m_new), jnp.exp2(s - m_ref).astype(BF16)


def _flash_accumulate(h, alpha, p, vt, acc_scr):
    lhs = jnp.concatenate([vt, jnp.ones((ONES_ROWS, vt.shape[1]), BF16)], axis=0)
    acc_scr[h] = alpha * acc_scr[h] + jnp.dot(lhs, p, preferred_element_type=F32)


def _flash_heads(heads, scores, vt_rows, m_scr, acc_scr, shift=None):
    s_next = scores(0)
    pending = None
    for h in range(heads):
        s_cur = s_next
        if h + 1 < heads:
            s_next = scores(h + 1)
        alpha, p = _flash_softmax(h, s_cur, m_scr, None if shift is None else shift(h))
        if pending is not None:
            _flash_accumulate(*pending, vt_rows(pending[0]), acc_scr)
        pending = (h, alpha, p)
    _flash_accumulate(*pending, vt_rows(pending[0]), acc_scr)


DIFF_TQ = 512
DIFF_TK = 512


def _diff_tile_range(max_len):
    rel = np.arange(-max_len + 1, max_len)
    b = _t5_bucket_np(rel)
    sat_pos = int(rel[b != b[-1]].max()) + 1
    sat_neg = int(rel[b != b[0]].min()) - 1
    hi = -(-(sat_pos + DIFF_TQ - 1) // DIFF_TQ)
    lo = (sat_neg - (DIFF_TK - 1)) // DIFF_TQ
    return lo, hi


def _diff_bias_tiles(rel_bias, lo, hi):
    d = np.arange(lo, hi + 1)[:, None, None] * DIFF_TQ
    rel = d + np.arange(DIFF_TK)[None, :, None] - np.arange(DIFF_TQ)[None, None, :]
    return _expand(rel_bias[:, 3 * A_HEADS:].T * LOG2E, _t5_bucket_np(rel))


def _diff_body(qb_ref, kb_ref, tile_ref, fl_ref, q_ref, k_ref, vt_ref, bias_ref, lam_ref, g_ref,
               o_ref, m_scr, acc_scr, *, lam_init):
    flags = fl_ref[pl.program_id(0)]
    tq = DIFF_TQ
    _flash_init(flags, m_scr, acc_scr)

    lane = lax.broadcasted_iota(jnp.int32, (tq, HEAD_DIM), 1)

    def raw_scores(h):
        cols = slice(h * HEAD_DIM, (h + 1) * HEAD_DIM)
        qh = q_ref[:, cols]
        zero = jnp.zeros_like(qh)
        q2 = jnp.concatenate([jnp.where(lane < B_QK_DIM, qh, zero),
                              jnp.where(lane >= B_QK_DIM, qh, zero)], axis=0)
        return lax.dot_general(k_ref[:, cols], q2, (((1,), (1,)), ((), ())),
                               preferred_element_type=F32)

    def biased_scores(h):
        b = bias_ref[0, h]
        return raw_scores(h) + jnp.concatenate([b, b], axis=1)

    vt_rows = lambda h: vt_ref[h * HEAD_DIM:(h + 1) * HEAD_DIM, :]
    far = (flags & 4) != 0

    @pl.when(far)
    def _():
        _flash_heads(B_HEADS, raw_scores, vt_rows, m_scr, acc_scr,
                     shift=lambda h: bias_ref[0, h, 0:1, 0:1])

    @pl.when(jnp.logical_not(far))
    def _():
        _flash_heads(B_HEADS, biased_scores, vt_rows, m_scr, acc_scr)

    @pl.when((flags & 2) != 0)
    def _():
        lf = lam_ref[...]
        lam = (jnp.exp(jnp.sum(lf[0:1] * lf[1:2], axis=-1, keepdims=True))
               - jnp.exp(jnp.sum(lf[2:3] * lf[3:4], axis=-1, keepdims=True)) + lam_init)
        for h in range(B_HEADS):
            cols = slice(h * HEAD_DIM, (h + 1) * HEAD_DIM)
            acc = acc_scr[h]
            att = acc[:HEAD_DIM] / acc[HEAD_DIM:HEAD_DIM + 1]
            o = att[:, :tq] - lam * att[:, tq:]
            inv = lax.rsqrt(jnp.mean(o * o, axis=0, keepdims=True) + EPS)
            o_ref[:, cols] = (((o * inv).T * g_ref[...]) * (1.0 - lam_init)).astype(o_ref.dtype)


def _diff_attention(cfg, proj, qcol, kcol, vt, bias_tiles, tile_lo, tile_hi, lam, subln_g,
                    lam_init):
    n = cfg.n
    tq, tk = DIFF_TQ, DIFF_TK
    qb, kb, tile, flags = _flash_worklist(cfg, tq, tk, tile_lo, tile_hi)
    grid_spec = pltpu.PrefetchScalarGridSpec(
        num_scalar_prefetch=4,
        grid=(qb.shape[0],),
        in_specs=[pl.BlockSpec((tq, B_W), lambda s, q, k, t, f: (q[s], qcol)),
                  pl.BlockSpec((tk, B_W), lambda s, q, k, t, f: (k[s], kcol)),
                  pl.BlockSpec((B_W, tk), lambda s, q, k, t, f: (0, k[s])),
                  pl.BlockSpec((1, B_HEADS, tk, tq), lambda s, q, k, t, f: (t[s], 0, 0, 0)),
                  pl.BlockSpec((4, B_QK_DIM), lambda s, q, k, t, f: (0, 0)),
                  pl.BlockSpec((1, HEAD_DIM), lambda s, q, k, t, f: (0, 0))],
        out_specs=pl.BlockSpec((tq, B_W), lambda s, q, k, t, f: (q[s], 0)),
        scratch_shapes=[pltpu.VMEM((B_HEADS, 1, 2 * tq), F32),
                        pltpu.VMEM((B_HEADS, HEAD_DIM + ONES_ROWS, 2 * tq), F32)],
    )
    return pl.pallas_call(
        functools.partial(_diff_body, lam_init=lam_init),
        grid_spec=grid_spec,
        out_shape=jax.ShapeDtypeStruct((n, B_W), BF16),
        compiler_params=_params(("arbitrary",)),
        name="diff_attention",
    )(qb, kb, tile, flags, proj, proj, vt, bias_tiles, lam, subln_g.reshape(1, HEAD_DIM))


MLA_TM = 512
MLA_TQ = 1024
MLA_TK = 512


def _rope_tables(max_len):
    inv = ROPE_THETA ** (-jnp.arange(0, C_ROPE, 2, dtype=F32) / C_ROPE)
    ang = jnp.arange(max_len, dtype=F32)[:, None] * inv[None, :]
    cos, sin = jnp.cos(ang), jnp.sin(ang)
    zero = jnp.zeros((max_len, HEAD_DIM - C_ROPE), F32)
    return (jnp.concatenate([cos, cos, zero], axis=1),
            jnp.concatenate([-sin, sin, zero], axis=1))


def _mla_prep_body(p_ref, gq_ref, gkv_ref, wqa_ref, wqb_ref, wk_ref, wvt_ref, cos_ref, sin_ref,
                   q_ref, k_ref, vt_ref, *, scale):
    def normed(x, g):
        inv = lax.rsqrt(jnp.mean(x * x, axis=-1, keepdims=True) + EPS)
        return (x * inv * g).astype(BF16)

    cq = normed(p_ref[:, 0:C_LORA], gq_ref[...])
    ckv = normed(p_ref[:, C_LORA:2 * C_LORA], gkv_ref[...])
    cos = cos_ref[...]
    sin = sin_ref[...]
    qa = jnp.dot(cq, wqa_ref[...], preferred_element_type=F32)
    qb = jnp.dot(cq, wqb_ref[...], preferred_element_type=F32)
    kn = jnp.dot(ckv, wk_ref[...], preferred_element_type=F32)
    vt_ref[...] = lax.dot_general(wvt_ref[...], ckv, (((1,), (1,)), ((), ())),
                                  preferred_element_type=F32).astype(BF16)
    kr = p_ref[:, 2 * C_LORA:2 * C_LORA + HEAD_DIM]
    kr_sw = p_ref[:, 2 * C_LORA + HEAD_DIM:2 * C_LORA + 2 * HEAD_DIM]
    k_rope = (kr * cos + kr_sw * sin).astype(BF16)
    for h in range(C_HEADS):
        a0 = h * C_QK_PAD
        a1 = a0 + HEAD_DIM
        a2 = a0 + C_QK_PAD
        b = slice(h * HEAD_DIM, (h + 1) * HEAD_DIM)
        q_ref[:, a0:a1] = (qa[:, a0:a1] * scale).astype(BF16)
        q_ref[:, a1:a2] = ((qa[:, a1:a2] * cos + qb[:, b] * sin) * scale).astype(BF16)
        k_ref[:, a0:a1] = kn[:, b].astype(BF16)
        k_ref[:, a1:a2] = k_rope


def _mla_prep(cfg, proj_c, gq, gkv, wqa, wqb, wk, wvt, cos_tab, sin_tab):
    n = cfg.n
    tm = MLA_TM
    pblocks = cfg.pc * cfg.ch // tm
    per = cfg.ch // tm
    pos = lambda i: (jnp.where(i < pblocks, i, (i - pblocks) % per), 0)
    full = lambda shape: pl.BlockSpec(shape, lambda i: (0, 0))
    qk_w = C_HEADS * C_QK_PAD
    v_w = C_HEADS * HEAD_DIM
    return pl.pallas_call(
        functools.partial(_mla_prep_body, scale=(C_NOPE + C_ROPE) ** -0.5 * LOG2E),
        grid=(n // tm,),
        in_specs=[pl.BlockSpec((tm, proj_c.shape[1]), lambda i: (i, 0)),
                  full((1, C_LORA)), full((1, C_LORA)),
                  full(wqa.shape), full(wqb.shape), full(wk.shape), full(wvt.shape),
                  pl.BlockSpec((tm, HEAD_DIM), pos), pl.BlockSpec((tm, HEAD_DIM), pos)],
        out_specs=[pl.BlockSpec((tm, qk_w), lambda i: (i, 0)),
                   pl.BlockSpec((tm, qk_w), lambda i: (i, 0)),
                   pl.BlockSpec((v_w, tm), lambda i: (0, i))],
        out_shape=[jax.ShapeDtypeStruct((n, qk_w), BF16),
                   jax.ShapeDtypeStruct((n, qk_w), BF16),
                   jax.ShapeDtypeStruct((v_w, n), BF16)],
        compiler_params=_params(("arbitrary",)),
        name="mla_prep",
    )(proj_c, gq.reshape(1, C_LORA), gkv.reshape(1, C_LORA), wqa, wqb, wk, wvt, cos_tab, sin_tab)


def _mla_body(qb_ref, kb_ref, tile_ref, fl_ref, q_ref, k_ref, vt_ref, o_ref, m_scr, acc_scr):
    flags = fl_ref[pl.program_id(0)]
    _flash_init(flags, m_scr, acc_scr)

    def scores(h):
        qk = slice(h * C_QK_PAD, (h + 1) * C_QK_PAD)
        return lax.dot_general(k_ref[:, qk], q_ref[:, qk], (((1,), (1,)), ((), ())),
                               preferred_element_type=F32)

    vt_rows = lambda h: vt_ref[h * HEAD_DIM:(h + 1) * HEAD_DIM, :]
    _flash_heads(C_HEADS, scores, vt_rows, m_scr, acc_scr)

    @pl.when((flags & 2) != 0)
    def _():
        for h in range(C_HEADS):
            vc = slice(h * HEAD_DIM, (h + 1) * HEAD_DIM)
            acc = acc_scr[h]
            o_ref[:, vc] = (acc[:HEAD_DIM] / acc[HEAD_DIM:HEAD_DIM + 1]).T.astype(o_ref.dtype)


def _mla_attention(cfg, q, k, vt):
    n = cfg.n
    tq, tk = MLA_TQ, MLA_TK
    qb, kb, tile, flags = _flash_worklist(cfg, tq, tk)
    qk_w = C_HEADS * C_QK_PAD
    v_w = C_HEADS * HEAD_DIM
    grid_spec = pltpu.PrefetchScalarGridSpec(
        num_scalar_prefetch=4,
        grid=(qb.shape[0],),
        in_specs=[pl.BlockSpec((tq, qk_w), lambda s, q_, k_, t, f: (q_[s], 0)),
                  pl.BlockSpec((tk, qk_w), lambda s, q_, k_, t, f: (k_[s], 0)),
                  pl.BlockSpec((v_w, tk), lambda s, q_, k_, t, f: (0, k_[s]))],
        out_specs=pl.BlockSpec((tq, v_w), lambda s, q_, k_, t, f: (q_[s], 0)),
        scratch_shapes=[pltpu.VMEM((C_HEADS, 1, tq), F32),
                        pltpu.VMEM((C_HEADS, HEAD_DIM + ONES_ROWS, tq), F32)],
    )
    return pl.pallas_call(
        _mla_body,
        grid_spec=grid_spec,
        out_shape=jax.ShapeDtypeStruct((n, v_w), BF16),
        compiler_params=_params(("arbitrary",)),
        name="mla_attention",
    )(qb, kb, tile, flags, q, k, vt)


def _na_tables(cfg):
    prev, nxt, var = [], [], []
    for start, length in cfg.seqs():
        nb = length // NA_BLOCK
        assert length % NA_BLOCK == 0 and start % NA_BLOCK == 0 and nb >= 3
        b0 = start // NA_BLOCK
        for r in range(nb):
            prev.append(b0 + max(r - 1, 0))
            nxt.append(b0 + min(r + 1, nb - 1))
            var.append(0 if r == 0 else (2 if r == nb - 1 else 1))
    as_i32 = lambda a: jnp.asarray(np.asarray(a, np.int32))
    return as_i32(prev), as_i32(nxt), as_i32(var)


def _na_window_start(a, variant):
    centred = a - NA_ROWS // 2
    return (max(centred, 0), centred, min(centred, 0))[variant]


def _na_bias_tables(rpb):
    ncol = 2 * NA_COLS - 1
    c = np.arange(GRID_W)[:, None, None]
    b = np.arange(NA_ROWS)[None, :, None]
    kc = np.arange(GRID_W)[None, None, :]
    cstart = np.clip(c - NA_COLS // 2, 0, GRID_W - NA_COLS)
    col_ok = (kc >= cstart) & (kc < cstart + NA_COLS)
    dc = np.clip(kc - c, -(NA_COLS - 1), NA_COLS - 1) + (NA_COLS - 1)
    masked = NA_ROWS * ncol
    idx = np.where(col_ok, b * ncol + dc, masked).reshape(1, GRID_W, NA_BLOCK)
    rows = np.arange(NA_ROWS)[:, None] + np.arange(NA_ROWS)[None, :]
    tab = rpb[:, rows, :].reshape(D_HEADS * NA_ROWS, NA_ROWS * ncol)
    tab = jnp.concatenate([tab, jnp.full((tab.shape[0], 1), NEG, F32)], axis=1)
    slabs = _expand(tab, idx)[0].reshape(D_HEADS, NA_ROWS, GRID_W, NA_BLOCK)
    variants = []
    for variant in range(3):
        row_blocks = []
        for a in range(NA_ROWS):
            start = _na_window_start(a, variant)
            left = (NA_ROWS + start) * GRID_W
            row_blocks.append(jnp.pad(slabs[:, start - a + NA_ROWS - 1],
                                      ((0, 0), (0, 0), (left, 2 * NA_BLOCK - left)),
                                      constant_values=NEG))
        variants.append(jnp.concatenate(row_blocks, axis=1))
    return jnp.stack(variants, axis=0)


def _na_body(prev_ref, nxt_ref, var_ref, q_ref, kp_ref, km_ref, kn_ref, vp_ref, vm_ref, vn_ref,
             tab_ref, o_ref):
    k = jnp.concatenate([kp_ref[...], km_ref[...], kn_ref[...]], axis=0)
    v = jnp.concatenate([vp_ref[...], vm_ref[...], vn_ref[...]], axis=0)
    def scores(h):
        cols = slice(h * HEAD_DIM, (h + 1) * HEAD_DIM)
        s = lax.dot_general(q_ref[:, cols], k[:, cols], (((1,), (1,)), ((), ())),
                            preferred_element_type=F32)
        return s + tab_ref[0, h]

    def finish(h, s):
        cols = slice(h * HEAD_DIM, (h + 1) * HEAD_DIM)
        m = jnp.max(s, axis=-1, keepdims=True)
        e = jnp.exp(s - m)
        den = jnp.sum(e, axis=-1, keepdims=True)
        o = jnp.dot(e.astype(BF16), v[:, cols], preferred_element_type=F32)
        o_ref[:, cols] = (o / den).astype(o_ref.dtype)

    _banded_heads(D_HEADS, scores, finish)


def _na_attention(cfg, qkv, tabs):
    n = cfg.n
    w = D_HEADS * HEAD_DIM
    prev, nxt, var = _na_tables(cfg)
    own = lambda cb: pl.BlockSpec((NA_BLOCK, w), lambda i, p, x, t: (i, cb))
    before = lambda cb: pl.BlockSpec((NA_BLOCK, w), lambda i, p, x, t: (p[i], cb))
    after = lambda cb: pl.BlockSpec((NA_BLOCK, w), lambda i, p, x, t: (x[i], cb))
    grid_spec = pltpu.PrefetchScalarGridSpec(
        num_scalar_prefetch=3,
        grid=(n // NA_BLOCK,),
        in_specs=[own(0), before(1), own(1), after(1), before(2), own(2), after(2),
                  pl.BlockSpec((1, D_HEADS, NA_BLOCK, 3 * NA_BLOCK),
                               lambda i, p, x, t: (t[i], 0, 0, 0))],
        out_specs=pl.BlockSpec((NA_BLOCK, w), lambda i, p, x, t: (i, 0)),
    )
    return pl.pallas_call(
        _na_body,
        grid_spec=grid_spec,
        out_shape=jax.ShapeDtypeStruct((n, w), BF16),
        compiler_params=_params(("arbitrary",)),
        name="na_attention",
    )(prev, nxt, var, qkv, qkv, qkv, qkv, qkv, qkv, qkv, tabs)


EVEN_BLOCKS_ROWMAJOR = (0, 3, 6, 9, 10)
EVEN_BLOCKS_STRIDED = (1, 2, 4, 5, 7, 8)
EVEN_BLOCK_VT = 11


def _even_weights(w_in):
    blk = lambda b: w_in[:, b * B_W:(b + 1) * B_W]
    w_a = jnp.concatenate([blk(b) for b in EVEN_BLOCKS_ROWMAJOR], axis=1)
    w_b = jnp.concatenate([blk(b) for b in EVEN_BLOCKS_STRIDED], axis=1)
    return w_a, w_b, blk(EVEN_BLOCK_VT).T


def _even_col_scales():
    cs_a = np.ones((1, len(EVEN_BLOCKS_ROWMAJOR) * B_W), np.float32)
    cs_a[0, :B_W] = HEAD_DIM ** -0.5
    cs_a[0, 3 * B_W:4 * B_W] = B_QK_DIM ** -0.5 * LOG2E
    cs_b = np.ones((1, len(EVEN_BLOCKS_STRIDED) * B_W), np.float32)
    cs_b[0, :2 * B_W] = HEAD_DIM ** -0.5
    return jnp.asarray(cs_a), jnp.asarray(cs_b)


def _even_tables(cfg, rel_bias):
    dil = [_dilf_bias(rel_bias, gi, d, min(DILF_BQ, cfg.ch // d))
           for gi, (_, d) in enumerate(A_CONFIGS)]
    tile_lo, tile_hi = _diff_tile_range(cfg.pc * cfg.ch)
    return dil, _diff_bias_tiles(rel_bias, tile_lo, tile_hi), tile_lo, tile_hi


def _even_mixer(cfg, x, g, mods, weights, w_out, lam, subln_g, tables, lam_init, tm):
    w_a, w_b, wvt = weights
    dil_bias, diff_tiles, tile_lo, tile_hi = tables
    cs_a, cs_b = _even_col_scales()
    p1 = _norm_matmul(cfg, x, g, mods, 1, w_a, cs_a, BF16, tm, 1024, "even_in_a")
    p23 = _norm_matmul(cfg, x, g, mods, 1, w_b, cs_b, F32, tm, 1024, "even_in_b")
    vt = _norm_matmul_t(cfg, x, g, mods, 1, wvt, tm, "even_in_vt")
    o_a = _dilated_fused(cfg, p1, p23, dil_bias)
    o_b = _diff_attention(cfg, p1, 3, 4, vt, diff_tiles, tile_lo, tile_hi, lam, subln_g, lam_init)
    return _resid_matmul(cfg, [o_a, o_b], w_out, x, mods, 1, 1.0, tm, 1024, "even_out")


def _odd_weights(w_in, w_q_up, w_kv_up):
    half = C_ROPE // 2
    swap = np.concatenate([np.arange(half, C_ROPE), np.arange(half)])
    o2 = 2 * C_LORA
    zpad = jnp.zeros((D_MODEL, HEAD_DIM - C_ROPE), w_in.dtype)
    kr = w_in[:, o2:o2 + C_ROPE]
    w_c = jnp.concatenate([w_in[:, :o2], kr, zpad, kr[:, swap], zpad], axis=1)
    w_d = w_in[:, o2 + C_ROPE:]
    q3 = w_q_up.reshape(C_LORA, C_HEADS, C_NOPE + C_ROPE)
    zq = jnp.zeros((C_LORA, C_HEADS, C_QK_PAD - C_NOPE - C_ROPE), w_q_up.dtype)
    wqa = jnp.concatenate([q3, zq], axis=2).reshape(C_LORA, C_HEADS * C_QK_PAD)
    wqb = jnp.concatenate([q3[:, :, C_NOPE:][:, :, swap], zq], axis=2).reshape(
        C_LORA, C_HEADS * HEAD_DIM)
    kv3 = w_kv_up.reshape(C_LORA, C_HEADS, 2 * HEAD_DIM)
    wk = kv3[:, :, :C_NOPE].reshape(C_LORA, C_HEADS * HEAD_DIM)
    wvt = kv3[:, :, C_NOPE:].reshape(C_LORA, C_HEADS * HEAD_DIM).T
    return w_c, w_d, wqa, wqb, wk, wvt


def _odd_mixer(cfg, x, g, mods, weights, gq, gkv, rpb, w_out, rope, tm):
    w_c, w_d, wqa, wqb, wk, wvt = weights
    ones_c = jnp.ones((1, w_c.shape[1]), F32)
    proj_c = _norm_matmul(cfg, x, g, mods, 1, w_c, ones_c, F32, tm, w_c.shape[1], "odd_in_latent")
    cs = np.ones((1, w_d.shape[1]), np.float32)
    cs[0, :D_HEADS * HEAD_DIM] = HEAD_DIM ** -0.5
    qkv_d = _norm_matmul(cfg, x, g, mods, 1, w_d, jnp.asarray(cs), BF16, tm, w_d.shape[1],
                         "odd_in_na")
    q, k, vt = _mla_prep(cfg, proj_c, gq, gkv, wqa, wqb, wk, wvt, *rope)
    o_c = _mla_attention(cfg, q, k, vt)
    o_d = _na_attention(cfg, qkv_d, _na_bias_tables(rpb))
    return _resid_matmul(cfg, [o_c, o_d], w_out, x, mods, 1, 1.0, tm, 1024, "odd_out")


def _trunk(cfg, x, c_pad, ada_w, ada_b, norm_g, ffn_w_in, ffn_w_out, rel_bias, ev_w_in, ev_w_out,
           diff_lambda, diff_subln_g, od_w_in, mla_q_norm_g, mla_kv_norm_g, mla_w_q_up,
           mla_w_kv_up, na_rpb, od_w_out, final_norm_g):
    depth = ada_w.shape[0]
    tm = 1024
    nseq = 1 + cfg.sb
    mod_all = _modulation(c_pad, ada_w, ada_b)[:, :nseq]
    chunk_seq = np.array([0] * cfg.pc + list(range(1, nseq)))
    mod_all = mod_all[:, chunk_seq].reshape(depth, (cfg.pc + cfg.sb) * 9, 1, D_MODEL)
    rope = _rope_tables(cfg.pc * cfg.ch)
    even_tables = _even_tables(cfg, rel_bias)
    for i in range(depth):
        mods = mod_all[i]
        j = i // 2
        hid = _norm_swiglu(cfg, x, norm_g[i, 0], mods, 0, ffn_w_in[i, 0].astype(BF16), tm, 512)
        x = _resid_matmul(cfg, [hid], ffn_w_out[i, 0].astype(BF16), x, mods, 0, 0.5, tm, 512,
                          "ffn_out")
        if i % 2 == 0:
            x = _even_mixer(cfg, x, norm_g[i, 1], mods, _even_weights(ev_w_in[j].astype(BF16)),
                            ev_w_out[j].astype(BF16), diff_lambda[j], diff_subln_g[j], even_tables,
                            0.8 - 0.6 * math.exp(-0.3 * i), tm)
        else:
            weights = tuple(a.astype(BF16) for a in
                            _odd_weights(od_w_in[j], mla_w_q_up[j], mla_w_kv_up[j]))
            x = _odd_mixer(cfg, x, norm_g[i, 1], mods, weights, mla_q_norm_g[j], mla_kv_norm_g[j],
                           na_rpb[j], od_w_out[j].astype(BF16), rope, tm)
        hid = _norm_swiglu(cfg, x, norm_g[i, 2], mods, 2, ffn_w_in[i, 1].astype(BF16), tm, 512)
        x = _resid_matmul(cfg, [hid], ffn_w_out[i, 1].astype(BF16), x, mods, 2, 0.5, tm, 512,
                          "ffn_out")
    return _final_norm(cfg, x, final_norm_g)


def kernel(x_prompt, x_sample, c_prompt, c_sample, ada_w, ada_b, norm_g, ffn_w_in, ffn_w_out, rel_bias, ev_w_in, ev_w_out, diff_lambda, diff_subln_g, od_w_in, mla_q_norm_g, mla_kv_norm_g, mla_w_q_up, mla_w_kv_up, na_rpb, od_w_out, final_norm_g):
    pb, pt, _ = x_prompt.shape
    sb, st, _ = x_sample.shape
    assert pb == 1 and pt % st == 0
    cfg = Cfg(ch=st, pc=pt // st, sb=sb)
    x = jnp.concatenate([x_prompt.reshape(-1, D_MODEL), x_sample.reshape(-1, D_MODEL)], axis=0)
    c = jnp.concatenate([c_prompt, c_sample], axis=0)
    c_pad = jnp.pad(c, ((0, -c.shape[0] % 8), (0, 0)))
    y_p, y_s = _trunk(cfg, x, c_pad, ada_w, ada_b, norm_g, ffn_w_in, ffn_w_out, rel_bias, ev_w_in,
               ev_w_out, diff_lambda, diff_subln_g, od_w_in, mla_q_norm_g, mla_kv_norm_g,
               mla_w_q_up, mla_w_kv_up, na_rpb, od_w_out, final_norm_g)
    return (y_p.reshape(x_prompt.shape), y_s.reshape(x_sample.shape))
```

```python
import functools
import math
from typing import NamedTuple

import numpy as np
import jax
import jax.numpy as jnp
from jax import lax
from jax.experimental import pallas as pl
from jax.experimental.pallas import tpu as pltpu

F32 = jnp.float32
BF16 = jnp.bfloat16

D_MODEL = 2048
D_FF = 5632
HEAD_DIM = 128
A_HEADS = 8
A_CONFIGS = ((128, 1), (512, 4), (2048, 16))
A_HALF = 64
A_IN = 3 * 3 * A_HEADS * HEAD_DIM
B_HEADS = 8
B_QK_DIM = 64
B_W = B_HEADS * 2 * B_QK_DIM
EVEN_IN = A_IN + 3 * B_W
C_HEADS = 12
C_LORA = 512
C_NOPE = 128
C_ROPE = 64
C_QK_PAD = 256
ROPE_THETA = 10000.0
D_HEADS = 4
GRID_W = 64
NA_ROWS = 8
NA_COLS = 16
NA_BLOCK = NA_ROWS * GRID_W
REL_BUCKETS = 32
REL_MAX_DIST = 1024
EPS = 1e-6
NEG = -1e30
LOG2E = math.log2(math.e)
ONES_ROWS = 16

V7X_VMEM_BYTES = 64 * 1024 * 1024
VMEM_LIMIT = V7X_VMEM_BYTES - 8 * 1024 * 1024


class Cfg(NamedTuple):
    ch: int
    pc: int
    sb: int

    @property
    def n(self):
        return self.ch * (self.pc + self.sb)

    def seqs(self):
        out = [(0, self.pc * self.ch)]
        out += [((self.pc + b) * self.ch, self.ch) for b in range(self.sb)]
        return out


def _params(sem):
    return pltpu.CompilerParams(dimension_semantics=sem, vmem_limit_bytes=VMEM_LIMIT)


def _mod_body(c_ref, w_ref, b_ref, o_ref):
    c = c_ref[...]
    act = (c * jax.nn.sigmoid(c)).astype(BF16)
    o_ref[0] = jnp.dot(act, w_ref[0].astype(BF16), preferred_element_type=F32) + b_ref[0]


def _modulation(c_pad, ada_w, ada_b):
    depth, _, nout = ada_w.shape
    r = c_pad.shape[0]
    tn = 1024
    return pl.pallas_call(
        _mod_body,
        grid=(depth, nout // tn),
        in_specs=[pl.BlockSpec((r, D_MODEL), lambda l, j: (0, 0)),
                  pl.BlockSpec((1, D_MODEL, tn), lambda l, j: (l, 0, j)),
                  pl.BlockSpec((1, 1, tn), lambda l, j: (l, 0, j))],
        out_specs=pl.BlockSpec((1, r, tn), lambda l, j: (l, 0, j)),
        out_shape=jax.ShapeDtypeStruct((depth, r, nout), F32),
        compiler_params=_params(("arbitrary", "arbitrary")),
        name="modulation",
    )(c_pad, ada_w, ada_b.reshape(depth, 1, nout))


NORM_ROWS = 32


def _norm_rows(x_ref, g_ref, sh_ref, sc_ref, h_scr, tm):
    gain = g_ref[...] * (1.0 + sc_ref[0])
    shift = sh_ref[0]

    def body(r, carry):
        rows = pl.ds(pl.multiple_of(r * NORM_ROWS, NORM_ROWS), NORM_ROWS)
        x = x_ref[rows, :]
        inv = lax.rsqrt(jnp.mean(x * x, axis=-1, keepdims=True) + EPS)
        h_scr[rows, :] = ((x * inv) * gain + shift).astype(BF16)
        return carry
    lax.fori_loop(0, tm // NORM_ROWS, body, 0, unroll=4)


def _norm_mm_body(x_ref, g_ref, sh_ref, sc_ref, w_ref, cs_ref, o_ref, h_scr, *, tm):
    @pl.when(pl.program_id(1) == 0)
    def _():
        _norm_rows(x_ref, g_ref, sh_ref, sc_ref, h_scr, tm)
    acc = jnp.dot(h_scr[...], w_ref[...], preferred_element_type=F32)
    o_ref[...] = (acc * cs_ref[...]).astype(o_ref.dtype)


def _norm_mm_t_body(x_ref, g_ref, sh_ref, sc_ref, wt_ref, o_ref, h_scr, *, tm):
    _norm_rows(x_ref, g_ref, sh_ref, sc_ref, h_scr, tm)
    acc = lax.dot_general(wt_ref[...], h_scr[...], (((1,), (1,)), ((), ())),
                          preferred_element_type=F32)
    o_ref[...] = acc.astype(o_ref.dtype)


def _norm_swiglu_body(x_ref, g_ref, sh_ref, sc_ref, wg_ref, wu_ref, o_ref, h_scr, *, tm):
    @pl.when(pl.program_id(1) == 0)
    def _():
        _norm_rows(x_ref, g_ref, sh_ref, sc_ref, h_scr, tm)
    h = h_scr[...]
    gate = jnp.dot(h, wg_ref[...], preferred_element_type=F32)
    up = jnp.dot(h, wu_ref[...], preferred_element_type=F32)
    o_ref[...] = (gate * jax.nn.sigmoid(gate) * up).astype(o_ref.dtype)


def _mod_specs(cfg, tm, sub):
    per = cfg.ch // tm
    shift = pl.BlockSpec((1, 1, D_MODEL), lambda i, *_: ((i // per) * 9 + sub * 3, 0, 0))
    scale = pl.BlockSpec((1, 1, D_MODEL), lambda i, *_: ((i // per) * 9 + sub * 3 + 1, 0, 0))
    return shift, scale


def _norm_matmul(cfg, x, g, mods, sub, w, col_scale, out_dtype, tm, tn, name):
    n = x.shape[0]
    nout = w.shape[1]
    shift, scale = _mod_specs(cfg, tm, sub)
    return pl.pallas_call(
        functools.partial(_norm_mm_body, tm=tm),
        grid=(n // tm, nout // tn),
        in_specs=[pl.BlockSpec((tm, D_MODEL), lambda i, j: (i, 0)),
                  pl.BlockSpec((1, D_MODEL), lambda i, j: (0, 0)),
                  shift, scale,
                  pl.BlockSpec((D_MODEL, tn), lambda i, j: (0, j)),
                  pl.BlockSpec((1, tn), lambda i, j: (0, j))],
        out_specs=pl.BlockSpec((tm, tn), lambda i, j: (i, j)),
        out_shape=jax.ShapeDtypeStruct((n, nout), out_dtype),
        scratch_shapes=[pltpu.VMEM((tm, D_MODEL), BF16)],
        compiler_params=_params(("arbitrary", "arbitrary")),
        name=name,
    )(x, g.reshape(1, D_MODEL), mods, mods, w, col_scale)


def _norm_matmul_t(cfg, x, g, mods, sub, wt, tm, name):
    n = x.shape[0]
    nout = wt.shape[0]
    shift, scale = _mod_specs(cfg, tm, sub)
    return pl.pallas_call(
        functools.partial(_norm_mm_t_body, tm=tm),
        grid=(n // tm,),
        in_specs=[pl.BlockSpec((tm, D_MODEL), lambda i: (i, 0)),
                  pl.BlockSpec((1, D_MODEL), lambda i: (0, 0)),
                  shift, scale,
                  pl.BlockSpec((nout, D_MODEL), lambda i: (0, 0))],
        out_specs=pl.BlockSpec((nout, tm), lambda i: (0, i)),
        out_shape=jax.ShapeDtypeStruct((nout, n), BF16),
        scratch_shapes=[pltpu.VMEM((tm, D_MODEL), BF16)],
        compiler_params=_params(("arbitrary",)),
        name=name,
    )(x, g.reshape(1, D_MODEL), mods, mods, wt)


def _norm_swiglu(cfg, x, g, mods, sub, w_in, tm, tn):
    n = x.shape[0]
    nj = D_FF // tn
    shift, scale = _mod_specs(cfg, tm, sub)
    return pl.pallas_call(
        functools.partial(_norm_swiglu_body, tm=tm),
        grid=(n // tm, nj),
        in_specs=[pl.BlockSpec((tm, D_MODEL), lambda i, j: (i, 0)),
                  pl.BlockSpec((1, D_MODEL), lambda i, j: (0, 0)),
                  shift, scale,
                  pl.BlockSpec((D_MODEL, tn), lambda i, j: (0, j)),
                  pl.BlockSpec((D_MODEL, tn), lambda i, j: (0, j + nj))],
        out_specs=pl.BlockSpec((tm, tn), lambda i, j: (i, j)),
        out_shape=jax.ShapeDtypeStruct((n, D_FF), BF16),
        scratch_shapes=[pltpu.VMEM((tm, D_MODEL), BF16)],
        compiler_params=_params(("arbitrary", "arbitrary")),
        name="ffn_in",
    )(x, g.reshape(1, D_MODEL), mods, mods, w_in, w_in)


def _resid_mm_body(*refs, pieces, coef):
    lhs = refs[:pieces]
    ws = refs[pieces:2 * pieces]
    x_ref, gate_ref, o_ref = refs[2 * pieces:]
    acc = jnp.dot(lhs[0][...], ws[0][...], preferred_element_type=F32)
    for p in range(1, pieces):
        acc = acc + jnp.dot(lhs[p][...], ws[p][...], preferred_element_type=F32)
    o_ref[...] = x_ref[...] + (coef * gate_ref[0]) * acc


def _resid_matmul(cfg, lhs_list, w, x, mods, sub, coef, tm, tn, name):
    n = x.shape[0]
    per = cfg.ch // tm
    widths = [a.shape[1] for a in lhs_list]
    offs = np.cumsum([0] + widths[:-1])
    in_specs = [pl.BlockSpec((tm, k), lambda i, j: (i, 0)) for k in widths]
    for k, off in zip(widths, offs):
        assert off % k == 0
        in_specs.append(pl.BlockSpec((k, tn), lambda i, j, b=int(off // k): (b, j)))
    in_specs += [pl.BlockSpec((tm, tn), lambda i, j: (i, j)),
                 pl.BlockSpec((1, 1, tn), lambda i, j: ((i // per) * 9 + sub * 3 + 2, 0, j))]
    return pl.pallas_call(
        functools.partial(_resid_mm_body, pieces=len(lhs_list), coef=coef),
        grid=(n // tm, D_MODEL // tn),
        in_specs=in_specs,
        out_specs=pl.BlockSpec((tm, tn), lambda i, j: (i, j)),
        out_shape=jax.ShapeDtypeStruct((n, D_MODEL), F32),
        compiler_params=_params(("arbitrary", "arbitrary")),
        name=name,
    )(*lhs_list, *([w] * len(lhs_list)), x, mods)


def _final_norm_body(x_ref, g_ref, op_ref, os_ref, *, pblocks):
    x = x_ref[...]
    inv = lax.rsqrt(jnp.mean(x * x, axis=-1, keepdims=True) + EPS)
    y = x * inv * g_ref[...]
    i = pl.program_id(0)

    @pl.when(i < pblocks)
    def _():
        op_ref[...] = y

    @pl.when(i >= pblocks)
    def _():
        os_ref[...] = y


def _final_norm(cfg, x, g, tm=512):
    n = x.shape[0]
    npr = cfg.pc * cfg.ch
    pblocks = npr // tm
    return pl.pallas_call(
        functools.partial(_final_norm_body, pblocks=pblocks),
        grid=(n // tm,),
        in_specs=[pl.BlockSpec((tm, D_MODEL), lambda i: (i, 0)),
                  pl.BlockSpec((1, D_MODEL), lambda i: (0, 0))],
        out_specs=[pl.BlockSpec((tm, D_MODEL), lambda i: (jnp.minimum(i, pblocks - 1), 0)),
                   pl.BlockSpec((tm, D_MODEL), lambda i: (jnp.maximum(i - pblocks, 0), 0))],
        out_shape=[jax.ShapeDtypeStruct((npr, D_MODEL), F32),
                   jax.ShapeDtypeStruct((n - npr, D_MODEL), F32)],
        compiler_params=_params(("arbitrary",)),
        name="final_norm",
    )(x, g.reshape(1, D_MODEL))


def _t5_bucket_np(rel):
    nb = REL_BUCKETS // 2
    max_exact = nb // 2
    rel = np.asarray(rel, np.int64)
    base = np.where(rel > 0, nb, 0)
    n = np.abs(rel)
    nf = np.maximum(n, 1).astype(np.float64)
    large = max_exact + (np.log(nf / max_exact) / math.log(REL_MAX_DIST / max_exact)
                         * (nb - max_exact)).astype(np.int64)
    large = np.minimum(large, nb - 1)
    return (base + np.where(n < max_exact, n, large)).astype(np.int32)


EXPAND_ROWS = 32
EXPAND_UNROLL_MAX = 64


def _expand_body(tab_ref, idx_ref, o_ref, *, nb):
    t = pl.program_id(1)

    def chunk(r, carry):
        rows = pl.ds(pl.multiple_of(r * EXPAND_ROWS, EXPAND_ROWS), EXPAND_ROWS)
        idx = idx_ref[0, rows, :]
        pick = lambda b, acc: jnp.where(idx == b, tab_ref[t, b], acc)
        o_ref[0, 0, rows, :] = lax.fori_loop(0, nb, pick, jnp.zeros(idx.shape, F32),
                                             unroll=True if nb <= EXPAND_UNROLL_MAX else 8)
        return carry
    lax.fori_loop(0, idx_ref.shape[1] // EXPAND_ROWS, chunk, 0)


def _expand(tab, idx):
    ntab, nb = tab.shape
    ni, r, c = idx.shape
    assert r % EXPAND_ROWS == 0 and idx.min() >= 0 and idx.max() < nb
    return pl.pallas_call(
        functools.partial(_expand_body, nb=nb),
        grid=(ni, ntab),
        in_specs=[pl.BlockSpec(memory_space=pltpu.SMEM),
                  pl.BlockSpec((1, r, c), lambda i, t: (i, 0, 0))],
        out_specs=pl.BlockSpec((1, 1, r, c), lambda i, t: (i, t, 0, 0)),
        out_shape=jax.ShapeDtypeStruct((ni, ntab, r, c), F32),
        compiler_params=_params(("arbitrary", "arbitrary")),
        name="expand_table",
    )(tab, jnp.asarray(idx.astype(np.int32)))


def _banded_heads(heads, scores, finish):
    s_next = scores(0)
    for h in range(heads):
        s_cur = s_next
        if h + 1 < heads:
            s_next = scores(h + 1)
        finish(h, s_cur)


DILF_BQ = 128
DILF_GROUP = 4


def _dilf_flags(cfg):
    nchunks = cfg.pc + cfg.sb
    prev = np.array([1 if 0 < c < cfg.pc else 0 for c in range(nchunks)], np.int32)
    nxt = np.array([1 if c < cfg.pc - 1 else 0 for c in range(nchunks)], np.int32)
    return jnp.asarray(prev), jnp.asarray(nxt)


def _dilf_scores(q, k, bias, valid):
    s = lax.dot_general(q, k, (((1,), (1,)), ((), ())), preferred_element_type=F32)
    return jnp.where(valid, s + bias, NEG)


def _dilf_finish(s, v):
    m = jnp.max(s, axis=-1, keepdims=True)
    e = jnp.exp(s - m)
    den = jnp.sum(e, axis=-1, keepdims=True)
    o = jnp.dot(e.astype(BF16), v, preferred_element_type=F32)
    return o / den, m + jnp.log(den)


def _pipelined(tasks):
    s_next = tasks[0][0]()
    for t, (_, finish) in enumerate(tasks):
        s_cur = s_next
        if t + 1 < len(tasks):
            s_next = tasks[t + 1][0]()
        finish(s_cur)


def _dilf_valid(bq, lo, hi):
    nk = bq + 2 * A_HALF
    row = lax.broadcasted_iota(jnp.int32, (bq, nk), 0)
    col = lax.broadcasted_iota(jnp.int32, (bq, nk), 1)
    off = col - row
    return (off >= 0) & (off <= 2 * A_HALF) & (col >= lo) & (col < hi)


def _dilf_body(hp_ref, hn_ref,
               q1_ref, k1p_ref, k1_ref, k1n_ref, v1p_ref, v1_ref, v1n_ref,
               q2_ref, k2p_ref, k2_ref, k2n_ref, v2p_ref, v2_ref, v2n_ref,
               q3_ref, k3p_ref, k3_ref, k3n_ref, v3p_ref, v3_ref, v3n_ref,
               b1_ref, b2_ref, b3_ref, o_ref, oacc, lacc, *, ch):
    c = pl.program_id(0)
    lo0 = jnp.where(hp_ref[c] != 0, 0, A_HALF)
    hi_cut = jnp.where(hn_ref[c] != 0, 0, A_HALF)

    def block_valid(bq, first, last):
        nk = bq + 2 * A_HALF
        return _dilf_valid(bq, lo0 if first else 0, nk - hi_cut if last else nk)

    def keep(g, rows, o, lse):
        oacc[g, rows, :] = o
        lacc[g, rows, :] = jnp.broadcast_to(lse, o.shape)

    def task(g, q_rows, k, v, bias_ref, valid, out_rows):
        scores = lambda: _dilf_scores(q_rows(), k, bias_ref[0], valid)
        finish = lambda s: keep(g, out_rows, *_dilf_finish(s, v))
        return scores, finish

    bq = DILF_BQ
    tasks = []
    nblk = ch // bq
    k = jnp.concatenate([k1p_ref[...], k1_ref[...], k1n_ref[...]], axis=0)
    v = jnp.concatenate([v1p_ref[...], v1_ref[...], v1n_ref[...]], axis=0)
    for b in range(nblk):
        rows = slice(b * bq, (b + 1) * bq)
        keys = slice(b * bq, (b + 1) * bq + 2 * A_HALF)
        tasks.append(task(0, lambda rows=rows: q1_ref[rows, :], k[keys], v[keys], b1_ref,
                          block_valid(bq, b == 0, b == nblk - 1), rows))
    dil = A_CONFIGS[1][1]
    lc = ch // dil
    nblk = lc // bq
    for r in range(dil):
        sub = lambda ref, n: ref[pl.ds(r, n, stride=dil), :].astype(BF16)
        k = jnp.concatenate([sub(k2p_ref, A_HALF), sub(k2_ref, lc), sub(k2n_ref, A_HALF)], axis=0)
        v = jnp.concatenate([sub(v2p_ref, A_HALF), sub(v2_ref, lc), sub(v2n_ref, A_HALF)], axis=0)
        for b in range(nblk):
            rows = pl.ds(r + b * bq * dil, bq, stride=dil)
            keys = slice(b * bq, (b + 1) * bq + 2 * A_HALF)
            tasks.append(task(1, lambda rows=rows: q2_ref[rows, :].astype(BF16), k[keys], v[keys],
                              b2_ref, block_valid(bq, b == 0, b == nblk - 1), rows))
    _pipelined(tasks)

    dil3 = A_CONFIGS[2][1]
    lc3 = ch // dil3
    valid3 = block_valid(lc3, True, True)

    def residues(i, carry):
        group = []
        for u in range(DILF_GROUP):
            r = i * DILF_GROUP + u
            sub = lambda ref, n, r=r: ref[pl.ds(r, n, stride=dil3), :].astype(BF16)
            k = jnp.concatenate([sub(k3p_ref, A_HALF), sub(k3_ref, lc3), sub(k3n_ref, A_HALF)],
                                axis=0)
            v = jnp.concatenate([sub(v3p_ref, A_HALF), sub(v3_ref, lc3), sub(v3n_ref, A_HALF)],
                                axis=0)
            rows = pl.ds(r, lc3, stride=dil3)
            group.append(task(2, lambda rows=rows: q3_ref[rows, :].astype(BF16), k, v, b3_ref,
                              valid3, rows))
        _pipelined(group)
        return carry
    lax.fori_loop(0, dil3 // DILF_GROUP, residues, 0)

    def combine(i, carry):
        rows = pl.ds(pl.multiple_of(i * DILF_BQ, DILF_BQ), DILF_BQ)
        l0, l1, l2 = lacc[0, rows, :], lacc[1, rows, :], lacc[2, rows, :]
        m = jnp.maximum(jnp.maximum(l0, l1), l2)
        e0, e1, e2 = jnp.exp(l0 - m), jnp.exp(l1 - m), jnp.exp(l2 - m)
        tot = e0 + e1 + e2
        out = (e0 / tot) * oacc[0, rows, :] + (e1 / tot) * oacc[1, rows, :] \
            + (e2 / tot) * oacc[2, rows, :]
        o_ref[rows, :] = out.astype(o_ref.dtype)
        return carry
    lax.fori_loop(0, ch // DILF_BQ, combine, 0)


def _dilated_fused(cfg, p1, p23, biases):
    n = cfg.n
    ch = cfg.ch
    hp, hn = _dilf_flags(cfg)
    nh = A_HEADS
    assert ch // A_CONFIGS[2][1] == 2 * A_HALF and ch % DILF_BQ == 0

    def specs(dil, qb, kb, vb):
        halo = A_HALF * dil
        per = ch // halo
        last = n // halo - 1
        own = lambda cb: pl.BlockSpec((ch, HEAD_DIM), lambda c, h, p, x: (c, cb + h))
        before = lambda cb: pl.BlockSpec(
            (halo, HEAD_DIM), lambda c, h, p, x: (jnp.maximum(c * per - 1, 0), cb + h))
        after = lambda cb: pl.BlockSpec(
            (halo, HEAD_DIM), lambda c, h, p, x: (jnp.minimum((c + 1) * per, last), cb + h))
        return [own(qb), before(kb), own(kb), after(kb), before(vb), own(vb), after(vb)]

    bias_spec = lambda b: pl.BlockSpec((1,) + b.shape[1:], lambda c, h, p, x: (h, 0, 0))
    grid_spec = pltpu.PrefetchScalarGridSpec(
        num_scalar_prefetch=2,
        grid=(n // ch, nh),
        in_specs=(specs(1, 0, nh, 2 * nh) + specs(A_CONFIGS[1][1], 0, 2 * nh, 4 * nh)
                  + specs(A_CONFIGS[2][1], nh, 3 * nh, 5 * nh)
                  + [bias_spec(b) for b in biases]),
        out_specs=pl.BlockSpec((ch, HEAD_DIM), lambda c, h, p, x: (c, h)),
        scratch_shapes=[pltpu.VMEM((3, ch, HEAD_DIM), F32), pltpu.VMEM((3, ch, HEAD_DIM), F32)],
    )
    return pl.pallas_call(
        functools.partial(_dilf_body, ch=ch),
        grid_spec=grid_spec,
        out_shape=jax.ShapeDtypeStruct((n, nh * HEAD_DIM), BF16),
        compiler_params=_params(("arbitrary", "arbitrary")),
        name="dilated_fused",
    )(hp, hn, *([p1] * 7), *([p23] * 14), *biases)


def _dilf_bias(rel_bias, g, dil, bq):
    off = np.arange(bq + 2 * A_HALF)[None, :] - np.arange(bq)[:, None] - A_HALF
    bucket = _t5_bucket_np(dil * np.clip(off, -A_HALF, A_HALF))
    return _expand(rel_bias[:, g * A_HEADS:(g + 1) * A_HEADS].T, bucket[None])[0]


def _flash_worklist(cfg, tq, tk, rel_lo=None, rel_hi=None):
    qb, kb, tile, flags = [], [], [], []
    for start, length in cfg.seqs():
        assert length % tq == 0 and length % tk == 0 and start % tq == 0 and start % tk == 0
        nk = length // tk
        for qi in range(length // tq):
            for kj in range(nk):
                qb.append(start // tq + qi)
                kb.append(start // tk + kj)
                if rel_lo is not None:
                    assert (kj * tk) % tq == 0
                    d = (kj * tk - qi * tq) // tq
                    tile.append(min(max(d, rel_lo), rel_hi) - rel_lo)
                    far = 4 if (d <= rel_lo or d >= rel_hi) else 0
                else:
                    tile.append(0)
                    far = 0
                flags.append((1 if kj == 0 else 0) | (2 if kj == nk - 1 else 0) | far)
    as_i32 = lambda a: jnp.asarray(np.asarray(a, np.int32))
    return as_i32(qb), as_i32(kb), as_i32(tile), as_i32(flags)


def _flash_init(flags, m_scr, acc_scr):
    @pl.when((flags & 1) != 0)
    def _():
        m_scr[...] = jnp.full(m_scr.shape, -jnp.inf, F32)
        acc_scr[...] = jnp.zeros(acc_scr.shape, F32)


def _flash_softmax(h, s, m_scr, shift=None):
    m_prev = m_scr[h]
    m_tile = jnp.max(s, axis=0, keepdims=True)
    if shift is not None:
        m_tile = m_tile + shift
    m_new = jnp.maximum(m_prev, m_tile)
    m_scr[h] = m_new
    m_ref = m_new if shift is None else m_new - shift
    return jnp.exp2(m_prev - m_new), jnp.exp2(s - m_ref).astype(BF16)


def _flash_accumulate(h, alpha, p, vt, acc_scr):
    lhs = jnp.concatenate([vt, jnp.ones((ONES_ROWS, vt.shape[1]), BF16)], axis=0)
    acc_scr[h] = alpha * acc_scr[h] + jnp.dot(lhs, p, preferred_element_type=F32)


def _flash_heads(heads, scores, vt_rows, m_scr, acc_scr, shift=None):
    s_next = scores(0)
    pending = None
    for h in range(heads):
        s_cur = s_next
        if h + 1 < heads:
            s_next = scores(h + 1)
        alpha, p = _flash_softmax(h, s_cur, m_scr, None if shift is None else shift(h))
        if pending is not None:
            _flash_accumulate(*pending, vt_rows(pending[0]), acc_scr)
        pending = (h, alpha, p)
    _flash_accumulate(*pending, vt_rows(pending[0]), acc_scr)


DIFF_TQ = 512
DIFF_TK = 512


def _diff_tile_range(max_len):
    rel = np.arange(-max_len + 1, max_len)
    b = _t5_bucket_np(rel)
    sat_pos = int(rel[b != b[-1]].max()) + 1
    sat_neg = int(rel[b != b[0]].min()) - 1
    hi = -(-(sat_pos + DIFF_TQ - 1) // DIFF_TQ)
    lo = (sat_neg - (DIFF_TK - 1)) // DIFF_TQ
    return lo, hi


def _diff_bias_tiles(rel_bias, lo, hi):
    d = np.arange(lo, hi + 1)[:, None, None] * DIFF_TQ
    rel = d + np.arange(DIFF_TK)[None, :, None] - np.arange(DIFF_TQ)[None, None, :]
    return _expand(rel_bias[:, 3 * A_HEADS:].T * LOG2E, _t5_bucket_np(rel))


def _diff_body(qb_ref, kb_ref, tile_ref, fl_ref, q_ref, k_ref, vt_ref, bias_ref, lam_ref, g_ref,
               o_ref, m_scr, acc_scr, *, lam_init):
    flags = fl_ref[pl.program_id(0)]
    tq = DIFF_TQ
    _flash_init(flags, m_scr, acc_scr)

    lane = lax.broadcasted_iota(jnp.int32, (tq, HEAD_DIM), 1)

    def raw_scores(h):
        cols = slice(h * HEAD_DIM, (h + 1) * HEAD_DIM)
        qh = q_ref[:, cols]
        zero = jnp.zeros_like(qh)
        q2 = jnp.concatenate([jnp.where(lane < B_QK_DIM, qh, zero),
                              jnp.where(lane >= B_QK_DIM, qh, zero)], axis=0)
        return lax.dot_general(k_ref[:, cols], q2, (((1,), (1,)), ((), ())),
                               preferred_element_type=F32)

    def biased_scores(h):
        b = bias_ref[0, h]
        return raw_scores(h) + jnp.concatenate([b, b], axis=1)

    vt_rows = lambda h: vt_ref[h * HEAD_DIM:(h + 1) * HEAD_DIM, :]
    far = (flags & 4) != 0

    @pl.when(far)
    def _():
        _flash_heads(B_HEADS, raw_scores, vt_rows, m_scr, acc_scr,
                     shift=lambda h: bias_ref[0, h, 0:1, 0:1])

    @pl.when(jnp.logical_not(far))
    def _():
        _flash_heads(B_HEADS, biased_scores, vt_rows, m_scr, acc_scr)

    @pl.when((flags & 2) != 0)
    def _():
        lf = lam_ref[...]
        lam = (jnp.exp(jnp.sum(lf[0:1] * lf[1:2], axis=-1, keepdims=True))
               - jnp.exp(jnp.sum(lf[2:3] * lf[3:4], axis=-1, keepdims=True)) + lam_init)
        for h in range(B_HEADS):
            cols = slice(h * HEAD_DIM, (h + 1) * HEAD_DIM)
            acc = acc_scr[h]
            att = acc[:HEAD_DIM] / acc[HEAD_DIM:HEAD_DIM + 1]
            o = att[:, :tq] - lam * att[:, tq:]
            inv = lax.rsqrt(jnp.mean(o * o, axis=0, keepdims=True) + EPS)
            o_ref[:, cols] = (((o * inv).T * g_ref[...]) * (1.0 - lam_init)).astype(o_ref.dtype)


def _diff_attention(cfg, proj, qcol, kcol, vt, bias_tiles, tile_lo, tile_hi, lam, subln_g,
                    lam_init):
    n = cfg.n
    tq, tk = DIFF_TQ, DIFF_TK
    qb, kb, tile, flags = _flash_worklist(cfg, tq, tk, tile_lo, tile_hi)
    grid_spec = pltpu.PrefetchScalarGridSpec(
        num_scalar_prefetch=4,
        grid=(qb.shape[0],),
        in_specs=[pl.BlockSpec((tq, B_W), lambda s, q, k, t, f: (q[s], qcol)),
                  pl.BlockSpec((tk, B_W), lambda s, q, k, t, f: (k[s], kcol)),
                  pl.BlockSpec((B_W, tk), lambda s, q, k, t, f: (0, k[s])),
                  pl.BlockSpec((1, B_HEADS, tk, tq), lambda s, q, k, t, f: (t[s], 0, 0, 0)),
                  pl.BlockSpec((4, B_QK_DIM), lambda s, q, k, t, f: (0, 0)),
                  pl.BlockSpec((1, HEAD_DIM), lambda s, q, k, t, f: (0, 0))],
        out_specs=pl.BlockSpec((tq, B_W), lambda s, q, k, t, f: (q[s], 0)),
        scratch_shapes=[pltpu.VMEM((B_HEADS, 1, 2 * tq), F32),
                        pltpu.VMEM((B_HEADS, HEAD_DIM + ONES_ROWS, 2 * tq), F32)],
    )
    return pl.pallas_call(
        functools.partial(_diff_body, lam_init=lam_init),
        grid_spec=grid_spec,
        out_shape=jax.ShapeDtypeStruct((n, B_W), BF16),
        compiler_params=_params(("arbitrary",)),
        name="diff_attention",
    )(qb, kb, tile, flags, proj, proj, vt, bias_tiles, lam, subln_g.reshape(1, HEAD_DIM))


MLA_TM = 512
MLA_TQ = 1024
MLA_TK = 512


def _rope_tables(max_len):
    inv = ROPE_THETA ** (-jnp.arange(0, C_ROPE, 2, dtype=F32) / C_ROPE)
    ang = jnp.arange(max_len, dtype=F32)[:, None] * inv[None, :]
    cos, sin = jnp.cos(ang), jnp.sin(ang)
    zero = jnp.zeros((max_len, HEAD_DIM - C_ROPE), F32)
    return (jnp.concatenate([cos, cos, zero], axis=1),
            jnp.concatenate([-sin, sin, zero], axis=1))


def _mla_prep_body(p_ref, gq_ref, gkv_ref, wqa_ref, wqb_ref, wk_ref, wvt_ref, cos_ref, sin_ref,
                   q_ref, k_ref, vt_ref, *, scale):
    def normed(x, g):
        inv = lax.rsqrt(jnp.mean(x * x, axis=-1, keepdims=True) + EPS)
        return (x * inv * g).astype(BF16)

    cq = normed(p_ref[:, 0:C_LORA], gq_ref[...])
    ckv = normed(p_ref[:, C_LORA:2 * C_LORA], gkv_ref[...])
    cos = cos_ref[...]
    sin = sin_ref[...]
    qa = jnp.dot(cq, wqa_ref[...], preferred_element_type=F32)
    qb = jnp.dot(cq, wqb_ref[...], preferred_element_type=F32)
    kn = jnp.dot(ckv, wk_ref[...], preferred_element_type=F32)
    vt_ref[...] = lax.dot_general(wvt_ref[...], ckv, (((1,), (1,)), ((), ())),
                                  preferred_element_type=F32).astype(BF16)
    kr = p_ref[:, 2 * C_LORA:2 * C_LORA + HEAD_DIM]
    kr_sw = p_ref[:, 2 * C_LORA + HEAD_DIM:2 * C_LORA + 2 * HEAD_DIM]
    k_rope = (kr * cos + kr_sw * sin).astype(BF16)
    for h in range(C_HEADS):
        a0 = h * C_QK_PAD
        a1 = a0 + HEAD_DIM
        a2 = a0 + C_QK_PAD
        b = slice(h * HEAD_DIM, (h + 1) * HEAD_DIM)
        q_ref[:, a0:a1] = (qa[:, a0:a1] * scale).astype(BF16)
        q_ref[:, a1:a2] = ((qa[:, a1:a2] * cos + qb[:, b] * sin) * scale).astype(BF16)
        k_ref[:, a0:a1] = kn[:, b].astype(BF16)
        k_ref[:, a1:a2] = k_rope


def _mla_prep(cfg, proj_c, gq, gkv, wqa, wqb, wk, wvt, cos_tab, sin_tab):
    n = cfg.n
    tm = MLA_TM
    pblocks = cfg.pc * cfg.ch // tm
    per = cfg.ch // tm
    pos = lambda i: (jnp.where(i < pblocks, i, (i - pblocks) % per), 0)
    full = lambda shape: pl.BlockSpec(shape, lambda i: (0, 0))
    qk_w = C_HEADS * C_QK_PAD
    v_w = C_HEADS * HEAD_DIM
    return pl.pallas_call(
        functools.partial(_mla_prep_body, scale=(C_NOPE + C_ROPE) ** -0.5 * LOG2E),
        grid=(n // tm,),
        in_specs=[pl.BlockSpec((tm, proj_c.shape[1]), lambda i: (i, 0)),
                  full((1, C_LORA)), full((1, C_LORA)),
                  full(wqa.shape), full(wqb.shape), full(wk.shape), full(wvt.shape),
                  pl.BlockSpec((tm, HEAD_DIM), pos), pl.BlockSpec((tm, HEAD_DIM), pos)],
        out_specs=[pl.BlockSpec((tm, qk_w), lambda i: (i, 0)),
                   pl.BlockSpec((tm, qk_w), lambda i: (i, 0)),
                   pl.BlockSpec((v_w, tm), lambda i: (0, i))],
        out_shape=[jax.ShapeDtypeStruct((n, qk_w), BF16),
                   jax.ShapeDtypeStruct((n, qk_w), BF16),
                   jax.ShapeDtypeStruct((v_w, n), BF16)],
        compiler_params=_params(("arbitrary",)),
        name="mla_prep",
    )(proj_c, gq.reshape(1, C_LORA), gkv.reshape(1, C_LORA), wqa, wqb, wk, wvt, cos_tab, sin_tab)


def _mla_body(qb_ref, kb_ref, tile_ref, fl_ref, q_ref, k_ref, vt_ref, o_ref, m_scr, acc_scr):
    flags = fl_ref[pl.program_id(0)]
    _flash_init(flags, m_scr, acc_scr)

    def scores(h):
        qk = slice(h * C_QK_PAD, (h + 1) * C_QK_PAD)
        return lax.dot_general(k_ref[:, qk], q_ref[:, qk], (((1,), (1,)), ((), ())),
                               preferred_element_type=F32)

    vt_rows = lambda h: vt_ref[h * HEAD_DIM:(h + 1) * HEAD_DIM, :]
    _flash_heads(C_HEADS, scores, vt_rows, m_scr, acc_scr)

    @pl.when((flags & 2) != 0)
    def _():
        for h in range(C_HEADS):
            vc = slice(h * HEAD_DIM, (h + 1) * HEAD_DIM)
            acc = acc_scr[h]
            o_ref[:, vc] = (acc[:HEAD_DIM] / acc[HEAD_DIM:HEAD_DIM + 1]).T.astype(o_ref.dtype)


def _mla_attention(cfg, q, k, vt):
    n = cfg.n
    tq, tk = MLA_TQ, MLA_TK
    qb, kb, tile, flags = _flash_worklist(cfg, tq, tk)
    qk_w = C_HEADS * C_QK_PAD
    v_w = C_HEADS * HEAD_DIM
    grid_spec = pltpu.PrefetchScalarGridSpec(
        num_scalar_prefetch=4,
        grid=(qb.shape[0],),
        in_specs=[pl.BlockSpec((tq, qk_w), lambda s, q_, k_, t, f: (q_[s], 0)),
                  pl.BlockSpec((tk, qk_w), lambda s, q_, k_, t, f: (k_[s], 0)),
                  pl.BlockSpec((v_w, tk), lambda s, q_, k_, t, f: (0, k_[s]))],
        out_specs=pl.BlockSpec((tq, v_w), lambda s, q_, k_, t, f: (q_[s], 0)),
        scratch_shapes=[pltpu.VMEM((C_HEADS, 1, tq), F32),
                        pltpu.VMEM((C_HEADS, HEAD_DIM + ONES_ROWS, tq), F32)],
    )
    return pl.pallas_call(
        _mla_body,
        grid_spec=grid_spec,
        out_shape=jax.ShapeDtypeStruct((n, v_w), BF16),
        compiler_params=_params(("arbitrary",)),
        name="mla_attention",
    )(qb, kb, tile, flags, q, k, vt)


def _na_tables(cfg):
    prev, nxt, var = [], [], []
    for start, length in cfg.seqs():
        nb = length // NA_BLOCK
        assert length % NA_BLOCK == 0 and start % NA_BLOCK == 0 and nb >= 3
        b0 = start // NA_BLOCK
        for r in range(nb):
            prev.append(b0 + max(r - 1, 0))
            nxt.append(b0 + min(r + 1, nb - 1))
            var.append(0 if r == 0 else (2 if r == nb - 1 else 1))
    as_i32 = lambda a: jnp.asarray(np.asarray(a, np.int32))
    return as_i32(prev), as_i32(nxt), as_i32(var)


def _na_window_start(a, variant):
    centred = a - NA_ROWS // 2
    return (max(centred, 0), centred, min(centred, 0))[variant]


def _na_bias_tables(rpb):
    ncol = 2 * NA_COLS - 1
    c = np.arange(GRID_W)[:, None, None]
    b = np.arange(NA_ROWS)[None, :, None]
    kc = np.arange(GRID_W)[None, None, :]
    cstart = np.clip(c - NA_COLS // 2, 0, GRID_W - NA_COLS)
    col_ok = (kc >= cstart) & (kc < cstart + NA_COLS)
    dc = np.clip(kc - c, -(NA_COLS - 1), NA_COLS - 1) + (NA_COLS - 1)
    masked = NA_ROWS * ncol
    idx = np.where(col_ok, b * ncol + dc, masked).reshape(1, GRID_W, NA_BLOCK)
    rows = np.arange(NA_ROWS)[:, None] + np.arange(NA_ROWS)[None, :]
    tab = rpb[:, rows, :].reshape(D_HEADS * NA_ROWS, NA_ROWS * ncol)
    tab = jnp.concatenate([tab, jnp.full((tab.shape[0], 1), NEG, F32)], axis=1)
    slabs = _expand(tab, idx)[0].reshape(D_HEADS, NA_ROWS, GRID_W, NA_BLOCK)
    variants = []
    for variant in range(3):
        row_blocks = []
        for a in range(NA_ROWS):
            start = _na_window_start(a, variant)
            left = (NA_ROWS + start) * GRID_W
            row_blocks.append(jnp.pad(slabs[:, start - a + NA_ROWS - 1],
                                      ((0, 0), (0, 0), (left, 2 * NA_BLOCK - left)),
                                      constant_values=NEG))
        variants.append(jnp.concatenate(row_blocks, axis=1))
    return jnp.stack(variants, axis=0)


def _na_body(prev_ref, nxt_ref, var_ref, q_ref, kp_ref, km_ref, kn_ref, vp_ref, vm_ref, vn_ref,
             tab_ref, o_ref):
    k = jnp.concatenate([kp_ref[...], km_ref[...], kn_ref[...]], axis=0)
    v = jnp.concatenate([vp_ref[...], vm_ref[...], vn_ref[...]], axis=0)
    def scores(h):
        cols = slice(h * HEAD_DIM, (h + 1) * HEAD_DIM)
        s = lax.dot_general(q_ref[:, cols], k[:, cols], (((1,), (1,)), ((), ())),
                            preferred_element_type=F32)
        return s + tab_ref[0, h]

    def finish(h, s):
        cols = slice(h * HEAD_DIM, (h + 1) * HEAD_DIM)
        m = jnp.max(s, axis=-1, keepdims=True)
        e = jnp.exp(s - m)
        den = jnp.sum(e, axis=-1, keepdims=True)
        o = jnp.dot(e.astype(BF16), v[:, cols], preferred_element_type=F32)
        o_ref[:, cols] = (o / den).astype(o_ref.dtype)

    _banded_heads(D_HEADS, scores, finish)


def _na_attention(cfg, qkv, tabs):
    n = cfg.n
    w = D_HEADS * HEAD_DIM
    prev, nxt, var = _na_tables(cfg)
    own = lambda cb: pl.BlockSpec((NA_BLOCK, w), lambda i, p, x, t: (i, cb))
    before = lambda cb: pl.BlockSpec((NA_BLOCK, w), lambda i, p, x, t: (p[i], cb))
    after = lambda cb: pl.BlockSpec((NA_BLOCK, w), lambda i, p, x, t: (x[i], cb))
    grid_spec = pltpu.PrefetchScalarGridSpec(
        num_scalar_prefetch=3,
        grid=(n // NA_BLOCK,),
        in_specs=[own(0), before(1), own(1), after(1), before(2), own(2), after(2),
                  pl.BlockSpec((1, D_HEADS, NA_BLOCK, 3 * NA_BLOCK),
                               lambda i, p, x, t: (t[i], 0, 0, 0))],
        out_specs=pl.BlockSpec((NA_BLOCK, w), lambda i, p, x, t: (i, 0)),
    )
    return pl.pallas_call(
        _na_body,
        grid_spec=grid_spec,
        out_shape=jax.ShapeDtypeStruct((n, w), BF16),
        compiler_params=_params(("arbitrary",)),
        name="na_attention",
    )(prev, nxt, var, qkv, qkv, qkv, qkv, qkv, qkv, qkv, tabs)


EVEN_BLOCKS_ROWMAJOR = (0, 3, 6, 9, 10)
EVEN_BLOCKS_STRIDED = (1, 2, 4, 5, 7, 8)
EVEN_BLOCK_VT = 11


def _even_weights(w_in):
    blk = lambda b: w_in[:, b * B_W:(b + 1) * B_W]
    w_a = jnp.concatenate([blk(b) for b in EVEN_BLOCKS_ROWMAJOR], axis=1)
    w_b = jnp.concatenate([blk(b) for b in EVEN_BLOCKS_STRIDED], axis=1)
    return w_a, w_b, blk(EVEN_BLOCK_VT).T


def _even_col_scales():
    cs_a = np.ones((1, len(EVEN_BLOCKS_ROWMAJOR) * B_W), np.float32)
    cs_a[0, :B_W] = HEAD_DIM ** -0.5
    cs_a[0, 3 * B_W:4 * B_W] = B_QK_DIM ** -0.5 * LOG2E
    cs_b = np.ones((1, len(EVEN_BLOCKS_STRIDED) * B_W), np.float32)
    cs_b[0, :2 * B_W] = HEAD_DIM ** -0.5
    return jnp.asarray(cs_a), jnp.asarray(cs_b)


def _even_tables(cfg, rel_bias):
    dil = [_dilf_bias(rel_bias, gi, d, min(DILF_BQ, cfg.ch // d))
           for gi, (_, d) in enumerate(A_CONFIGS)]
    tile_lo, tile_hi = _diff_tile_range(cfg.pc * cfg.ch)
    return dil, _diff_bias_tiles(rel_bias, tile_lo, tile_hi), tile_lo, tile_hi


def _even_mixer(cfg, x, g, mods, weights, w_out, lam, subln_g, tables, lam_init, tm):
    w_a, w_b, wvt = weights
    dil_bias, diff_tiles, tile_lo, tile_hi = tables
    cs_a, cs_b = _even_col_scales()
    p1 = _norm_matmul(cfg, x, g, mods, 1, w_a, cs_a, BF16, tm, 1024, "even_in_a")
    p23 = _norm_matmul(cfg, x, g, mods, 1, w_b, cs_b, F32, tm, 1024, "even_in_b")
    vt = _norm_matmul_t(cfg, x, g, mods, 1, wvt, tm, "even_in_vt")
    o_a = _dilated_fused(cfg, p1, p23, dil_bias)
    o_b = _diff_attention(cfg, p1, 3, 4, vt, diff_tiles, tile_lo, tile_hi, lam, subln_g, lam_init)
    return _resid_matmul(cfg, [o_a, o_b], w_out, x, mods, 1, 1.0, tm, 1024, "even_out")


def _odd_weights(w_in, w_q_up, w_kv_up):
    half = C_ROPE // 2
    swap = np.concatenate([np.arange(half, C_ROPE), np.arange(half)])
    o2 = 2 * C_LORA
    zpad = jnp.zeros((D_MODEL, HEAD_DIM - C_ROPE), w_in.dtype)
    kr = w_in[:, o2:o2 + C_ROPE]
    w_c = jnp.concatenate([w_in[:, :o2], kr, zpad, kr[:, swap], zpad], axis=1)
    w_d = w_in[:, o2 + C_ROPE:]
    q3 = w_q_up.reshape(C_LORA, C_HEADS, C_NOPE + C_ROPE)
    zq = jnp.zeros((C_LORA, C_HEADS, C_QK_PAD - C_NOPE - C_ROPE), w_q_up.dtype)
    wqa = jnp.concatenate([q3, zq], axis=2).reshape(C_LORA, C_HEADS * C_QK_PAD)
    wqb = jnp.concatenate([q3[:, :, C_NOPE:][:, :, swap], zq], axis=2).reshape(
        C_LORA, C_HEADS * HEAD_DIM)
    kv3 = w_kv_up.reshape(C_LORA, C_HEADS, 2 * HEAD_DIM)
    wk = kv3[:, :, :C_NOPE].reshape(C_LORA, C_HEADS * HEAD_DIM)
    wvt = kv3[:, :, C_NOPE:].reshape(C_LORA, C_HEADS * HEAD_DIM).T
    return w_c, w_d, wqa, wqb, wk, wvt


def _odd_mixer(cfg, x, g, mods, weights, gq, gkv, rpb, w_out, rope, tm):
    w_c, w_d, wqa, wqb, wk, wvt = weights
    ones_c = jnp.ones((1, w_c.shape[1]), F32)
    proj_c = _norm_matmul(cfg, x, g, mods, 1, w_c, ones_c, F32, tm, w_c.shape[1], "odd_in_latent")
    cs = np.ones((1, w_d.shape[1]), np.float32)
    cs[0, :D_HEADS * HEAD_DIM] = HEAD_DIM ** -0.5
    qkv_d = _norm_matmul(cfg, x, g, mods, 1, w_d, jnp.asarray(cs), BF16, tm, w_d.shape[1],
                         "odd_in_na")
    q, k, vt = _mla_prep(cfg, proj_c, gq, gkv, wqa, wqb, wk, wvt, *rope)
    o_c = _mla_attention(cfg, q, k, vt)
    o_d = _na_attention(cfg, qkv_d, _na_bias_tables(rpb))
    return _resid_matmul(cfg, [o_c, o_d], w_out, x, mods, 1, 1.0, tm, 1024, "odd_out")


def _trunk(cfg, x, c_pad, ada_w, ada_b, norm_g, ffn_w_in, ffn_w_out, rel_bias, ev_w_in, ev_w_out,
           diff_lambda, diff_subln_g, od_w_in, mla_q_norm_g, mla_kv_norm_g, mla_w_q_up,
           mla_w_kv_up, na_rpb, od_w_out, final_norm_g):
    depth = ada_w.shape[0]
    tm = 1024
    nseq = 1 + cfg.sb
    mod_all = _modulation(c_pad, ada_w, ada_b)[:, :nseq]
    chunk_seq = np.array([0] * cfg.pc + list(range(1, nseq)))
    mod_all = mod_all[:, chunk_seq].reshape(depth, (cfg.pc + cfg.sb) * 9, 1, D_MODEL)
    rope = _rope_tables(cfg.pc * cfg.ch)
    even_tables = _even_tables(cfg, rel_bias)
    for i in range(depth):
        mods = mod_all[i]
        j = i // 2
        hid = _norm_swiglu(cfg, x, norm_g[i, 0], mods, 0, ffn_w_in[i, 0].astype(BF16), tm, 512)
        x = _resid_matmul(cfg, [hid], ffn_w_out[i, 0].astype(BF16), x, mods, 0, 0.5, tm, 512,
                          "ffn_out")
        if i % 2 == 0:
            x = _even_mixer(cfg, x, norm_g[i, 1], mods, _even_weights(ev_w_in[j].astype(BF16)),
                            ev_w_out[j].astype(BF16), diff_lambda[j], diff_subln_g[j], even_tables,
                            0.8 - 0.6 * math.exp(-0.3 * i), tm)
        else:
            weights = tuple(a.astype(BF16) for a in
                            _odd_weights(od_w_in[j], mla_w_q_up[j], mla_w_kv_up[j]))
            x = _odd_mixer(cfg, x, norm_g[i, 1], mods, weights, mla_q_norm_g[j], mla_kv_norm_g[j],
                           na_rpb[j], od_w_out[j].astype(BF16), rope, tm)
        hid = _norm_swiglu(cfg, x, norm_g[i, 2], mods, 2, ffn_w_in[i, 1].astype(BF16), tm, 512)
        x = _resid_matmul(cfg, [hid], ffn_w_out[i, 1].astype(BF16), x, mods, 2, 0.5, tm, 512,
                          "ffn_out")
    return _final_norm(cfg, x, final_norm_g)


def kernel(x_prompt, x_sample, c_prompt, c_sample, ada_w, ada_b, norm_g, ffn_w_in, ffn_w_out, rel_bias, ev_w_in, ev_w_out, diff_lambda, diff_subln_g, od_w_in, mla_q_norm_g, mla_kv_norm_g, mla_w_q_up, mla_w_kv_up, na_rpb, od_w_out, final_norm_g):
    pb, pt, _ = x_prompt.shape
    sb, st, _ = x_sample.shape
    assert pb == 1 and pt % st == 0
    cfg = Cfg(ch=st, pc=pt // st, sb=sb)
    x = jnp.concatenate([x_prompt.reshape(-1, D_MODEL), x_sample.reshape(-1, D_MODEL)], axis=0)
    c = jnp.concatenate([c_prompt, c_sample], axis=0)
    c_pad = jnp.pad(c, ((0, -c.shape[0] % 8), (0, 0)))
    y_p, y_s = _trunk(cfg, x, c_pad, ada_w, ada_b, norm_g, ffn_w_in, ffn_w_out, rel_bias, ev_w_in,
               ev_w_out, diff_lambda, diff_subln_g, od_w_in, mla_q_norm_g, mla_kv_norm_g,
               mla_w_q_up, mla_w_kv_up, na_rpb, od_w_out, final_norm_g)
    return (y_p.reshape(x_prompt.shape), y_s.reshape(x_sample.shape))
```

```python
import functools
import math
from typing import NamedTuple

import numpy as np
import jax
import jax.numpy as jnp
from jax import lax
from jax.experimental import pallas as pl
from jax.experimental.pallas import tpu as pltpu

F32 = jnp.float32
BF16 = jnp.bfloat16

D_MODEL = 2048
D_FF = 5632
HEAD_DIM = 128
A_HEADS = 8
A_CONFIGS = ((128, 1), (512, 4), (2048, 16))
A_HALF = 64
A_IN = 3 * 3 * A_HEADS * HEAD_DIM
B_HEADS = 8
B_QK_DIM = 64
B_W = B_HEADS * 2 * B_QK_DIM
EVEN_IN = A_IN + 3 * B_W
C_HEADS = 12
C_LORA = 512
C_NOPE = 128
C_ROPE = 64
C_QK_PAD = 256
ROPE_THETA = 10000.0
D_HEADS = 4
GRID_W = 64
NA_ROWS = 8
NA_COLS = 16
NA_BLOCK = NA_ROWS * GRID_W
REL_BUCKETS = 32
REL_MAX_DIST = 1024
EPS = 1e-6
NEG = -1e30
LOG2E = math.log2(math.e)
ONES_ROWS = 16

V7X_VMEM_BYTES = 64 * 1024 * 1024
VMEM_LIMIT = V7X_VMEM_BYTES - 8 * 1024 * 1024


class Cfg(NamedTuple):
    ch: int
    pc: int
    sb: int

    @property
    def n(self):
        return self.ch * (self.pc + self.sb)

    def seqs(self):
        out = [(0, self.pc * self.ch)]
        out += [((self.pc + b) * self.ch, self.ch) for b in range(self.sb)]
        return out


def _params(sem):
    return pltpu.CompilerParams(dimension_semantics=sem, vmem_limit_bytes=VMEM_LIMIT)


def _mod_body(c_ref, w_ref, b_ref, o_ref):
    c = c_ref[...]
    act = (c * jax.nn.sigmoid(c)).astype(BF16)
    o_ref[0] = jnp.dot(act, w_ref[0].astype(BF16), preferred_element_type=F32) + b_ref[0]


def _modulation(c_pad, ada_w, ada_b):
    depth, _, nout = ada_w.shape
    r = c_pad.shape[0]
    tn = 1024
    return pl.pallas_call(
        _mod_body,
        grid=(depth, nout // tn),
        in_specs=[pl.BlockSpec((r, D_MODEL), lambda l, j: (0, 0)),
                  pl.BlockSpec((1, D_MODEL, tn), lambda l, j: (l, 0, j)),
                  pl.BlockSpec((1, 1, tn), lambda l, j: (l, 0, j))],
        out_specs=pl.BlockSpec((1, r, tn), lambda l, j: (l, 0, j)),
        out_shape=jax.ShapeDtypeStruct((depth, r, nout), F32),
        compiler_params=_params(("arbitrary", "arbitrary")),
        name="modulation",
    )(c_pad, ada_w, ada_b.reshape(depth, 1, nout))


NORM_ROWS = 32


def _norm_rows(x_ref, g_ref, sh_ref, sc_ref, h_scr, tm):
    gain = g_ref[...] * (1.0 + sc_ref[0])
    shift = sh_ref[0]

    def body(r, carry):
        rows = pl.ds(pl.multiple_of(r * NORM_ROWS, NORM_ROWS), NORM_ROWS)
        x = x_ref[rows, :]
        inv = lax.rsqrt(jnp.mean(x * x, axis=-1, keepdims=True) + EPS)
        h_scr[rows, :] = ((x * inv) * gain + shift).astype(BF16)
        return carry
    lax.fori_loop(0, tm // NORM_ROWS, body, 0, unroll=4)


def _norm_mm_body(x_ref, g_ref, sh_ref, sc_ref, w_ref, cs_ref, o_ref, h_scr, *, tm):
    @pl.when(pl.program_id(1) == 0)
    def _():
        _norm_rows(x_ref, g_ref, sh_ref, sc_ref, h_scr, tm)
    acc = jnp.dot(h_scr[...], w_ref[...], preferred_element_type=F32)
    o_ref[...] = (acc * cs_ref[...]).astype(o_ref.dtype)


def _norm_mm_t_body(x_ref, g_ref, sh_ref, sc_ref, wt_ref, o_ref, h_scr, *, tm):
    _norm_rows(x_ref, g_ref, sh_ref, sc_ref, h_scr, tm)
    acc = lax.dot_general(wt_ref[...], h_scr[...], (((1,), (1,)), ((), ())),
                          preferred_element_type=F32)
    o_ref[...] = acc.astype(o_ref.dtype)


def _norm_swiglu_body(x_ref, g_ref, sh_ref, sc_ref, wg_ref, wu_ref, o_ref, h_scr, *, tm):
    @pl.when(pl.program_id(1) == 0)
    def _():
        _norm_rows(x_ref, g_ref, sh_ref, sc_ref, h_scr, tm)
    h = h_scr[...]
    gate = jnp.dot(h, wg_ref[...], preferred_element_type=F32)
    up = jnp.dot(h, wu_ref[...], preferred_element_type=F32)
    o_ref[...] = (gate * jax.nn.sigmoid(gate) * up).astype(o_ref.dtype)


def _mod_specs(cfg, tm, sub):
    per = cfg.ch // tm
    shift = pl.BlockSpec((1, 1, D_MODEL), lambda i, *_: ((i // per) * 9 + sub * 3, 0, 0))
    scale = pl.BlockSpec((1, 1, D_MODEL), lambda i, *_: ((i // per) * 9 + sub * 3 + 1, 0, 0))
    return shift, scale


def _norm_matmul(cfg, x, g, mods, sub, w, col_scale, out_dtype, tm, tn, name):
    n = x.shape[0]
    nout = w.shape[1]
    shift, scale = _mod_specs(cfg, tm, sub)
    return pl.pallas_call(
        functools.partial(_norm_mm_body, tm=tm),
        grid=(n // tm, nout // tn),
        in_specs=[pl.BlockSpec((tm, D_MODEL), lambda i, j: (i, 0)),
                  pl.BlockSpec((1, D_MODEL), lambda i, j: (0, 0)),
                  shift, scale,
                  pl.BlockSpec((D_MODEL, tn), lambda i, j: (0, j)),
                  pl.BlockSpec((1, tn), lambda i, j: (0, j))],
        out_specs=pl.BlockSpec((tm, tn), lambda i, j: (i, j)),
        out_shape=jax.ShapeDtypeStruct((n, nout), out_dtype),
        scratch_shapes=[pltpu.VMEM((tm, D_MODEL), BF16)],
        compiler_params=_params(("arbitrary", "arbitrary")),
        name=name,
    )(x, g.reshape(1, D_MODEL), mods, mods, w, col_scale)


def _norm_matmul_t(cfg, x, g, mods, sub, wt, tm, name):
    n = x.shape[0]
    nout = wt.shape[0]
    shift, scale = _mod_specs(cfg, tm, sub)
    return pl.pallas_call(
        functools.partial(_norm_mm_t_body, tm=tm),
        grid=(n // tm,),
        in_specs=[pl.BlockSpec((tm, D_MODEL), lambda i: (i, 0)),
                  pl.BlockSpec((1, D_MODEL), lambda i: (0, 0)),
                  shift, scale,
                  pl.BlockSpec((nout, D_MODEL), lambda i: (0, 0))],
        out_specs=pl.BlockSpec((nout, tm), lambda i: (0, i)),
        out_shape=jax.ShapeDtypeStruct((nout, n), BF16),
        scratch_shapes=[pltpu.VMEM((tm, D_MODEL), BF16)],
        compiler_params=_params(("arbitrary",)),
        name=name,
    )(x, g.reshape(1, D_MODEL), mods, mods, wt)


def _norm_swiglu(cfg, x, g, mods, sub, w_in, tm, tn):
    n = x.shape[0]
    nj = D_FF // tn
    shift, scale = _mod_specs(cfg, tm, sub)
    return pl.pallas_call(
        functools.partial(_norm_swiglu_body, tm=tm),
        grid=(n // tm, nj),
        in_specs=[pl.BlockSpec((tm, D_MODEL), lambda i, j: (i, 0)),
                  pl.BlockSpec((1, D_MODEL), lambda i, j: (0, 0)),
                  shift, scale,
                  pl.BlockSpec((D_MODEL, tn), lambda i, j: (0, j)),
                  pl.BlockSpec((D_MODEL, tn), lambda i, j: (0, j + nj))],
        out_specs=pl.BlockSpec((tm, tn), lambda i, j: (i, j)),
        out_shape=jax.ShapeDtypeStruct((n, D_FF), BF16),
        scratch_shapes=[pltpu.VMEM((tm, D_MODEL), BF16)],
        compiler_params=_params(("arbitrary", "arbitrary")),
        name="ffn_in",
    )(x, g.reshape(1, D_MODEL), mods, mods, w_in, w_in)


def _resid_mm_body(*refs, pieces, coef):
    lhs = refs[:pieces]
    ws = refs[pieces:2 * pieces]
    x_ref, gate_ref, o_ref = refs[2 * pieces:]
    acc = jnp.dot(lhs[0][...], ws[0][...], preferred_element_type=F32)
    for p in range(1, pieces):
        acc = acc + jnp.dot(lhs[p][...], ws[p][...], preferred_element_type=F32)
    o_ref[...] = x_ref[...] + (coef * gate_ref[0]) * acc


def _resid_matmul(cfg, lhs_list, w, x, mods, sub, coef, tm, tn, name):
    n = x.shape[0]
    per = cfg.ch // tm
    widths = [a.shape[1] for a in lhs_list]
    offs = np.cumsum([0] + widths[:-1])
    in_specs = [pl.BlockSpec((tm, k), lambda i, j: (i, 0)) for k in widths]
    for k, off in zip(widths, offs):
        assert off % k == 0
        in_specs.append(pl.BlockSpec((k, tn), lambda i, j, b=int(off // k): (b, j)))
    in_specs += [pl.BlockSpec((tm, tn), lambda i, j: (i, j)),
                 pl.BlockSpec((1, 1, tn), lambda i, j: ((i // per) * 9 + sub * 3 + 2, 0, j))]
    return pl.pallas_call(
        functools.partial(_resid_mm_body, pieces=len(lhs_list), coef=coef),
        grid=(n // tm, D_MODEL // tn),
        in_specs=in_specs,
        out_specs=pl.BlockSpec((tm, tn), lambda i, j: (i, j)),
        out_shape=jax.ShapeDtypeStruct((n, D_MODEL), F32),
        compiler_params=_params(("arbitrary", "arbitrary")),
        name=name,
    )(*lhs_list, *([w] * len(lhs_list)), x, mods)


def _final_norm_body(x_ref, g_ref, op_ref, os_ref, *, pblocks):
    x = x_ref[...]
    inv = lax.rsqrt(jnp.mean(x * x, axis=-1, keepdims=True) + EPS)
    y = x * inv * g_ref[...]
    i = pl.program_id(0)

    @pl.when(i < pblocks)
    def _():
        op_ref[...] = y

    @pl.when(i >= pblocks)
    def _():
        os_ref[...] = y


def _final_norm(cfg, x, g, tm=512):
    n = x.shape[0]
    npr = cfg.pc * cfg.ch
    pblocks = npr // tm
    return pl.pallas_call(
        functools.partial(_final_norm_body, pblocks=pblocks),
        grid=(n // tm,),
        in_specs=[pl.BlockSpec((tm, D_MODEL), lambda i: (i, 0)),
                  pl.BlockSpec((1, D_MODEL), lambda i: (0, 0))],
        out_specs=[pl.BlockSpec((tm, D_MODEL), lambda i: (jnp.minimum(i, pblocks - 1), 0)),
                   pl.BlockSpec((tm, D_MODEL), lambda i: (jnp.maximum(i - pblocks, 0), 0))],
        out_shape=[jax.ShapeDtypeStruct((npr, D_MODEL), F32),
                   jax.ShapeDtypeStruct((n - npr, D_MODEL), F32)],
        compiler_params=_params(("arbitrary",)),
        name="final_norm",
    )(x, g.reshape(1, D_MODEL))


def _t5_bucket_np(rel):
    nb = REL_BUCKETS // 2
    max_exact = nb // 2
    rel = np.asarray(rel, np.int64)
    base = np.where(rel > 0, nb, 0)
    n = np.abs(rel)
    nf = np.maximum(n, 1).astype(np.float64)
    large = max_exact + (np.log(nf / max_exact) / math.log(REL_MAX_DIST / max_exact)
                         * (nb - max_exact)).astype(np.int64)
    large = np.minimum(large, nb - 1)
    return (base + np.where(n < max_exact, n, large)).astype(np.int32)


EXPAND_ROWS = 32
EXPAND_UNROLL_MAX = 64


def _expand_body(tab_ref, idx_ref, o_ref, *, nb):
    t = pl.program_id(1)

    def chunk(r, carry):
        rows = pl.ds(pl.multiple_of(r * EXPAND_ROWS, EXPAND_ROWS), EXPAND_ROWS)
        idx = idx_ref[0, rows, :]
        pick = lambda b, acc: jnp.where(idx == b, tab_ref[t, b], acc)
        o_ref[0, 0, rows, :] = lax.fori_loop(0, nb, pick, jnp.zeros(idx.shape, F32),
                                             unroll=True if nb <= EXPAND_UNROLL_MAX else 8)
        return carry
    lax.fori_loop(0, idx_ref.shape[1] // EXPAND_ROWS, chunk, 0)


def _expand(tab, idx):
    ntab, nb = tab.shape
    ni, r, c = idx.shape
    assert r % EXPAND_ROWS == 0 and idx.min() >= 0 and idx.max() < nb
    return pl.pallas_call(
        functools.partial(_expand_body, nb=nb),
        grid=(ni, ntab),
        in_specs=[pl.BlockSpec(memory_space=pltpu.SMEM),
                  pl.BlockSpec((1, r, c), lambda i, t: (i, 0, 0))],
        out_specs=pl.BlockSpec((1, 1, r, c), lambda i, t: (i, t, 0, 0)),
        out_shape=jax.ShapeDtypeStruct((ni, ntab, r, c), F32),
        compiler_params=_params(("arbitrary", "arbitrary")),
        name="expand_table",
    )(tab, jnp.asarray(idx.astype(np.int32)))


def _banded_heads(heads, scores, finish):
    s_next = scores(0)
    for h in range(heads):
        s_cur = s_next
        if h + 1 < heads:
            s_next = scores(h + 1)
        finish(h, s_cur)


DILF_BQ = 128
DILF_GROUP = 4


def _dilf_flags(cfg):
    nchunks = cfg.pc + cfg.sb
    prev = np.array([1 if 0 < c < cfg.pc else 0 for c in range(nchunks)], np.int32)
    nxt = np.array([1 if c < cfg.pc - 1 else 0 for c in range(nchunks)], np.int32)
    return jnp.asarray(prev), jnp.asarray(nxt)


def _dilf_scores(q, k, bias, valid):
    s = lax.dot_general(q, k, (((1,), (1,)), ((), ())), preferred_element_type=F32)
    return jnp.where(valid, s + bias, NEG)


def _dilf_finish(s, v):
    m = jnp.max(s, axis=-1, keepdims=True)
    e = jnp.exp(s - m)
    den = jnp.sum(e, axis=-1, keepdims=True)
    o = jnp.dot(e.astype(BF16), v, preferred_element_type=F32)
    return o / den, m + jnp.log(den)


def _pipelined(tasks):
    s_next = tasks[0][0]()
    for t, (_, finish) in enumerate(tasks):
        s_cur = s_next
        if t + 1 < len(tasks):
            s_next = tasks[t + 1][0]()
        finish(s_cur)


def _dilf_valid(bq, lo, hi):
    nk = bq + 2 * A_HALF
    row = lax.broadcasted_iota(jnp.int32, (bq, nk), 0)
    col = lax.broadcasted_iota(jnp.int32, (bq, nk), 1)
    off = col - row
    return (off >= 0) & (off <= 2 * A_HALF) & (col >= lo) & (col < hi)


def _dilf_body(hp_ref, hn_ref,
               q1_ref, k1p_ref, k1_ref, k1n_ref, v1p_ref, v1_ref, v1n_ref,
               q2_ref, k2p_ref, k2_ref, k2n_ref, v2p_ref, v2_ref, v2n_ref,
               q3_ref, k3p_ref, k3_ref, k3n_ref, v3p_ref, v3_ref, v3n_ref,
               b1_ref, b2_ref, b3_ref, o_ref, oacc, lacc, *, ch):
    c = pl.program_id(0)
    lo0 = jnp.where(hp_ref[c] != 0, 0, A_HALF)
    hi_cut = jnp.where(hn_ref[c] != 0, 0, A_HALF)

    def block_valid(bq, first, last):
        nk = bq + 2 * A_HALF
        return _dilf_valid(bq, lo0 if first else 0, nk - hi_cut if last else nk)

    def keep(g, rows, o, lse):
        oacc[g, rows, :] = o
        lacc[g, rows, :] = jnp.broadcast_to(lse, o.shape)

    def task(g, q_rows, k, v, bias_ref, valid, out_rows):
        scores = lambda: _dilf_scores(q_rows(), k, bias_ref[0], valid)
        finish = lambda s: keep(g, out_rows, *_dilf_finish(s, v))
        return scores, finish

    bq = DILF_BQ
    tasks = []
    nblk = ch // bq
    k = jnp.concatenate([k1p_ref[...], k1_ref[...], k1n_ref[...]], axis=0)
    v = jnp.concatenate([v1p_ref[...], v1_ref[...], v1n_ref[...]], axis=0)
    for b in range(nblk):
        rows = slice(b * bq, (b + 1) * bq)
        keys = slice(b * bq, (b + 1) * bq + 2 * A_HALF)
        tasks.append(task(0, lambda rows=rows: q1_ref[rows, :], k[keys], v[keys], b1_ref,
                          block_valid(bq, b == 0, b == nblk - 1), rows))
    dil = A_CONFIGS[1][1]
    lc = ch // dil
    nblk = lc // bq
    for r in range(dil):
        sub = lambda ref, n: ref[pl.ds(r, n, stride=dil), :].astype(BF16)
        k = jnp.concatenate([sub(k2p_ref, A_HALF), sub(k2_ref, lc), sub(k2n_ref, A_HALF)], axis=0)
        v = jnp.concatenate([sub(v2p_ref, A_HALF), sub(v2_ref, lc), sub(v2n_ref, A_HALF)], axis=0)
        for b in range(nblk):
            rows = pl.ds(r + b * bq * dil, bq, stride=dil)
            keys = slice(b * bq, (b + 1) * bq + 2 * A_HALF)
            tasks.append(task(1, lambda rows=rows: q2_ref[rows, :].astype(BF16), k[keys], v[keys],
                              b2_ref, block_valid(bq, b == 0, b == nblk - 1), rows))
    _pipelined(tasks)

    dil3 = A_CONFIGS[2][1]
    lc3 = ch // dil3
    valid3 = block_valid(lc3, True, True)

    def residues(i, carry):
        group = []
        for u in range(DILF_GROUP):
            r = i * DILF_GROUP + u
            sub = lambda ref, n, r=r: ref[pl.ds(r, n, stride=dil3), :].astype(BF16)
            k = jnp.concatenate([sub(k3p_ref, A_HALF), sub(k3_ref, lc3), sub(k3n_ref, A_HALF)],
                                axis=0)
            v = jnp.concatenate([sub(v3p_ref, A_HALF), sub(v3_ref, lc3), sub(v3n_ref, A_HALF)],
                                axis=0)
            rows = pl.ds(r, lc3, stride=dil3)
            group.append(task(2, lambda rows=rows: q3_ref[rows, :].astype(BF16), k, v, b3_ref,
                              valid3, rows))
        _pipelined(group)
        return carry
    lax.fori_loop(0, dil3 // DILF_GROUP, residues, 0)

    def combine(i, carry):
        rows = pl.ds(pl.multiple_of(i * DILF_BQ, DILF_BQ), DILF_BQ)
        l0, l1, l2 = lacc[0, rows, :], lacc[1, rows, :], lacc[2, rows, :]
        m = jnp.maximum(jnp.maximum(l0, l1), l2)
        e0, e1, e2 = jnp.exp(l0 - m), jnp.exp(l1 - m), jnp.exp(l2 - m)
        tot = e0 + e1 + e2
        out = (e0 / tot) * oacc[0, rows, :] + (e1 / tot) * oacc[1, rows, :] \
            + (e2 / tot) * oacc[2, rows, :]
        o_ref[rows, :] = out.astype(o_ref.dtype)
        return carry
    lax.fori_loop(0, ch // DILF_BQ, combine, 0)


def _dilated_fused(cfg, p1, p23, biases):
    n = cfg.n
    ch = cfg.ch
    hp, hn = _dilf_flags(cfg)
    nh = A_HEADS
    assert ch // A_CONFIGS[2][1] == 2 * A_HALF and ch % DILF_BQ == 0

    def specs(dil, qb, kb, vb):
        halo = A_HALF * dil
        per = ch // halo
        last = n // halo - 1
        own = lambda cb: pl.BlockSpec((ch, HEAD_DIM), lambda c, h, p, x: (c, cb + h))
        before = lambda cb: pl.BlockSpec(
            (halo, HEAD_DIM), lambda c, h, p, x: (jnp.maximum(c * per - 1, 0), cb + h))
        after = lambda cb: pl.BlockSpec(
            (halo, HEAD_DIM), lambda c, h, p, x: (jnp.minimum((c + 1) * per, last), cb + h))
        return [own(qb), before(kb), own(kb), after(kb), before(vb), own(vb), after(vb)]

    bias_spec = lambda b: pl.BlockSpec((1,) + b.shape[1:], lambda c, h, p, x: (h, 0, 0))
    grid_spec = pltpu.PrefetchScalarGridSpec(
        num_scalar_prefetch=2,
        grid=(n // ch, nh),
        in_specs=(specs(1, 0, nh, 2 * nh) + specs(A_CONFIGS[1][1], 0, 2 * nh, 4 * nh)
                  + specs(A_CONFIGS[2][1], nh, 3 * nh, 5 * nh)
                  + [bias_spec(b) for b in biases]),
        out_specs=pl.BlockSpec((ch, HEAD_DIM), lambda c, h, p, x: (c, h)),
        scratch_shapes=[pltpu.VMEM((3, ch, HEAD_DIM), F32), pltpu.VMEM((3, ch, HEAD_DIM), F32)],
    )
    return pl.pallas_call(
        functools.partial(_dilf_body, ch=ch),
        grid_spec=grid_spec,
        out_shape=jax.ShapeDtypeStruct((n, nh * HEAD_DIM), BF16),
        compiler_params=_params(("arbitrary", "arbitrary")),
        name="dilated_fused",
    )(hp, hn, *([p1] * 7), *([p23] * 14), *biases)


def _dilf_bias(rel_bias, g, dil, bq):
    off = np.arange(bq + 2 * A_HALF)[None, :] - np.arange(bq)[:, None] - A_HALF
    bucket = _t5_bucket_np(dil * np.clip(off, -A_HALF, A_HALF))
    return _expand(rel_bias[:, g * A_HEADS:(g + 1) * A_HEADS].T, bucket[None])[0]


def _flash_worklist(cfg, tq, tk, rel_lo=None, rel_hi=None):
    qb, kb, tile, flags = [], [], [], []
    for start, length in cfg.seqs():
        assert length % tq == 0 and length % tk == 0 and start % tq == 0 and start % tk == 0
        nk = length // tk
        for qi in range(length // tq):
            for kj in range(nk):
                qb.append(start // tq + qi)
                kb.append(start // tk + kj)
                if rel_lo is not None:
                    assert (kj * tk) % tq == 0
                    d = (kj * tk - qi * tq) // tq
                    tile.append(min(max(d, rel_lo), rel_hi) - rel_lo)
                    far = 4 if (d <= rel_lo or d >= rel_hi) else 0
                else:
                    tile.append(0)
                    far = 0
                flags.append((1 if kj == 0 else 0) | (2 if kj == nk - 1 else 0) | far)
    as_i32 = lambda a: jnp.asarray(np.asarray(a, np.int32))
    return as_i32(qb), as_i32(kb), as_i32(tile), as_i32(flags)


def _flash_init(flags, m_scr, acc_scr):
    @pl.when((flags & 1) != 0)
    def _():
        m_scr[...] = jnp.full(m_scr.shape, -jnp.inf, F32)
        acc_scr[...] = jnp.zeros(acc_scr.shape, F32)


def _flash_softmax(h, s, m_scr, shift=None):
    m_prev = m_scr[h]
    m_tile = jnp.max(s, axis=0, keepdims=True)
    if shift is not None:
        m_tile = m_tile + shift
    m_new = jnp.maximum(m_prev, m_tile)
    m_scr[h] = m_new
    m_ref = m_new if shift is None else m_new - shift
    return jnp.exp2(m_prev - m_new), jnp.exp2(s - m_ref).astype(BF16)


def _flash_accumulate(h, alpha, p, vt, acc_scr):
    lhs = jnp.concatenate([vt, jnp.ones((ONES_ROWS, vt.shape[1]), BF16)], axis=0)
    acc_scr[h] = alpha * acc_scr[h] + jnp.dot(lhs, p, preferred_element_type=F32)


def _flash_heads(heads, scores, vt_rows, m_scr, acc_scr, shift=None):
    s_next = scores(0)
    pending = None
    for h in range(heads):
        s_cur = s_next
        if h + 1 < heads:
            s_next = scores(h + 1)
        alpha, p = _flash_softmax(h, s_cur, m_scr, None if shift is None else shift(h))
        if pending is not None:
            _flash_accumulate(*pending, vt_rows(pending[0]), acc_scr)
        pending = (h, alpha, p)
    _flash_accumulate(*pending, vt_rows(pending[0]), acc_scr)


DIFF_TQ = 512
DIFF_TK = 512


def _diff_tile_range(max_len):
    rel = np.arange(-max_len + 1, max_len)
    b = _t5_bucket_np(rel)
    sat_pos = int(rel[b != b[-1]].max()) + 1
    sat_neg = int(rel[b != b[0]].min()) - 1
    hi = -(-(sat_pos + DIFF_TQ - 1) // DIFF_TQ)
    lo = (sat_neg - (DIFF_TK - 1)) // DIFF_TQ
    return lo, hi


def _diff_bias_tiles(rel_bias, lo, hi):
    d = np.arange(lo, hi + 1)[:, None, None] * DIFF_TQ
    rel = d + np.arange(DIFF_TK)[None, :, None] - np.arange(DIFF_TQ)[None, None, :]
    return _expand(rel_bias[:, 3 * A_HEADS:].T * LOG2E, _t5_bucket_np(rel))


def _diff_body(qb_ref, kb_ref, tile_ref, fl_ref, q_ref, k_ref, vt_ref, bias_ref, lam_ref, g_ref,
               o_ref, m_scr, acc_scr, *, lam_init):
    flags = fl_ref[pl.program_id(0)]
    tq = DIFF_TQ
    _flash_init(flags, m_scr, acc_scr)

    lane = lax.broadcasted_iota(jnp.int32, (tq, HEAD_DIM), 1)

    def raw_scores(h):
        cols = slice(h * HEAD_DIM, (h + 1) * HEAD_DIM)
        qh = q_ref[:, cols]
        zero = jnp.zeros_like(qh)
        q2 = jnp.concatenate([jnp.where(lane < B_QK_DIM, qh, zero),
                              jnp.where(lane >= B_QK_DIM, qh, zero)], axis=0)
        return lax.dot_general(k_ref[:, cols], q2, (((1,), (1,)), ((), ())),
                               preferred_element_type=F32)

    def biased_scores(h):
        b = bias_ref[0, h]
        return raw_scores(h) + jnp.concatenate([b, b], axis=1)

    vt_rows = lambda h: vt_ref[h * HEAD_DIM:(h + 1) * HEAD_DIM, :]
    far = (flags & 4) != 0

    @pl.when(far)
    def _():
        _flash_heads(B_HEADS, raw_scores, vt_rows, m_scr, acc_scr,
                     shift=lambda h: bias_ref[0, h, 0:1, 0:1])

    @pl.when(jnp.logical_not(far))
    def _():
        _flash_heads(B_HEADS, biased_scores, vt_rows, m_scr, acc_scr)

    @pl.when((flags & 2) != 0)
    def _():
        lf = lam_ref[...]
        lam = (jnp.exp(jnp.sum(lf[0:1] * lf[1:2], axis=-1, keepdims=True))
               - jnp.exp(jnp.sum(lf[2:3] * lf[3:4], axis=-1, keepdims=True)) + lam_init)
        for h in range(B_HEADS):
            cols = slice(h * HEAD_DIM, (h + 1) * HEAD_DIM)
            acc = acc_scr[h]
            att = acc[:HEAD_DIM] / acc[HEAD_DIM:HEAD_DIM + 1]
            o = att[:, :tq] - lam * att[:, tq:]
            inv = lax.rsqrt(jnp.mean(o * o, axis=0, keepdims=True) + EPS)
            o_ref[:, cols] = (((o * inv).T * g_ref[...]) * (1.0 - lam_init)).astype(o_ref.dtype)


def _diff_attention(cfg, proj, qcol, kcol, vt, bias_tiles, tile_lo, tile_hi, lam, subln_g,
                    lam_init):
    n = cfg.n
    tq, tk = DIFF_TQ, DIFF_TK
    qb, kb, tile, flags = _flash_worklist(cfg, tq, tk, tile_lo, tile_hi)
    grid_spec = pltpu.PrefetchScalarGridSpec(
        num_scalar_prefetch=4,
        grid=(qb.shape[0],),
        in_specs=[pl.BlockSpec((tq, B_W), lambda s, q, k, t, f: (q[s], qcol)),
                  pl.BlockSpec((tk, B_W), lambda s, q, k, t, f: (k[s], kcol)),
                  pl.BlockSpec((B_W, tk), lambda s, q, k, t, f: (0, k[s])),
                  pl.BlockSpec((1, B_HEADS, tk, tq), lambda s, q, k, t, f: (t[s], 0, 0, 0)),
                  pl.BlockSpec((4, B_QK_DIM), lambda s, q, k, t, f: (0, 0)),
                  pl.BlockSpec((1, HEAD_DIM), lambda s, q, k, t, f: (0, 0))],
        out_specs=pl.BlockSpec((tq, B_W), lambda s, q, k, t, f: (q[s], 0)),
        scratch_shapes=[pltpu.VMEM((B_HEADS, 1, 2 * tq), F32),
                        pltpu.VMEM((B_HEADS, HEAD_DIM + ONES_ROWS, 2 * tq), F32)],
    )
    return pl.pallas_call(
        functools.partial(_diff_body, lam_init=lam_init),
        grid_spec=grid_spec,
        out_shape=jax.ShapeDtypeStruct((n, B_W), BF16),
        compiler_params=_params(("arbitrary",)),
        name="diff_attention",
    )(qb, kb, tile, flags, proj, proj, vt, bias_tiles, lam, subln_g.reshape(1, HEAD_DIM))


MLA_TM = 512
MLA_TQ = 1024
MLA_TK = 512


def _rope_tables(max_len):
    inv = ROPE_THETA ** (-jnp.arange(0, C_ROPE, 2, dtype=F32) / C_ROPE)
    ang = jnp.arange(max_len, dtype=F32)[:, None] * inv[None, :]
    cos, sin = jnp.cos(ang), jnp.sin(ang)
    zero = jnp.zeros((max_len, HEAD_DIM - C_ROPE), F32)
    return (jnp.concatenate([cos, cos, zero], axis=1),
            jnp.concatenate([-sin, sin, zero], axis=1))


def _mla_prep_body(p_ref, gq_ref, gkv_ref, wqa_ref, wqb_ref, wk_ref, wvt_ref, cos_ref, sin_ref,
                   q_ref, k_ref, vt_ref, *, scale):
    def normed(x, g):
        inv = lax.rsqrt(jnp.mean(x * x, axis=-1, keepdims=True) + EPS)
        return (x * inv * g).astype(BF16)

    cq = normed(p_ref[:, 0:C_LORA], gq_ref[...])
    ckv = normed(p_ref[:, C_LORA:2 * C_LORA], gkv_ref[...])
    cos = cos_ref[...]
    sin = sin_ref[...]
    qa = jnp.dot(cq, wqa_ref[...], preferred_element_type=F32)
    qb = jnp.dot(cq, wqb_ref[...], preferred_element_type=F32)
    kn = jnp.dot(ckv, wk_ref[...], preferred_element_type=F32)
    vt_ref[0] = lax.dot_general(wvt_ref[...], ckv, (((1,), (1,)), ((), ())),
                                preferred_element_type=F32).astype(BF16)
    kr = p_ref[:, 2 * C_LORA:2 * C_LORA + HEAD_DIM]
    kr_sw = p_ref[:, 2 * C_LORA + HEAD_DIM:2 * C_LORA + 2 * HEAD_DIM]
    k_rope = (kr * cos + kr_sw * sin).astype(BF16)
    for h in range(C_HEADS):
        a0 = h * C_QK_PAD
        a1 = a0 + HEAD_DIM
        a2 = a0 + C_QK_PAD
        b = slice(h * HEAD_DIM, (h + 1) * HEAD_DIM)
        q_ref[:, a0:a1] = (qa[:, a0:a1] * scale).astype(BF16)
        q_ref[:, a1:a2] = ((qa[:, a1:a2] * cos + qb[:, b] * sin) * scale).astype(BF16)
        k_ref[:, a0:a1] = kn[:, b].astype(BF16)
        k_ref[:, a1:a2] = k_rope


def _mla_prep(cfg, proj_c, gq, gkv, wqa, wqb, wk, wvt, cos_tab, sin_tab):
    n = cfg.n
    tm = MLA_TM
    pblocks = cfg.pc * cfg.ch // tm
    per = cfg.ch // tm
    pos = lambda i: (jnp.where(i < pblocks, i, (i - pblocks) % per), 0)
    full = lambda shape: pl.BlockSpec(shape, lambda i: (0, 0))
    qk_w = C_HEADS * C_QK_PAD
    v_w = C_HEADS * HEAD_DIM
    return pl.pallas_call(
        functools.partial(_mla_prep_body, scale=(C_NOPE + C_ROPE) ** -0.5 * LOG2E),
        grid=(n // tm,),
        in_specs=[pl.BlockSpec((tm, proj_c.shape[1]), lambda i: (i, 0)),
                  full((1, C_LORA)), full((1, C_LORA)),
                  full(wqa.shape), full(wqb.shape), full(wk.shape), full(wvt.shape),
                  pl.BlockSpec((tm, HEAD_DIM), pos), pl.BlockSpec((tm, HEAD_DIM), pos)],
        out_specs=[pl.BlockSpec((tm, qk_w), lambda i: (i, 0)),
                   pl.BlockSpec((tm, qk_w), lambda i: (i, 0)),
                   pl.BlockSpec((1, v_w, tm), lambda i: (i, 0, 0))],
        out_shape=[jax.ShapeDtypeStruct((n, qk_w), BF16),
                   jax.ShapeDtypeStruct((n, qk_w), BF16),
                   jax.ShapeDtypeStruct((n // tm, v_w, tm), BF16)],
        compiler_params=_params(("arbitrary",)),
        name="mla_prep",
    )(proj_c, gq.reshape(1, C_LORA), gkv.reshape(1, C_LORA), wqa, wqb, wk, wvt, cos_tab, sin_tab)


def _mla_body(qb_ref, kb_ref, tile_ref, fl_ref, q_ref, k_ref, vt_ref, o_ref, m_scr, acc_scr):
    flags = fl_ref[pl.program_id(0)]
    _flash_init(flags, m_scr, acc_scr)

    def scores(h):
        qk = slice(h * C_QK_PAD, (h + 1) * C_QK_PAD)
        return lax.dot_general(k_ref[:, qk], q_ref[:, qk], (((1,), (1,)), ((), ())),
                               preferred_element_type=F32)

    vt_rows = lambda h: vt_ref[0, h * HEAD_DIM:(h + 1) * HEAD_DIM, :]
    _flash_heads(C_HEADS, scores, vt_rows, m_scr, acc_scr)

    @pl.when((flags & 2) != 0)
    def _():
        for h in range(C_HEADS):
            vc = slice(h * HEAD_DIM, (h + 1) * HEAD_DIM)
            acc = acc_scr[h]
            o_ref[:, vc] = (acc[:HEAD_DIM] / acc[HEAD_DIM:HEAD_DIM + 1]).T.astype(o_ref.dtype)


def _mla_attention(cfg, q, k, vt):
    n = cfg.n
    tq, tk = MLA_TQ, MLA_TK
    qb, kb, tile, flags = _flash_worklist(cfg, tq, tk)
    qk_w = C_HEADS * C_QK_PAD
    v_w = C_HEADS * HEAD_DIM
    grid_spec = pltpu.PrefetchScalarGridSpec(
        num_scalar_prefetch=4,
        grid=(qb.shape[0],),
        in_specs=[pl.BlockSpec((tq, qk_w), lambda s, q_, k_, t, f: (q_[s], 0)),
                  pl.BlockSpec((tk, qk_w), lambda s, q_, k_, t, f: (k_[s], 0)),
                  pl.BlockSpec((1, v_w, tk), lambda s, q_, k_, t, f: (k_[s], 0, 0))],
        out_specs=pl.BlockSpec((tq, v_w), lambda s, q_, k_, t, f: (q_[s], 0)),
        scratch_shapes=[pltpu.VMEM((C_HEADS, 1, tq), F32),
                        pltpu.VMEM((C_HEADS, HEAD_DIM + ONES_ROWS, tq), F32)],
    )
    return pl.pallas_call(
        _mla_body,
        grid_spec=grid_spec,
        out_shape=jax.ShapeDtypeStruct((n, v_w), BF16),
        compiler_params=_params(("arbitrary",)),
        name="mla_attention",
    )(qb, kb, tile, flags, q, k, vt)


def _na_tables(cfg):
    prev, nxt, var = [], [], []
    for start, length in cfg.seqs():
        nb = length // NA_BLOCK
        assert length % NA_BLOCK == 0 and start % NA_BLOCK == 0 and nb >= 3
        b0 = start // NA_BLOCK
        for r in range(nb):
            prev.append(b0 + max(r - 1, 0))
            nxt.append(b0 + min(r + 1, nb - 1))
            var.append(0 if r == 0 else (2 if r == nb - 1 else 1))
    as_i32 = lambda a: jnp.asarray(np.asarray(a, np.int32))
    return as_i32(prev), as_i32(nxt), as_i32(var)


def _na_window_start(a, variant):
    centred = a - NA_ROWS // 2
    return (max(centred, 0), centred, min(centred, 0))[variant]


def _na_bias_tables(rpb):
    ncol = 2 * NA_COLS - 1
    c = np.arange(GRID_W)[:, None, None]
    b = np.arange(NA_ROWS)[None, :, None]
    kc = np.arange(GRID_W)[None, None, :]
    cstart = np.clip(c - NA_COLS // 2, 0, GRID_W - NA_COLS)
    col_ok = (kc >= cstart) & (kc < cstart + NA_COLS)
    dc = np.clip(kc - c, -(NA_COLS - 1), NA_COLS - 1) + (NA_COLS - 1)
    masked = NA_ROWS * ncol
    idx = np.where(col_ok, b * ncol + dc, masked).reshape(1, GRID_W, NA_BLOCK)
    rows = np.arange(NA_ROWS)[:, None] + np.arange(NA_ROWS)[None, :]
    tab = rpb[:, rows, :].reshape(D_HEADS * NA_ROWS, NA_ROWS * ncol)
    tab = jnp.concatenate([tab, jnp.full((tab.shape[0], 1), NEG, F32)], axis=1)
    slabs = _expand(tab, idx)[0].reshape(D_HEADS, NA_ROWS, GRID_W, NA_BLOCK)
    variants = []
    for variant in range(3):
        row_blocks = []
        for a in range(NA_ROWS):
            start = _na_window_start(a, variant)
            left = (NA_ROWS + start) * GRID_W
            row_blocks.append(jnp.pad(slabs[:, start - a + NA_ROWS - 1],
                                      ((0, 0), (0, 0), (left, 2 * NA_BLOCK - left)),
                                      constant_values=NEG))
        variants.append(jnp.concatenate(row_blocks, axis=1))
    return jnp.stack(variants, axis=0)


def _na_body(prev_ref, nxt_ref, var_ref, q_ref, kp_ref, km_ref, kn_ref, vp_ref, vm_ref, vn_ref,
             tab_ref, o_ref):
    k = jnp.concatenate([kp_ref[...], km_ref[...], kn_ref[...]], axis=0)
    v = jnp.concatenate([vp_ref[...], vm_ref[...], vn_ref[...]], axis=0)
    def scores(h):
        cols = slice(h * HEAD_DIM, (h + 1) * HEAD_DIM)
        s = lax.dot_general(q_ref[:, cols], k[:, cols], (((1,), (1,)), ((), ())),
                            preferred_element_type=F32)
        return s + tab_ref[0, h]

    def finish(h, s):
        cols = slice(h * HEAD_DIM, (h + 1) * HEAD_DIM)
        m = jnp.max(s, axis=-1, keepdims=True)
        e = jnp.exp(s - m)
        den = jnp.sum(e, axis=-1, keepdims=True)
        o = jnp.dot(e.astype(BF16), v[:, cols], preferred_element_type=F32)
        o_ref[:, cols] = (o / den).astype(o_ref.dtype)

    _banded_heads(D_HEADS, scores, finish)


def _na_attention(cfg, qkv, tabs):
    n = cfg.n
    w = D_HEADS * HEAD_DIM
    prev, nxt, var = _na_tables(cfg)
    own = lambda cb: pl.BlockSpec((NA_BLOCK, w), lambda i, p, x, t: (i, cb))
    before = lambda cb: pl.BlockSpec((NA_BLOCK, w), lambda i, p, x, t: (p[i], cb))
    after = lambda cb: pl.BlockSpec((NA_BLOCK, w), lambda i, p, x, t: (x[i], cb))
    grid_spec = pltpu.PrefetchScalarGridSpec(
        num_scalar_prefetch=3,
        grid=(n // NA_BLOCK,),
        in_specs=[own(0), before(1), own(1), after(1), before(2), own(2), after(2),
                  pl.BlockSpec((1, D_HEADS, NA_BLOCK, 3 * NA_BLOCK),
                               lambda i, p, x, t: (t[i], 0, 0, 0))],
        out_specs=pl.BlockSpec((NA_BLOCK, w), lambda i, p, x, t: (i, 0)),
    )
    return pl.pallas_call(
        _na_body,
        grid_spec=grid_spec,
        out_shape=jax.ShapeDtypeStruct((n, w), BF16),
        compiler_params=_params(("arbitrary",)),
        name="na_attention",
    )(prev, nxt, var, qkv, qkv, qkv, qkv, qkv, qkv, qkv, tabs)


EVEN_BLOCKS_ROWMAJOR = (0, 3, 6, 9, 10)
EVEN_BLOCKS_STRIDED = (1, 2, 4, 5, 7, 8)
EVEN_BLOCK_VT = 11


def _even_weights(w_in):
    blk = lambda b: w_in[:, b * B_W:(b + 1) * B_W]
    w_a = jnp.concatenate([blk(b) for b in EVEN_BLOCKS_ROWMAJOR], axis=1)
    w_b = jnp.concatenate([blk(b) for b in EVEN_BLOCKS_STRIDED], axis=1)
    return w_a, w_b, blk(EVEN_BLOCK_VT).T


def _even_col_scales():
    cs_a = np.ones((1, len(EVEN_BLOCKS_ROWMAJOR) * B_W), np.float32)
    cs_a[0, :B_W] = HEAD_DIM ** -0.5
    cs_a[0, 3 * B_W:4 * B_W] = B_QK_DIM ** -0.5 * LOG2E
    cs_b = np.ones((1, len(EVEN_BLOCKS_STRIDED) * B_W), np.float32)
    cs_b[0, :2 * B_W] = HEAD_DIM ** -0.5
    return jnp.asarray(cs_a), jnp.asarray(cs_b)


def _even_tables(cfg, rel_bias):
    dil = [_dilf_bias(rel_bias, gi, d, min(DILF_BQ, cfg.ch // d))
           for gi, (_, d) in enumerate(A_CONFIGS)]
    tile_lo, tile_hi = _diff_tile_range(cfg.pc * cfg.ch)
    return dil, _diff_bias_tiles(rel_bias, tile_lo, tile_hi), tile_lo, tile_hi


def _even_mixer(cfg, x, g, mods, weights, w_out, lam, subln_g, tables, lam_init, tm):
    w_a, w_b, wvt = weights
    dil_bias, diff_tiles, tile_lo, tile_hi = tables
    cs_a, cs_b = _even_col_scales()
    p1 = _norm_matmul(cfg, x, g, mods, 1, w_a, cs_a, BF16, tm, 1024, "even_in_a")
    p23 = _norm_matmul(cfg, x, g, mods, 1, w_b, cs_b, F32, tm, 1024, "even_in_b")
    vt = _norm_matmul_t(cfg, x, g, mods, 1, wvt, tm, "even_in_vt")
    o_a = _dilated_fused(cfg, p1, p23, dil_bias)
    o_b = _diff_attention(cfg, p1, 3, 4, vt, diff_tiles, tile_lo, tile_hi, lam, subln_g, lam_init)
    return _resid_matmul(cfg, [o_a, o_b], w_out, x, mods, 1, 1.0, tm, 1024, "even_out")


def _odd_weights(w_in, w_q_up, w_kv_up):
    half = C_ROPE // 2
    swap = np.concatenate([np.arange(half, C_ROPE), np.arange(half)])
    o2 = 2 * C_LORA
    zpad = jnp.zeros((D_MODEL, HEAD_DIM - C_ROPE), w_in.dtype)
    kr = w_in[:, o2:o2 + C_ROPE]
    w_c = jnp.concatenate([w_in[:, :o2], kr, zpad, kr[:, swap], zpad], axis=1)
    w_d = w_in[:, o2 + C_ROPE:]
    q3 = w_q_up.reshape(C_LORA, C_HEADS, C_NOPE + C_ROPE)
    zq = jnp.zeros((C_LORA, C_HEADS, C_QK_PAD - C_NOPE - C_ROPE), w_q_up.dtype)
    wqa = jnp.concatenate([q3, zq], axis=2).reshape(C_LORA, C_HEADS * C_QK_PAD)
    wqb = jnp.concatenate([q3[:, :, C_NOPE:][:, :, swap], zq], axis=2).reshape(
        C_LORA, C_HEADS * HEAD_DIM)
    kv3 = w_kv_up.reshape(C_LORA, C_HEADS, 2 * HEAD_DIM)
    wk = kv3[:, :, :C_NOPE].reshape(C_LORA, C_HEADS * HEAD_DIM)
    wvt = kv3[:, :, C_NOPE:].reshape(C_LORA, C_HEADS * HEAD_DIM).T
    return w_c, w_d, wqa, wqb, wk, wvt


def _odd_mixer(cfg, x, g, mods, weights, gq, gkv, rpb, w_out, rope, tm):
    w_c, w_d, wqa, wqb, wk, wvt = weights
    ones_c = jnp.ones((1, w_c.shape[1]), F32)
    proj_c = _norm_matmul(cfg, x, g, mods, 1, w_c, ones_c, F32, tm, w_c.shape[1], "odd_in_latent")
    cs = np.ones((1, w_d.shape[1]), np.float32)
    cs[0, :D_HEADS * HEAD_DIM] = HEAD_DIM ** -0.5
    qkv_d = _norm_matmul(cfg, x, g, mods, 1, w_d, jnp.asarray(cs), BF16, tm, w_d.shape[1],
                         "odd_in_na")
    q, k, vt = _mla_prep(cfg, proj_c, gq, gkv, wqa, wqb, wk, wvt, *rope)
    o_c = _mla_attention(cfg, q, k, vt)
    o_d = _na_attention(cfg, qkv_d, _na_bias_tables(rpb))
    return _resid_matmul(cfg, [o_c, o_d], w_out, x, mods, 1, 1.0, tm, 1024, "odd_out")


def _trunk(cfg, x, c_pad, ada_w, ada_b, norm_g, ffn_w_in, ffn_w_out, rel_bias, ev_w_in, ev_w_out,
           diff_lambda, diff_subln_g, od_w_in, mla_q_norm_g, mla_kv_norm_g, mla_w_q_up,
           mla_w_kv_up, na_rpb, od_w_out, final_norm_g):
    depth = ada_w.shape[0]
    tm = 1024
    nseq = 1 + cfg.sb
    mod_all = _modulation(c_pad, ada_w, ada_b)[:, :nseq]
    chunk_seq = np.array([0] * cfg.pc + list(range(1, nseq)))
    mod_all = mod_all[:, chunk_seq].reshape(depth, (cfg.pc + cfg.sb) * 9, 1, D_MODEL)
    rope = _rope_tables(cfg.pc * cfg.ch)
    even_tables = _even_tables(cfg, rel_bias)
    for i in range(depth):
        mods = mod_all[i]
        j = i // 2
        hid = _norm_swiglu(cfg, x, norm_g[i, 0], mods, 0, ffn_w_in[i, 0].astype(BF16), tm, 512)
        x = _resid_matmul(cfg, [hid], ffn_w_out[i, 0].astype(BF16), x, mods, 0, 0.5, tm, 512,
                          "ffn_out")
        if i % 2 == 0:
            x = _even_mixer(cfg, x, norm_g[i, 1], mods, _even_weights(ev_w_in[j].astype(BF16)),
                            ev_w_out[j].astype(BF16), diff_lambda[j], diff_subln_g[j], even_tables,
                            0.8 - 0.6 * math.exp(-0.3 * i), tm)
        else:
            weights = tuple(a.astype(BF16) for a in
                            _odd_weights(od_w_in[j], mla_w_q_up[j], mla_w_kv_up[j]))
            x = _odd_mixer(cfg, x, norm_g[i, 1], mods, weights, mla_q_norm_g[j], mla_kv_norm_g[j],
                           na_rpb[j], od_w_out[j].astype(BF16), rope, tm)
        hid = _norm_swiglu(cfg, x, norm_g[i, 2], mods, 2, ffn_w_in[i, 1].astype(BF16), tm, 512)
        x = _resid_matmul(cfg, [hid], ffn_w_out[i, 1].astype(BF16), x, mods, 2, 0.5, tm, 512,
                          "ffn_out")
    return _final_norm(cfg, x, final_norm_g)


def kernel(x_prompt, x_sample, c_prompt, c_sample, ada_w, ada_b, norm_g, ffn_w_in, ffn_w_out, rel_bias, ev_w_in, ev_w_out, diff_lambda, diff_subln_g, od_w_in, mla_q_norm_g, mla_kv_norm_g, mla_w_q_up, mla_w_kv_up, na_rpb, od_w_out, final_norm_g):
    pb, pt, _ = x_prompt.shape
    sb, st, _ = x_sample.shape
    assert pb == 1 and pt % st == 0
    cfg = Cfg(ch=st, pc=pt // st, sb=sb)
    x = jnp.concatenate([x_prompt.reshape(-1, D_MODEL), x_sample.reshape(-1, D_MODEL)], axis=0)
    c = jnp.concatenate([c_prompt, c_sample], axis=0)
    c_pad = jnp.pad(c, ((0, -c.shape[0] % 8), (0, 0)))
    y_p, y_s = _trunk(cfg, x, c_pad, ada_w, ada_b, norm_g, ffn_w_in, ffn_w_out, rel_bias, ev_w_in,
               ev_w_out, diff_lambda, diff_subln_g, od_w_in, mla_q_norm_g, mla_kv_norm_g,
               mla_w_q_up, mla_w_kv_up, na_rpb, od_w_out, final_norm_g)
    return (y_p.reshape(x_prompt.shape), y_s.reshape(x_sample.shape))
```
